```python
import jax, jax.numpy as jnp
from jax import lax
import numpy as np

D_MODEL = 2048
BATCH = 4
SEQ = 2048
DEPTH = 1
DEC_BATCH = 128
DEC_SEQ = 1
PAST_LEN = 16384
PAGE_SIZE = 128

D_CONV = D_MODEL // 2
CONV_GROUPS = 8
CONV_W = 3
N_HEADS = 8
HEAD_K = 128
HEAD_V = 128
D_REC = N_HEADS * HEAD_K
D_VAL = N_HEADS * HEAD_V
D_MIX = D_CONV + D_VAL
SPLITS = [D_CONV, 2 * D_CONV, 3 * D_CONV, 3 * D_CONV + D_REC, 3 * D_CONV + 2 * D_REC, 3 * D_CONV + 2 * D_REC + D_VAL]
D_IN = 3 * D_CONV + 2 * D_REC + 2 * D_VAL
CHUNK = 64
N_EXPERTS = 32
TOP_K = 4
D_FF = D_MODEL
SWIGLU_LIMIT = 7.0
SWIGLU_ALPHA = 1.702
EXPERT_BLOCK = 128
D_PLE = 256
LN_EPS = 1e-5
RMS_EPS = 1e-6
DEEPNORM_ALPHA = (2 * DEPTH) ** 0.25
DEEPNORM_BETA = (8 * DEPTH) ** -0.25

kernel_name = "hybrid_shortconv_hgrn2_moe_deepnorm_step"


def layer_norm(x, g, b):
    xf = x.astype(jnp.float32)
    mu = xf.mean(-1, keepdims=True)
    var = jnp.square(xf - mu).mean(-1, keepdims=True)
    return ((xf - mu) * lax.rsqrt(var + LN_EPS) * g + b).astype(x.dtype)


def hgrn2_chunked(q, k, v, logf, s0, chunk):
    bsz, t_len, h, hk = q.shape
    hv = v.shape[-1]
    n_chunks = t_len // chunk

    def to_chunks(a):
        return jnp.moveaxis(a.reshape(bsz, n_chunks, chunk, *a.shape[2:]), 1, 0)

    causal = jnp.tril(jnp.ones((chunk, chunk), dtype=bool))[None, :, :, None, None]

    def step(state, xs):
        qc, kc, vc, gc = xs
        b = jnp.cumsum(gc, axis=1)
        o_inter = jnp.einsum('bthk,bhkv->bthv', qc * jnp.exp(b), state)
        diff = jnp.where(causal, b[:, :, None] - b[:, None, :], -jnp.inf)
        scores = jnp.einsum('bthk,bshk,btshk->btsh', qc, kc, jnp.exp(diff))
        o_intra = jnp.einsum('btsh,bshv->bthv', scores, vc)
        b_last = b[:, -1]
        new_state = jnp.exp(b_last)[..., None] * state + jnp.einsum(
            'bshk,bshv->bhkv', kc * jnp.exp(b_last[:, None] - b), vc)
        return new_state, o_inter + o_intra

    s_fin, o = lax.scan(step, s0, (to_chunks(q), to_chunks(k), to_chunks(v), to_chunks(logf)))
    o = jnp.moveaxis(o, 0, 1).reshape(bsz, t_len, h, hv)
    return o, s_fin


def token_mixers(x, conv_buf, rec_state, w_in, conv_w, lb, rms_g, w_out, chunk):
    bsz, t_len, _ = x.shape
    proj = x @ w_in
    c_gate, b_gate, h_in, q, fz, i_val, o_gate = jnp.split(proj, SPLITS, axis=-1)

    u = c_gate * h_in
    up = jnp.concatenate([conv_buf.astype(u.dtype), u], axis=1)
    y_conv = sum(conv_w[j] * up[:, j:j + t_len] for j in range(CONV_W))
    z_conv = b_gate * y_conv
    new_conv = up[:, t_len:]

    fz = fz.astype(jnp.float32).reshape(bsz, t_len, N_HEADS, HEAD_K)
    lb_h = lb.reshape(N_HEADS, HEAD_K)
    f = lb_h + (1.0 - lb_h) * jax.nn.sigmoid(fz)
    logf = jnp.log(f)
    k = (1.0 - lb_h) * jax.nn.sigmoid(-fz)
    qh = q.astype(jnp.float32).reshape(bsz, t_len, N_HEADS, HEAD_K)
    vh = i_val.astype(jnp.float32).reshape(bsz, t_len, N_HEADS, HEAD_V)
    o, new_rec = hgrn2_chunked(qh, k, vh, logf, rec_state.astype(jnp.float32), chunk)
    o = o * lax.rsqrt(jnp.mean(jnp.square(o), axis=-1, keepdims=True) + RMS_EPS)
    o = o * rms_g.reshape(N_HEADS, HEAD_V)
    o = o.reshape(bsz, t_len, D_VAL).astype(x.dtype) * jax.nn.silu(o_gate)

    mix = jnp.concatenate([z_conv, o], axis=-1) @ w_out
    return mix, new_conv, new_rec.astype(rec_state.dtype)


def moe(x, w_router, b_router, w_gate, b_gate, w_up, b_up, w_down, b_down):
    n_tok, d = x.shape
    logits = (x @ w_router + b_router).astype(jnp.float32)
    top_v, top_i = lax.top_k(logits, TOP_K)
    gates = jax.nn.softmax(top_v, axis=-1)
    m = n_tok * TOP_K
    flat_e = top_i.reshape(m)
    flat_g = gates.reshape(m)
    flat_tok = jnp.repeat(jnp.arange(n_tok, dtype=jnp.int32), TOP_K)
    order = jnp.argsort(flat_e, stable=True)
    sorted_e = flat_e[order]
    counts = jnp.bincount(flat_e, length=N_EXPERTS)
    padded = (counts + EXPERT_BLOCK - 1) // EXPERT_BLOCK * EXPERT_BLOCK
    start = jnp.cumsum(counts) - counts
    pad_end = jnp.cumsum(padded)
    pad_start = pad_end - padded
    dest = pad_start[sorted_e] + jnp.arange(m, dtype=jnp.int32) - start[sorted_e]
    n_blocks = (m + EXPERT_BLOCK - 1) // EXPERT_BLOCK + N_EXPERTS
    n_slots = n_blocks * EXPERT_BLOCK
    slot_tok = jnp.full((n_slots,), n_tok, dtype=jnp.int32).at[dest].set(flat_tok[order])
    slot_g = jnp.zeros((n_slots,), jnp.float32).at[dest].set(flat_g[order])
    block_e = jnp.minimum(
        jnp.searchsorted(pad_end, jnp.arange(n_blocks, dtype=jnp.int32) * EXPERT_BLOCK, side='right'),
        N_EXPERTS - 1)
    x_pad = jnp.concatenate([x, jnp.zeros((1, d), x.dtype)], axis=0)
    xb = x_pad[slot_tok].reshape(n_blocks, EXPERT_BLOCK, d)

    def expert_block(args):
        xe, e = args
        g = xe @ w_gate[e] + b_gate[e]
        u = xe @ w_up[e] + b_up[e]
        g = jnp.minimum(g, SWIGLU_LIMIT)
        u = jnp.clip(u, -SWIGLU_LIMIT, SWIGLU_LIMIT)
        hid = g * jax.nn.sigmoid(SWIGLU_ALPHA * g) * (u + 1.0)
        return hid @ w_down[e] + b_down[e]

    yb = lax.map(expert_block, (xb, block_e))
    y = yb.reshape(n_slots, d) * slot_g[:, None].astype(x.dtype)
    return jax.ops.segment_sum(y, slot_tok, num_segments=n_tok + 1)[:n_tok]


def setup_inputs(seed: int = 0) -> dict:
    key = jax.random.key(seed)
    ks = jax.random.split(key, 32)

    def nrm(k, shape, scale):
        return jax.random.normal(k, shape, jnp.float32) * scale

    L, E = DEPTH, N_EXPERTS
    return {
        'x_prompt': nrm(ks[0], (BATCH, SEQ, D_MODEL), 1.0),
        'x_sample': nrm(ks[1], (DEC_BATCH, DEC_SEQ, D_MODEL), 1.0),
        'state_conv': nrm(ks[2], (L, DEC_BATCH, CONV_W - 1, D_CONV), 1.0),
        'state_rec': nrm(ks[3], (L, DEC_BATCH, N_HEADS, HEAD_K, HEAD_V), 0.5),
        'p_prompt': nrm(ks[4], (L, BATCH, SEQ, D_PLE), 1.0),
        'p_sample': nrm(ks[5], (L, DEC_BATCH, DEC_SEQ, D_PLE), 1.0),
        'ln_in_g': 1.0 + nrm(ks[6], (D_MODEL,), 0.01),
        'ln_in_b': nrm(ks[7], (D_MODEL,), 0.01),
        'w_in': nrm(ks[8], (L, D_MODEL, D_IN), D_MODEL ** -0.5),
        'conv_w': nrm(ks[9], (L, CONV_W, D_CONV), CONV_W ** -0.5),
        'lb_theta': nrm(ks[10], (L + 1, D_REC), 1.0),
        'rms_g': 1.0 + nrm(ks[11], (L, D_VAL), 0.01),
        'w_out': nrm(ks[12], (L, D_MIX, D_MODEL), DEEPNORM_BETA * D_MIX ** -0.5),
        'ln1_g': 1.0 + nrm(ks[13], (L, D_MODEL), 0.01),
        'ln1_b': nrm(ks[14], (L, D_MODEL), 0.01),
        'w_router': nrm(ks[15], (L, D_MODEL, E), D_MODEL ** -0.5),
        'b_router': nrm(ks[16], (L, E), 0.01),
        'w_gate': nrm(ks[17], (L, E, D_MODEL, D_FF), D_MODEL ** -0.5),
        'b_gate': nrm(ks[18], (L, E, D_FF), 0.01),
        'w_up': nrm(ks[19], (L, E, D_MODEL, D_FF), D_MODEL ** -0.5),
        'b_up': nrm(ks[20], (L, E, D_FF), 0.01),
        'w_down': nrm(ks[21], (L, E, D_FF, D_MODEL), DEEPNORM_BETA * D_FF ** -0.5),
        'b_down': nrm(ks[22], (L, E, D_MODEL), 0.01),
        'ln2_g': 1.0 + nrm(ks[23], (L, D_MODEL), 0.01),
        'ln2_b': nrm(ks[24], (L, D_MODEL), 0.01),
        'w_ple_gate': nrm(ks[25], (L, D_MODEL, D_MODEL), D_MODEL ** -0.5),
        'w_ple_proj': nrm(ks[26], (L, D_PLE, D_MODEL), DEEPNORM_BETA * D_PLE ** -0.5),
        'ln3_g': 1.0 + nrm(ks[27], (L, D_MODEL), 0.01),
        'ln3_b': nrm(ks[28], (L, D_MODEL), 0.01),
    }


def reference(x_prompt, x_sample, state_conv, state_rec, p_prompt, p_sample,
              ln_in_g, ln_in_b, w_in, conv_w, lb_theta, rms_g, w_out, ln1_g, ln1_b,
              w_router, b_router, w_gate, b_gate, w_up, b_up, w_down, b_down,
              ln2_g, ln2_b, w_ple_gate, w_ple_proj, ln3_g, ln3_b):
    bp, sp, d = x_prompt.shape
    bs, ss, _ = x_sample.shape
    lb_all = jnp.cumsum(jax.nn.softmax(lb_theta.astype(jnp.float32), axis=0), axis=0)
    xp = layer_norm(x_prompt, ln_in_g, ln_in_b)
    xs = layer_norm(x_sample, ln_in_g, ln_in_b)
    conv_p_list, rec_p_list, conv_s_list, rec_s_list = [], [], [], []
    for i in range(DEPTH):
        conv0 = jnp.zeros((bp, CONV_W - 1, D_CONV), x_prompt.dtype)
        rec0 = jnp.zeros((bp, N_HEADS, HEAD_K, HEAD_V), state_rec.dtype)
        mp, cp, rp = token_mixers(xp, conv0, rec0, w_in[i], conv_w[i], lb_all[i], rms_g[i], w_out[i],
                                  min(CHUNK, sp))
        ms, cs, rs = token_mixers(xs, state_conv[i], state_rec[i], w_in[i], conv_w[i], lb_all[i], rms_g[i],
                                  w_out[i], ss)
        conv_p_list.append(cp); rec_p_list.append(rp)
        conv_s_list.append(cs); rec_s_list.append(rs)
        xp = layer_norm(DEEPNORM_ALPHA * xp + mp, ln1_g[i], ln1_b[i])
        xs = layer_norm(DEEPNORM_ALPHA * xs + ms, ln1_g[i], ln1_b[i])
        flat = jnp.concatenate([xp.reshape(bp * sp, d), xs.reshape(bs * ss, d)], axis=0)
        ff = moe(flat, w_router[i], b_router[i], w_gate[i], b_gate[i], w_up[i], b_up[i], w_down[i], b_down[i])
        xp = layer_norm(DEEPNORM_ALPHA * xp + ff[:bp * sp].reshape(bp, sp, d), ln2_g[i], ln2_b[i])
        xs = layer_norm(DEEPNORM_ALPHA * xs + ff[bp * sp:].reshape(bs, ss, d), ln2_g[i], ln2_b[i])
        ep = jax.nn.sigmoid(xp @ w_ple_gate[i]) * (p_prompt[i] @ w_ple_proj[i])
        es = jax.nn.sigmoid(xs @ w_ple_gate[i]) * (p_sample[i] @ w_ple_proj[i])
        xp = layer_norm(DEEPNORM_ALPHA * xp + ep, ln3_g[i], ln3_b[i])
        xs = layer_norm(DEEPNORM_ALPHA * xs + es, ln3_g[i], ln3_b[i])
    conv_prompt = jnp.stack(conv_p_list, axis=0)
    rec_prompt = jnp.stack(rec_p_list, axis=0)
    conv_sample = jnp.stack(conv_s_list, axis=0)
    rec_sample = jnp.stack(rec_s_list, axis=0)
    return (xp, xs, conv_prompt, rec_prompt, conv_sample, rec_sample)
```

```python
import functools

import numpy as np
import jax
import jax.numpy as jnp
from jax import lax
from jax.experimental import pallas as pl
from jax.experimental.pallas import tpu as pltpu

F32 = jnp.float32
BF16 = jnp.bfloat16
HIGHEST = lax.Precision.HIGHEST

CONV_W = 3
N_HEADS = 8
HEAD_K = 128
HEAD_V = 128
TOP_K = 4
SWIGLU_LIMIT = 7.0
SWIGLU_ALPHA = 1.702
LN_EPS = 1e-5
RMS_EPS = 1e-6

LANES = 128
SUBLANES = 8
VMEM_PHYSICAL_BYTES = 64 * 1024 * 1024

CHUNK = 64
SAMPLE_BLOCK = 16
ROW_BLOCK = 128
ITEM_ROWS = 1536
FF_TILE = 512
NEG_BIG = -1e30


def _cparams(sem, vmem_mb):
    return pltpu.CompilerParams(dimension_semantics=sem, vmem_limit_bytes=vmem_mb * 1024 * 1024)


def _layer_norm(x, g, b):
    mu = jnp.mean(x, axis=-1, keepdims=True)
    xc = x - mu
    var = jnp.mean(xc * xc, axis=-1, keepdims=True)
    return xc * lax.rsqrt(var + LN_EPS) * g + b


def _sigmoid(x):
    return 1.0 / (1.0 + jnp.exp(-x))


def _ln_in_kernel(x_ref, g_ref, b_ref, o_ref):
    o_ref[...] = _layer_norm(x_ref[...], g_ref[...], b_ref[...]).astype(BF16)


def _ln_in(x, g, b, tm):
    n, d = x.shape
    return pl.pallas_call(
        _ln_in_kernel,
        grid=(n // tm,),
        in_specs=[pl.BlockSpec((tm, d), lambda i: (i, 0)),
                  pl.BlockSpec((1, d), lambda i: (0, 0)),
                  pl.BlockSpec((1, d), lambda i: (0, 0))],
        out_specs=pl.BlockSpec((tm, d), lambda i: (i, 0)),
        out_shape=jax.ShapeDtypeStruct((n, d), BF16),
        compiler_params=_cparams(("parallel",), 40),
        name="ln_in",
    )(x, g, b)


def _matmul_kernel(x_ref, w_ref, o_ref):
    o_ref[...] = jnp.dot(x_ref[...], w_ref[...], preferred_element_type=F32)


def _in_proj(xn, w, tm, tn):
    n, d = xn.shape
    d_in = w.shape[1]
    return pl.pallas_call(
        _matmul_kernel,
        grid=(d_in // tn, n // tm),
        in_specs=[pl.BlockSpec((tm, d), lambda j, i: (i, 0)),
                  pl.BlockSpec((d, tn), lambda j, i: (0, j))],
        out_specs=pl.BlockSpec((tm, tn), lambda j, i: (i, j)),
        out_shape=jax.ShapeDtypeStruct((n, d_in), F32),
        compiler_params=_cparams(("parallel", "parallel"), 48),
        name="in_proj",
    )(xn, w)


def _forget_gates(fz, lb):
    e = jnp.exp(-jnp.abs(fz))
    r = 1.0 / (1.0 + e)
    er = e * r
    pos = fz >= 0
    sig_p = jnp.where(pos, r, er)
    sig_n = jnp.where(pos, er, r)
    oml = 1.0 - lb
    return lb + oml * sig_p, oml * sig_n


def _chunk_matrices(c):
    t = np.arange(c)[:, None]
    j = np.arange(c)[None, :]
    mats = [(j <= t), (j > t)]
    blk = c
    while blk >= 2:
        half = blk // 2
        mid = (t // blk) * blk + half
        second = (t % blk) >= half
        m_q = (j >= mid) & (j <= t)
        m_k = (j > t) & (j < mid)
        mats.append(np.where(second, m_q, m_k))
        blk = half
    return np.concatenate(mats, axis=0).astype(np.float32)


def _mix_prompt_kernel(proj_ref, convw_ref, lb_ref, rmsg_ref, cmat_ref,
                       mix_ref, convst_ref, recst_ref, s_ref, carry_ref, *, d_conv):
    c = CHUNK
    tb = pl.program_id(1)
    n_tb = pl.num_programs(1)

    @pl.when(tb == 0)
    def _():
        s_ref[...] = jnp.zeros_like(s_ref)
        carry_ref[...] = jnp.zeros_like(carry_ref)

    u = proj_ref[:, 0:d_conv] * proj_ref[:, 2 * d_conv:3 * d_conv]
    row = lax.broadcasted_iota(jnp.int32, u.shape, 0)
    prev1 = carry_ref[SUBLANES - 1:SUBLANES, :]
    prev2 = carry_ref[SUBLANES - 2:SUBLANES - 1, :]
    u1 = jnp.where(row == 0, prev1, pltpu.roll(u, 1, 0))
    u2 = jnp.where(row == 0, prev2, jnp.where(row == 1, prev1, pltpu.roll(u, 2, 0)))
    y = convw_ref[0:1, :] * u2 + convw_ref[1:2, :] * u1 + convw_ref[2:3, :] * u
    mix_ref[:, 0:d_conv] = (proj_ref[:, d_conv:2 * d_conv] * y).astype(BF16)
    carry_ref[...] = u[c - SUBLANES:c, :]
    convst_ref[0] = u[c - SUBLANES:c, :]

    o0 = 3 * d_conv
    d_rec = N_HEADS * HEAD_K
    q = proj_ref[:, o0:o0 + d_rec]
    fz = proj_ref[:, o0 + d_rec:o0 + 2 * d_rec]
    v = proj_ref[:, o0 + 2 * d_rec:o0 + 3 * d_rec]
    g = proj_ref[:, o0 + 3 * d_rec:o0 + 4 * d_rec]
    f, kk = _forget_gates(fz, lb_ref[...])
    logf = jnp.log(f)
    ex = jnp.dot(cmat_ref[...], logf, precision=HIGHEST, preferred_element_type=F32)
    b_cum = ex[0:c]
    d_end = ex[c:2 * c]
    n_lev = cmat_ref.shape[0] // c - 2

    trow = lax.broadcasted_iota(jnp.int32, (c, c), 0)
    tcol = lax.broadcasted_iota(jnp.int32, (c, c), 1)
    prow = lax.broadcasted_iota(jnp.int32, (c, HEAD_K), 0)
    ones_ck = jnp.ones((c, HEAD_V), F32)
    contract0 = (((0,), (0,)), ((), ()))
    contract1 = (((1,), (1,)), ((), ()))

    for h in range(N_HEADS):
        sl = slice(h * HEAD_K, (h + 1) * HEAD_K)
        qh, kh, vh = q[:, sl], kk[:, sl], v[:, sl]
        vb = vh.astype(BF16)
        s_old = s_ref[h]
        o = jnp.dot((qh * jnp.exp(b_cum[:, sl])).astype(BF16), s_old.astype(BF16),
                    preferred_element_type=F32)
        sc = jnp.zeros((c, c), F32)
        for lev in range(n_lev):
            blk = c >> lev
            sh = blk.bit_length() - 1
            dl = jnp.exp(ex[(2 + lev) * c:(3 + lev) * c, sl])
            second = (prow & (blk - 1)) >= (blk // 2)
            qt = jnp.where(second, qh * dl, 0.0).astype(BF16)
            kt = jnp.where(second, 0.0, kh * dl).astype(BF16)
            s_l = lax.dot_general(qt, kt, contract1, preferred_element_type=F32)
            sc = sc + jnp.where((trow >> sh) == (tcol >> sh), s_l, 0.0)
        o = o + jnp.dot(sc.astype(BF16), vb, preferred_element_type=F32)
        o = o + jnp.sum(qh * kh, axis=1, keepdims=True) * vh
        khat = (kh * jnp.exp(d_end[:, sl])).astype(BF16)
        upd = lax.dot_general(khat, vb, contract0, preferred_element_type=F32)
        b_col = lax.dot_general(logf[:, sl], ones_ck, contract0, precision=HIGHEST,
                                preferred_element_type=F32)
        s_ref[h] = jnp.exp(b_col) * s_old + upd
        on = o * lax.rsqrt(jnp.mean(o * o, axis=1, keepdims=True) + RMS_EPS) * rmsg_ref[:, sl]
        gh = g[:, sl]
        mix_ref[:, d_conv + h * HEAD_V:d_conv + (h + 1) * HEAD_V] = (
            on * (gh * _sigmoid(gh))).astype(BF16)

    @pl.when(tb == n_tb - 1)
    def _():
        recst_ref[0] = s_ref[...]


def _mix_prompt(proj, conv_w, lb, rms_g, n_tok, bsz, seq, d_conv):
    d_in = proj.shape[1]
    d_mix = d_conv + N_HEADS * HEAD_V
    n_tb = seq // CHUNK
    cmat = jnp.asarray(_chunk_matrices(CHUNK))
    kern = functools.partial(_mix_prompt_kernel, d_conv=d_conv)
    return pl.pallas_call(
        kern,
        grid=(bsz, n_tb),
        in_specs=[pl.BlockSpec((CHUNK, d_in), lambda b, t: (b * n_tb + t, 0)),
                  pl.BlockSpec((CONV_W, d_conv), lambda b, t: (0, 0)),
                  pl.BlockSpec((1, N_HEADS * HEAD_K), lambda b, t: (0, 0)),
                  pl.BlockSpec((1, N_HEADS * HEAD_V), lambda b, t: (0, 0)),
                  pl.BlockSpec(cmat.shape, lambda b, t: (0, 0))],
        out_specs=[pl.BlockSpec((CHUNK, d_mix), lambda b, t: (b * n_tb + t, 0)),
                   pl.BlockSpec((1, SUBLANES, d_conv), lambda b, t: (b, 0, 0)),
                   pl.BlockSpec((1, N_HEADS, HEAD_K, HEAD_V), lambda b, t: (b, 0, 0, 0))],
        out_shape=[jax.ShapeDtypeStruct((n_tok, d_mix), BF16),
                   jax.ShapeDtypeStruct((bsz, SUBLANES, d_conv), F32),
                   jax.ShapeDtypeStruct((bsz, N_HEADS, HEAD_K, HEAD_V), F32)],
        scratch_shapes=[pltpu.VMEM((N_HEADS, HEAD_K, HEAD_V), F32),
                        pltpu.VMEM((SUBLANES, d_conv), F32)],
        compiler_params=_cparams(("parallel", "arbitrary"), 40),
        name="mix_prompt",
    )(proj, conv_w, lb, rms_g, cmat)


def _mix_sample_kernel(proj_ref, cst_ref, rst_ref, convw_ref, lb_ref, rmsg_ref, sel_ref,
                       mix_ref, cnew_ref, rnew_ref, *, d_conv):
    nb = SAMPLE_BLOCK
    u = proj_ref[:, 0:d_conv] * proj_ref[:, 2 * d_conv:3 * d_conv]
    buf0 = cst_ref[:, 0:d_conv]
    buf1 = cst_ref[:, d_conv:2 * d_conv]
    y = convw_ref[0:1, :] * buf0 + convw_ref[1:2, :] * buf1 + convw_ref[2:3, :] * u
    mix_ref[:, 0:d_conv] = (proj_ref[:, d_conv:2 * d_conv] * y).astype(BF16)
    cnew_ref[:, 0:d_conv] = buf1
    cnew_ref[:, d_conv:2 * d_conv] = u

    o0 = 3 * d_conv
    d_rec = N_HEADS * HEAD_K
    q = proj_ref[:, o0:o0 + d_rec]
    fz = proj_ref[:, o0 + d_rec:o0 + 2 * d_rec]
    v = proj_ref[:, o0 + 2 * d_rec:o0 + 3 * d_rec]
    g = proj_ref[:, o0 + 3 * d_rec:o0 + 4 * d_rec]
    f, kk = _forget_gates(fz, lb_ref[...])
    contract0 = (((0,), (0,)), ((), ()))
    sel = sel_ref[...]
    row = lax.broadcasted_iota(jnp.int32, (nb, HEAD_V), 0)

    def columns(a):
        return lax.dot_general(a, sel, contract0, precision=HIGHEST, preferred_element_type=F32)

    for h in range(N_HEADS):
        sl = slice(h * HEAD_K, (h + 1) * HEAD_K)
        f_c, k_c, q_c = columns(f[:, sl]), columns(kk[:, sl]), columns(q[:, sl])
        o = jnp.zeros((nb, HEAD_V), F32)
        for n in range(nb):
            nl = slice(n * HEAD_V, (n + 1) * HEAD_V)
            s_new = f_c[:, nl] * rst_ref[n, h] + k_c[:, nl] * v[n:n + 1, sl]
            rnew_ref[n, h] = s_new
            o_row = jnp.sum(q_c[:, nl] * s_new, axis=0, keepdims=True)
            o = jnp.where(row == n, o_row, o)
        on = o * lax.rsqrt(jnp.mean(o * o, axis=1, keepdims=True) + RMS_EPS) * rmsg_ref[:, sl]
        gh = g[:, sl]
        mix_ref[:, d_conv + h * HEAD_V:d_conv + (h + 1) * HEAD_V] = (
            on * (gh * _sigmoid(gh))).astype(BF16)


def _mix_sample(proj, conv_state, rec_state, conv_w, lb, rms_g, n_prompt, d_conv):
    n_seq = conv_state.shape[0]
    d_in = proj.shape[1]
    d_mix = d_conv + N_HEADS * HEAD_V
    nb = SAMPLE_BLOCK
    blk0 = n_prompt // nb
    sel = jnp.asarray(np.kron(np.eye(nb, dtype=np.float32), np.ones((1, HEAD_V), np.float32)))
    kern = functools.partial(_mix_sample_kernel, d_conv=d_conv)
    return pl.pallas_call(
        kern,
        grid=(n_seq // nb,),
        in_specs=[pl.BlockSpec((nb, d_in), lambda i: (blk0 + i, 0)),
                  pl.BlockSpec((nb, 2 * d_conv), lambda i: (i, 0)),
                  pl.BlockSpec((nb, N_HEADS, HEAD_K, HEAD_V), lambda i: (i, 0, 0, 0)),
                  pl.BlockSpec((CONV_W, d_conv), lambda i: (0, 0)),
                  pl.BlockSpec((1, N_HEADS * HEAD_K), lambda i: (0, 0)),
                  pl.BlockSpec((1, N_HEADS * HEAD_V), lambda i: (0, 0)),
                  pl.BlockSpec(sel.shape, lambda i: (0, 0))],
        out_specs=[pl.BlockSpec((nb, d_mix), lambda i: (i, 0)),
                   pl.BlockSpec((nb, 2 * d_conv), lambda i: (i, 0)),
                   pl.BlockSpec((nb, N_HEADS, HEAD_K, HEAD_V), lambda i: (i, 0, 0, 0))],
        out_shape=[jax.ShapeDtypeStruct((n_seq, d_mix), BF16),
                   jax.ShapeDtypeStruct((n_seq, 2 * d_conv), F32),
                   jax.ShapeDtypeStruct(rec_state.shape, F32)],
        compiler_params=_cparams(("parallel",), 52),
        name="mix_sample",
    )(proj, conv_state, rec_state, conv_w, lb, rms_g, sel)


def _post_mix_kernel(mixp_ref, mixs_ref, x_ref, wout_ref, ling_ref, linb_ref, l1g_ref, l1b_ref, wr_ref,
                     br_ref, x1_ref, x1p_ref, route_ref, gate_ref, cnt_ref, run_ref, *, alpha, n_pt):
    i = pl.program_id(0)

    @pl.when(i == 0)
    def _():
        run_ref[...] = jnp.zeros_like(run_ref)

    xn = _layer_norm(x_ref[...], ling_ref[...], linb_ref[...])
    mix = jnp.where(i < n_pt, mixp_ref[...], mixs_ref[...])
    h = jnp.dot(mix, wout_ref[...], preferred_element_type=F32)
    x1 = _layer_norm(alpha * xn + h, l1g_ref[...], l1b_ref[...])
    x1_ref[...] = x1
    half = x1.shape[1] // 2
    bits = pltpu.bitcast(x1.astype(BF16).astype(F32), jnp.uint32)
    x1p_ref[...] = (bits[:, half:] & jnp.uint32(0xFFFF0000)) | (bits[:, :half] >> 16)

    tm = x1.shape[0]
    logits = jnp.dot(x1, wr_ref[...], precision=HIGHEST, preferred_element_type=F32) + br_ref[...]
    lane = lax.broadcasted_iota(jnp.int32, (tm, LANES), 1)
    lane_f = lane.astype(F32)
    work = logits
    vals, idxs = [], []
    for _ in range(TOP_K):
        m = jnp.max(work, axis=1, keepdims=True)
        ix = jnp.min(jnp.where(work == m, lane_f, float(LANES)), axis=1, keepdims=True)
        vals.append(m)
        idxs.append(ix)
        work = jnp.where(lane_f == ix, NEG_BIG, work)
    ex = [jnp.exp(vv - vals[0]) for vv in vals]
    den = ex[0] + ex[1] + ex[2] + ex[3]
    onehots = [(lane_f == ix).astype(F32) for ix in idxs]
    oh = onehots[0] + onehots[1] + onehots[2] + onehots[3]
    tr = lax.broadcasted_iota(jnp.int32, (tm, tm), 0)
    tc = lax.broadcasted_iota(jnp.int32, (tm, tm), 1)
    before = jnp.dot((tc < tr).astype(BF16), oh.astype(BF16), preferred_element_type=F32)
    pos = before + run_ref[...]
    route = jnp.zeros((tm, LANES), F32)
    gates = jnp.zeros((tm, LANES), F32)
    for k in range(TOP_K):
        rank = jnp.sum(onehots[k] * pos, axis=1, keepdims=True)
        route = jnp.where(lane == k, idxs[k], route)
        route = jnp.where(lane == TOP_K + k, rank, route)
        gates = jnp.where(lane == k, ex[k] / den, gates)
    route_ref[...] = route.astype(jnp.int32)
    gate_ref[...] = gates
    run_ref[...] = run_ref[...] + jnp.sum(oh, axis=0, keepdims=True)
    cnt_ref[...] = run_ref[...].astype(jnp.int32)


def _post_mix(mix_p, mix_s, x, w_out, lin_g, lin_b, l1_g, l1_b, w_r, b_r, alpha, tm):
    n, d = x.shape
    d_mix = mix_p.shape[1]
    n_pt = mix_p.shape[0] // tm
    assert mix_p.shape[0] % tm == 0 and mix_s.shape[0] % tm == 0
    row = lambda i: (i, 0)
    fixed = lambda i: (0, 0)
    kern = functools.partial(_post_mix_kernel, alpha=alpha, n_pt=n_pt)
    return pl.pallas_call(
        kern,
        grid=(n // tm,),
        in_specs=[pl.BlockSpec((tm, d_mix), lambda i: (jnp.minimum(i, n_pt - 1), 0)),
                  pl.BlockSpec((tm, d_mix), lambda i: (jnp.maximum(i - n_pt, 0), 0)),
                  pl.BlockSpec((tm, d), row),
                  pl.BlockSpec(w_out.shape, fixed),
                  pl.BlockSpec((1, d), fixed), pl.BlockSpec((1, d), fixed),
                  pl.BlockSpec((1, d), fixed), pl.BlockSpec((1, d), fixed),
                  pl.BlockSpec((d, LANES), fixed), pl.BlockSpec((1, LANES), fixed)],
        out_specs=[pl.BlockSpec((tm, d), row),
                   pl.BlockSpec((tm, d // 2), row),
                   pl.BlockSpec((tm, LANES), row),
                   pl.BlockSpec((tm, LANES), row),
                   pl.BlockSpec((1, LANES), fixed)],
        out_shape=[jax.ShapeDtypeStruct((n, d), F32),
                   jax.ShapeDtypeStruct((n, d // 2), jnp.uint32),
                   jax.ShapeDtypeStruct((n, LANES), jnp.int32),
                   jax.ShapeDtypeStruct((n, LANES), F32),
                   jax.ShapeDtypeStruct((1, LANES), jnp.int32)],
        scratch_shapes=[pltpu.VMEM((1, LANES), F32)],
        compiler_params=_cparams(("arbitrary",), 48),
        name="post_mix",
    )(mix_p, mix_s, x, w_out, lin_g, lin_b, l1_g, l1_b, w_r, b_r)


def _dispatch_kernel(dest_ref, padlo_ref, padhi_ref, x_hbm, xs_hbm, zero_ref, sem, *, tm, n_experts):
    i = pl.program_id(0)

    def row_copy(src, dst_row):
        return pltpu.make_async_copy(src, xs_hbm.at[pl.ds(dst_row, 1), :], sem)

    @pl.when(i == 0)
    def _():
        zero_ref[...] = jnp.zeros_like(zero_ref)

        def per_expert(e, carry):
            def start(r, c):
                row_copy(zero_ref.at[pl.ds(0, 1), :], r).start()
                return c

            def wait(r, c):
                row_copy(zero_ref.at[pl.ds(0, 1), :], r).wait()
                return c

            lax.fori_loop(padlo_ref[e], padhi_ref[e], start, 0)
            lax.fori_loop(padlo_ref[e], padhi_ref[e], wait, 0)
            return carry

        lax.fori_loop(0, n_experts, per_expert, 0)

        tail0 = padhi_ref[n_experts - 1]
        n_tail = (xs_hbm.shape[0] - tail0) // ROW_BLOCK

        def tail_copy(c):
            r0 = pl.multiple_of(tail0 + c * ROW_BLOCK, ROW_BLOCK)
            return pltpu.make_async_copy(zero_ref, xs_hbm.at[pl.ds(r0, ROW_BLOCK), :], sem)

        def tail_start(c, carry):
            tail_copy(c).start()
            return carry

        def tail_wait(c, carry):
            tail_copy(c).wait()
            return carry

        lax.fori_loop(0, n_tail, tail_start, 0)
        lax.fori_loop(0, n_tail, tail_wait, 0)

    base = i * tm

    def start(t, c):
        src = x_hbm.at[pl.ds(base + t, 1), :]
        for k in range(TOP_K):
            row_copy(src, dest_ref[t * TOP_K + k]).start()
        return c

    def wait(t, c):
        src = x_hbm.at[pl.ds(base + t, 1), :]
        for k in range(TOP_K):
            row_copy(src, dest_ref[t * TOP_K + k]).wait()
        return c

    lax.fori_loop(0, tm, start, 0)
    lax.fori_loop(0, tm, wait, 0)


def _dispatch(x1p, dest_flat, pad_lo, pad_hi, n_slots, tm):
    n, dh = x1p.shape
    n_experts = pad_lo.shape[0]
    kern = functools.partial(_dispatch_kernel, tm=tm, n_experts=n_experts)
    return pl.pallas_call(
        kern,
        grid=(n // tm,),
        in_specs=[pl.BlockSpec((tm * TOP_K,), lambda i: (i,), memory_space=pltpu.SMEM),
                  pl.BlockSpec(memory_space=pltpu.SMEM),
                  pl.BlockSpec(memory_space=pltpu.SMEM),
                  pl.BlockSpec(memory_space=pl.ANY)],
        out_specs=pl.BlockSpec(memory_space=pl.ANY),
        out_shape=jax.ShapeDtypeStruct((n_slots, dh), jnp.uint32),
        scratch_shapes=[pltpu.VMEM((ROW_BLOCK, dh), jnp.uint32), pltpu.SemaphoreType.DMA(())],
        compiler_params=_cparams(("arbitrary",), 32),
        name="dispatch",
    )(dest_flat, pad_lo, pad_hi, x1p)


def _expert_kernel(ie_ref, is_ref, ir_ref, xs_hbm, wg_ref, wu_ref, wd_ref, bg_ref, bu_ref, bd_ref,
                   ys_hbm, xbuf, ybuf, wgu_bf, wd_bf, sem_in, sem_out):
    del ie_ref
    i = pl.program_id(0)
    j = pl.program_id(1)
    n_j = pl.num_programs(1)
    rows = ir_ref[i]
    start = is_ref[i]
    n_chunk = rows // ROW_BLOCK
    tf = wg_ref.shape[1]

    def in_copy(c):
        r0 = pl.multiple_of(c * ROW_BLOCK, ROW_BLOCK)
        g0 = pl.multiple_of(start + c * ROW_BLOCK, ROW_BLOCK)
        return pltpu.make_async_copy(xs_hbm.at[pl.ds(g0, ROW_BLOCK), :],
                                     xbuf.at[pl.ds(r0, ROW_BLOCK), :], sem_in)

    def out_copy(c):
        r0 = pl.multiple_of(c * ROW_BLOCK, ROW_BLOCK)
        g0 = pl.multiple_of(start + c * ROW_BLOCK, ROW_BLOCK)
        return pltpu.make_async_copy(ybuf.at[pl.ds(r0, ROW_BLOCK), :],
                                     ys_hbm.at[pl.ds(g0, ROW_BLOCK), :], sem_out)

    def for_chunks(fn):
        def body(c, carry):
            fn(c)
            return carry
        lax.fori_loop(0, n_chunk, body, 0)

    @pl.when(rows > 0)
    def _():
        @pl.when(j == 0)
        def _():
            for_chunks(lambda c: in_copy(c).start())

        wgu_bf[:, 0:tf] = wg_ref[...].astype(BF16)
        wgu_bf[:, tf:2 * tf] = wu_ref[...].astype(BF16)
        wd_bf[...] = wd_ref[...].astype(BF16)

        @pl.when(j == 0)
        def _():
            for_chunks(lambda c: in_copy(c).wait())

        def chunk(c, first, last):
            r0 = pl.multiple_of(c * ROW_BLOCK, ROW_BLOCK)
            xu = xbuf[pl.ds(r0, ROW_BLOCK), :]
            lo = pltpu.bitcast(xu << 16, F32).astype(BF16)
            hi = pltpu.bitcast(xu & jnp.uint32(0xFFFF0000), F32).astype(BF16)
            x = jnp.concatenate([lo, hi], axis=1)
            gu = jnp.dot(x, wgu_bf[...], preferred_element_type=F32)
            gg = jnp.minimum(gu[:, 0:tf] + bg_ref[...], SWIGLU_LIMIT)
            uu = jnp.clip(gu[:, tf:2 * tf] + bu_ref[...], -SWIGLU_LIMIT, SWIGLU_LIMIT)
            hid = gg * _sigmoid(SWIGLU_ALPHA * gg) * (uu + 1.0)
            y = jnp.dot(hid.astype(BF16), wd_bf[...], preferred_element_type=F32)
            if not first:
                y = y + ybuf[pl.ds(r0, ROW_BLOCK), :]
            if last:
                y = y + bd_ref[...]
            ybuf[pl.ds(r0, ROW_BLOCK), :] = y
            if last:
                out_copy(c).start()

        @pl.when(j == 0)
        def _():
            for_chunks(lambda c: chunk(c, True, False))

        @pl.when(jnp.logical_and(j > 0, j < n_j - 1))
        def _():
            for_chunks(lambda c: chunk(c, False, False))

        @pl.when(j == n_j - 1)
        def _():
            for_chunks(lambda c: chunk(c, False, True))
            for_chunks(lambda c: out_copy(c).wait())

    @pl.when(jnp.logical_and(i == pl.num_programs(0) - 1, j == n_j - 1))
    def _():
        tail0 = is_ref[pl.num_programs(0)]
        n_tail = (ys_hbm.shape[0] - tail0) // ROW_BLOCK
        ybuf[0:ROW_BLOCK, :] = jnp.zeros((ROW_BLOCK, ybuf.shape[1]), F32)

        def tail_copy(c):
            g0 = pl.multiple_of(tail0 + c * ROW_BLOCK, ROW_BLOCK)
            return pltpu.make_async_copy(ybuf.at[pl.ds(0, ROW_BLOCK), :],
                                         ys_hbm.at[pl.ds(g0, ROW_BLOCK), :], sem_out)

        def tail_start(c, carry):
            tail_copy(c).start()
            return carry

        def tail_wait(c, carry):
            tail_copy(c).wait()
            return carry

        lax.fori_loop(0, n_tail, tail_start, 0)
        lax.fori_loop(0, n_tail, tail_wait, 0)


def _experts(xs, item_e, item_start, item_rows, w_gate, w_up, w_down, b_gate, b_up, b_down):
    n_slots, dh = xs.shape
    n_exp, d, d_ff = w_gate.shape
    n_items = item_e.shape[0]
    n_j = d_ff // FF_TILE
    assert n_j >= 2 and d == 2 * dh

    def jj(i, j, ir):
        return jnp.where(ir[i] > 0, j, n_j - 1)

    grid_spec = pltpu.PrefetchScalarGridSpec(
        num_scalar_prefetch=3,
        grid=(n_items, n_j),
        in_specs=[pl.BlockSpec(memory_space=pl.ANY),
                  pl.BlockSpec((None, d, FF_TILE), lambda i, j, ie, is_, ir: (ie[i], 0, jj(i, j, ir))),
                  pl.BlockSpec((None, d, FF_TILE), lambda i, j, ie, is_, ir: (ie[i], 0, jj(i, j, ir))),
                  pl.BlockSpec((None, FF_TILE, d), lambda i, j, ie, is_, ir: (ie[i], jj(i, j, ir), 0)),
                  pl.BlockSpec((None, 1, FF_TILE), lambda i, j, ie, is_, ir: (ie[i], 0, jj(i, j, ir))),
                  pl.BlockSpec((None, 1, FF_TILE), lambda i, j, ie, is_, ir: (ie[i], 0, jj(i, j, ir))),
                  pl.BlockSpec((None, 1, d), lambda i, j, ie, is_, ir: (ie[i], 0, 0))],
        out_specs=pl.BlockSpec(memory_space=pl.ANY),
        scratch_shapes=[pltpu.VMEM((ITEM_ROWS, dh), jnp.uint32),
                        pltpu.VMEM((ITEM_ROWS, d), F32),
                        pltpu.VMEM((d, 2 * FF_TILE), BF16),
                        pltpu.VMEM((FF_TILE, d), BF16),
                        pltpu.SemaphoreType.DMA(()),
                        pltpu.SemaphoreType.DMA(())],
    )
    return pl.pallas_call(
        _expert_kernel,
        grid_spec=grid_spec,
        out_shape=jax.ShapeDtypeStruct((n_slots, d), F32),
        compiler_params=_cparams(("arbitrary", "arbitrary"), 58),
        name="experts",
    )(item_e, item_start, item_rows, xs, w_gate, w_up, w_down,
      b_gate.reshape(n_exp, 1, d_ff), b_up.reshape(n_exp, 1, d_ff), b_down.reshape(n_exp, 1, d))


def _combine_kernel(dcur_ref, dnext_ref, ys_hbm, x1_ref, gate_ref, p_ref, wpg_ref, wpp_ref,
                    l2g_ref, l2b_ref, l3g_ref, l3b_ref, o_ref, gbuf, sems, *, alpha, tm):
    i = pl.program_id(0)
    n_i = pl.num_programs(0)
    slot = lax.rem(i, 2)

    def row_copy(dref, t, k, s):
        return pltpu.make_async_copy(ys_hbm.at[pl.ds(dref[t * TOP_K + k], 1), :],
                                     gbuf.at[s, k, pl.ds(t, 1), :], sems.at[s])

    def gather(dref, s, wait):
        def body(t, c):
            for k in range(TOP_K):
                cp = row_copy(dref, t, k, s)
                if wait:
                    cp.wait()
                else:
                    cp.start()
            return c
        lax.fori_loop(0, tm, body, 0)

    @pl.when(i == 0)
    def _():
        gather(dcur_ref, 0, False)

    @pl.when(i + 1 < n_i)
    def _():
        gather(dnext_ref, 1 - slot, False)

    gather(dcur_ref, slot, True)

    x1 = x1_ref[...]
    lane = lax.broadcasted_iota(jnp.int32, gate_ref.shape, 1)
    gates = gate_ref[...]
    ff = jnp.zeros_like(x1)
    for k in range(TOP_K):
        gk = jnp.sum(jnp.where(lane == k, gates, 0.0), axis=1, keepdims=True)
        ff = ff + gk * gbuf[slot, k]
    x2 = _layer_norm(alpha * x1 + ff, l2g_ref[...], l2b_ref[...])
    eg = _sigmoid(jnp.dot(x2.astype(BF16), wpg_ref[...], preferred_element_type=F32))
    ep = jnp.dot(p_ref[...].astype(BF16), wpp_ref[...], preferred_element_type=F32)
    o_ref[...] = _layer_norm(alpha * x2 + eg * ep, l3g_ref[...], l3b_ref[...])


def _combine(ys, dest_flat, x1, gates, p, w_pg, w_pp, l2_g, l2_b, l3_g, l3_b, alpha, tm):
    n, d = x1.shape
    n_i = n // tm
    row = lambda i: (i, 0)
    fixed = lambda i: (0, 0)
    kern = functools.partial(_combine_kernel, alpha=alpha, tm=tm)
    return pl.pallas_call(
        kern,
        grid=(n_i,),
        in_specs=[pl.BlockSpec((tm * TOP_K,), lambda i: (i,), memory_space=pltpu.SMEM),
                  pl.BlockSpec((tm * TOP_K,), lambda i: (jnp.minimum(i + 1, n_i - 1),),
                               memory_space=pltpu.SMEM),
                  pl.BlockSpec(memory_space=pl.ANY),
                  pl.BlockSpec((tm, d), row),
                  pl.BlockSpec((tm, LANES), row),
                  pl.BlockSpec((tm, p.shape[1]), row),
                  pl.BlockSpec(w_pg.shape, fixed),
                  pl.BlockSpec(w_pp.shape, fixed),
                  pl.BlockSpec((1, d), fixed), pl.BlockSpec((1, d), fixed),
                  pl.BlockSpec((1, d), fixed), pl.BlockSpec((1, d), fixed)],
        out_specs=pl.BlockSpec((tm, d), row),
        out_shape=jax.ShapeDtypeStruct((n, d), F32),
        scratch_shapes=[pltpu.VMEM((2, TOP_K, tm, d), F32), pltpu.SemaphoreType.DMA((2,))],
        compiler_params=_cparams(("arbitrary",), 48),
        name="combine",
    )(dest_flat, dest_flat, ys, x1, gates, p, w_pg, w_pp, l2_g, l2_b, l3_g, l3_b)


def _routing_tables(expert_idx, rank, counts, n_items):
    n_exp = counts.shape[0]
    padded = (counts + ROW_BLOCK - 1) // ROW_BLOCK * ROW_BLOCK
    pad_end = jnp.cumsum(padded)
    pad_start = pad_end - padded
    dest = (pad_start[expert_idx] + rank).reshape(-1).astype(jnp.int32)
    items_per = (padded + ITEM_ROWS - 1) // ITEM_ROWS
    item_end = jnp.cumsum(items_per)
    ids = jnp.arange(n_items, dtype=jnp.int32)
    total = item_end[-1]
    last_valid = jnp.maximum(total - 1, 0)
    eff = jnp.minimum(ids, last_valid)
    e_of = jnp.minimum(jnp.searchsorted(item_end, eff, side='right'), n_exp - 1).astype(jnp.int32)
    sub = eff - (item_end[e_of] - items_per[e_of])
    start = pad_start[e_of] + sub * ITEM_ROWS
    rows = jnp.clip(padded[e_of] - sub * ITEM_ROWS, 0, ITEM_ROWS)
    rows = jnp.where(ids < total, rows, 0)
    start = jnp.concatenate([start, pad_end[-1:]])
    return (dest, (pad_start + counts).astype(jnp.int32), pad_end.astype(jnp.int32),
            e_of, start.astype(jnp.int32), rows.astype(jnp.int32))


def kernel(x_prompt, x_sample, state_conv, state_rec, p_prompt, p_sample, ln_in_g, ln_in_b, w_in, conv_w,
           lb_theta, rms_g, w_out, ln1_g, ln1_b, w_router, b_router, w_gate, b_gate, w_up, b_up, w_down,
           b_down, ln2_g, ln2_b, w_ple_gate, w_ple_proj, ln3_g, ln3_b):
    bp, sp, d = x_prompt.shape
    bs, ss, _ = x_sample.shape
    depth = w_in.shape[0]
    assert depth == 1 and ss == 1 and sp % CHUNK == 0
    d_conv = state_conv.shape[-1]
    n_exp = w_router.shape[-1]
    n_p = bp * sp
    n = n_p + bs
    alpha = (2 * depth) ** 0.25
    row2 = lambda a: a.reshape(1, -1)

    lb = jnp.cumsum(jax.nn.softmax(lb_theta.astype(F32), axis=0), axis=0)[0]
    x_all = jnp.concatenate([x_prompt.reshape(n_p, d), x_sample.reshape(bs, d)], axis=0)
    p_all = jnp.concatenate([p_prompt[0].reshape(n_p, -1), p_sample[0].reshape(bs, -1)], axis=0)

    xn = _ln_in(x_all, row2(ln_in_g), row2(ln_in_b), tm=640)
    proj = _in_proj(xn, w_in[0].astype(BF16), tm=1040, tn=1024)

    mix_p, conv_tail, rec_p = _mix_prompt(proj, conv_w[0], row2(lb), row2(rms_g[0]), n_p, bp, sp, d_conv)
    mix_s, conv_s, rec_s = _mix_sample(proj, state_conv[0].reshape(bs, -1), state_rec[0], conv_w[0],
                                       row2(lb), row2(rms_g[0]), n_p, d_conv)

    w_r = jnp.zeros((d, LANES), F32).at[:, :n_exp].set(w_router[0])
    b_r = jnp.full((1, LANES), NEG_BIG, F32).at[0, :n_exp].set(b_router[0])
    x1, x1p, route, gates, counts = _post_mix(
        mix_p, mix_s, x_all, w_out[0].astype(BF16), row2(ln_in_g), row2(ln_in_b), row2(ln1_g[0]),
        row2(ln1_b[0]), w_r, b_r, alpha, tm=128)

    n_slots = n * TOP_K + n_exp * ROW_BLOCK
    n_items = n_exp + n_slots // ITEM_ROWS
    dest, pad_lo, pad_hi, item_e, item_start, item_rows = _routing_tables(
        route[:, 0:TOP_K], route[:, TOP_K:2 * TOP_K], counts[0, :n_exp], n_items)

    xs = _dispatch(x1p, dest, pad_lo, pad_hi, n_slots, tm=128)
    ys = _experts(xs, item_e, item_start, item_rows, w_gate[0], w_up[0], w_down[0],
                  b_gate[0], b_up[0], b_down[0])
    y_all = _combine(ys, dest, x1, gates, p_all, w_ple_gate[0].astype(BF16), w_ple_proj[0].astype(BF16),
                     row2(ln2_g[0]), row2(ln2_b[0]), row2(ln3_g[0]), row2(ln3_b[0]), alpha, tm=128)

    return (y_all[:n_p].reshape(bp, sp, d),
            y_all[n_p:].reshape(bs, ss, d),
            conv_tail[:, SUBLANES - (CONV_W - 1):, :][None],
            rec_p[None],
            conv_s.reshape(bs, CONV_W - 1, d_conv)[None],
            rec_s[None])
```

```python
import functools

import numpy as np
import jax
import jax.numpy as jnp
from jax import lax
from jax.experimental import pallas as pl
from jax.experimental.pallas import tpu as pltpu

F32 = jnp.float32
BF16 = jnp.bfloat16
HIGHEST = lax.Precision.HIGHEST

CONV_W = 3
N_HEADS = 8
HEAD_K = 128
HEAD_V = 128
TOP_K = 4
SWIGLU_LIMIT = 7.0
SWIGLU_ALPHA = 1.702
LN_EPS = 1e-5
RMS_EPS = 1e-6

LANES = 128
SUBLANES = 8
VMEM_PHYSICAL_BYTES = 64 * 1024 * 1024

CHUNK = 64
SAMPLE_BLOCK = 16
ROW_BLOCK = 128
ITEM_ROWS = 1536
FF_TILE = 512
NEG_BIG = -1e30


def _cparams(sem, vmem_mb):
    return pltpu.CompilerParams(dimension_semantics=sem, vmem_limit_bytes=vmem_mb * 1024 * 1024)


def _layer_norm(x, g, b):
    mu = jnp.mean(x, axis=-1, keepdims=True)
    xc = x - mu
    var = jnp.mean(xc * xc, axis=-1, keepdims=True)
    return xc * lax.rsqrt(var + LN_EPS) * g + b


def _sigmoid(x):
    return 1.0 / (1.0 + jnp.exp(-x))


def _ln_in_kernel(x_ref, g_ref, b_ref, o_ref):
    o_ref[...] = _layer_norm(x_ref[...], g_ref[...], b_ref[...]).astype(BF16)


def _ln_in(x, g, b, tm):
    n, d = x.shape
    return pl.pallas_call(
        _ln_in_kernel,
        grid=(n // tm,),
        in_specs=[pl.BlockSpec((tm, d), lambda i: (i, 0)),
                  pl.BlockSpec((1, d), lambda i: (0, 0)),
                  pl.BlockSpec((1, d), lambda i: (0, 0))],
        out_specs=pl.BlockSpec((tm, d), lambda i: (i, 0)),
        out_shape=jax.ShapeDtypeStruct((n, d), BF16),
        compiler_params=_cparams(("parallel",), 40),
        name="ln_in",
    )(x, g, b)


def _matmul_kernel(x_ref, w_ref, o_ref):
    o_ref[...] = jnp.dot(x_ref[...], w_ref[...], preferred_element_type=F32)


def _in_proj(xn, w, tm, tn):
    n, d = xn.shape
    d_in = w.shape[1]
    return pl.pallas_call(
        _matmul_kernel,
        grid=(d_in // tn, n // tm),
        in_specs=[pl.BlockSpec((tm, d), lambda j, i: (i, 0)),
                  pl.BlockSpec((d, tn), lambda j, i: (0, j))],
        out_specs=pl.BlockSpec((tm, tn), lambda j, i: (i, j)),
        out_shape=jax.ShapeDtypeStruct((n, d_in), F32),
        compiler_params=_cparams(("parallel", "parallel"), 48),
        name="in_proj",
    )(xn, w)


def _in_proj_sample_kernel(x_ref, g_ref, b_ref, w_ref, o_ref):
    xn = _layer_norm(x_ref[...], g_ref[...], b_ref[...])
    o_ref[...] = jnp.dot(xn, w_ref[...], precision=HIGHEST, preferred_element_type=F32)


def _in_proj_sample(x, g, b, w, tn):
    n, d = x.shape
    d_in = w.shape[1]
    return pl.pallas_call(
        _in_proj_sample_kernel,
        grid=(d_in // tn,),
        in_specs=[pl.BlockSpec((n, d), lambda j: (0, 0)),
                  pl.BlockSpec((1, d), lambda j: (0, 0)),
                  pl.BlockSpec((1, d), lambda j: (0, 0)),
                  pl.BlockSpec((d, tn), lambda j: (0, j))],
        out_specs=pl.BlockSpec((n, tn), lambda j: (0, j)),
        out_shape=jax.ShapeDtypeStruct((n, d_in), F32),
        compiler_params=_cparams(("parallel",), 40),
        name="in_proj_sample",
    )(x, g, b, w)


def _forget_gates(fz, lb):
    e = jnp.exp(-jnp.abs(fz))
    r = 1.0 / (1.0 + e)
    er = e * r
    pos = fz >= 0
    sig_p = jnp.where(pos, r, er)
    sig_n = jnp.where(pos, er, r)
    oml = 1.0 - lb
    return lb + oml * sig_p, oml * sig_n


def _chunk_matrices(c):
    t = np.arange(c)[:, None]
    j = np.arange(c)[None, :]
    mats = [(j <= t), (j > t)]
    blk = c
    while blk >= 2:
        half = blk // 2
        mid = (t // blk) * blk + half
        second = (t % blk) >= half
        m_q = (j >= mid) & (j <= t)
        m_k = (j > t) & (j < mid)
        mats.append(np.where(second, m_q, m_k))
        blk = half
    return np.concatenate(mats, axis=0).astype(np.float32)


def _mix_prompt_kernel(proj_ref, convw_ref, lb_ref, rmsg_ref, cmat_ref,
                       mix_ref, convst_ref, recst_ref, s_ref, carry_ref, *, d_conv):
    c = CHUNK
    tb = pl.program_id(1)
    n_tb = pl.num_programs(1)

    @pl.when(tb == 0)
    def _():
        s_ref[...] = jnp.zeros_like(s_ref)
        carry_ref[...] = jnp.zeros_like(carry_ref)

    u = proj_ref[:, 0:d_conv] * proj_ref[:, 2 * d_conv:3 * d_conv]
    row = lax.broadcasted_iota(jnp.int32, u.shape, 0)
    prev1 = carry_ref[SUBLANES - 1:SUBLANES, :]
    prev2 = carry_ref[SUBLANES - 2:SUBLANES - 1, :]
    u1 = jnp.where(row == 0, prev1, pltpu.roll(u, 1, 0))
    u2 = jnp.where(row == 0, prev2, jnp.where(row == 1, prev1, pltpu.roll(u, 2, 0)))
    y = convw_ref[0:1, :] * u2 + convw_ref[1:2, :] * u1 + convw_ref[2:3, :] * u
    mix_ref[:, 0:d_conv] = (proj_ref[:, d_conv:2 * d_conv] * y).astype(BF16)
    carry_ref[...] = u[c - SUBLANES:c, :]
    convst_ref[0] = u[c - SUBLANES:c, :]

    o0 = 3 * d_conv
    d_rec = N_HEADS * HEAD_K
    q = proj_ref[:, o0:o0 + d_rec]
    fz = proj_ref[:, o0 + d_rec:o0 + 2 * d_rec]
    v = proj_ref[:, o0 + 2 * d_rec:o0 + 3 * d_rec]
    g = proj_ref[:, o0 + 3 * d_rec:o0 + 4 * d_rec]
    f, kk = _forget_gates(fz, lb_ref[...])
    logf = jnp.log(f)
    ex = jnp.dot(cmat_ref[...], logf, precision=HIGHEST, preferred_element_type=F32)
    b_cum = ex[0:c]
    d_end = ex[c:2 * c]
    n_lev = cmat_ref.shape[0] // c - 2

    trow = lax.broadcasted_iota(jnp.int32, (c, c), 0)
    tcol = lax.broadcasted_iota(jnp.int32, (c, c), 1)
    prow = lax.broadcasted_iota(jnp.int32, (c, HEAD_K), 0)
    ones_ck = jnp.ones((c, HEAD_V), F32)
    contract0 = (((0,), (0,)), ((), ()))
    contract1 = (((1,), (1,)), ((), ()))

    for h in range(N_HEADS):
        sl = slice(h * HEAD_K, (h + 1) * HEAD_K)
        qh, kh, vh = q[:, sl], kk[:, sl], v[:, sl]
        vb = vh.astype(BF16)
        s_old = s_ref[h]
        o = jnp.dot((qh * jnp.exp(b_cum[:, sl])).astype(BF16), s_old.astype(BF16),
                    preferred_element_type=F32)
        sc = jnp.zeros((c, c), F32)
        for lev in range(n_lev):
            blk = c >> lev
            sh = blk.bit_length() - 1
            dl = jnp.exp(ex[(2 + lev) * c:(3 + lev) * c, sl])
            second = (prow & (blk - 1)) >= (blk // 2)
            qt = jnp.where(second, qh * dl, 0.0).astype(BF16)
            kt = jnp.where(second, 0.0, kh * dl).astype(BF16)
            s_l = lax.dot_general(qt, kt, contract1, preferred_element_type=F32)
            sc = sc + jnp.where((trow >> sh) == (tcol >> sh), s_l, 0.0)
        o = o + jnp.dot(sc.astype(BF16), vb, preferred_element_type=F32)
        o = o + jnp.sum(qh * kh, axis=1, keepdims=True) * vh
        khat = (kh * jnp.exp(d_end[:, sl])).astype(BF16)
        upd = lax.dot_general(khat, vb, contract0, preferred_element_type=F32)
        b_col = lax.dot_general(logf[:, sl], ones_ck, contract0, precision=HIGHEST,
                                preferred_element_type=F32)
        s_ref[h] = jnp.exp(b_col) * s_old + upd
        on = o * lax.rsqrt(jnp.mean(o * o, axis=1, keepdims=True) + RMS_EPS) * rmsg_ref[:, sl]
        gh = g[:, sl]
        mix_ref[:, d_conv + h * HEAD_V:d_conv + (h + 1) * HEAD_V] = (
            on * (gh * _sigmoid(gh))).astype(BF16)

    @pl.when(tb == n_tb - 1)
    def _():
        recst_ref[0] = s_ref[...]


def _mix_prompt(proj, conv_w, lb, rms_g, n_tok, bsz, seq, d_conv):
    d_in = proj.shape[1]
    d_mix = d_conv + N_HEADS * HEAD_V
    n_tb = seq // CHUNK
    cmat = jnp.asarray(_chunk_matrices(CHUNK))
    kern = functools.partial(_mix_prompt_kernel, d_conv=d_conv)
    return pl.pallas_call(
        kern,
        grid=(bsz, n_tb),
        in_specs=[pl.BlockSpec((CHUNK, d_in), lambda b, t: (b * n_tb + t, 0)),
                  pl.BlockSpec((CONV_W, d_conv), lambda b, t: (0, 0)),
                  pl.BlockSpec((1, N_HEADS * HEAD_K), lambda b, t: (0, 0)),
                  pl.BlockSpec((1, N_HEADS * HEAD_V), lambda b, t: (0, 0)),
                  pl.BlockSpec(cmat.shape, lambda b, t: (0, 0))],
        out_specs=[pl.BlockSpec((CHUNK, d_mix), lambda b, t: (b * n_tb + t, 0)),
                   pl.BlockSpec((1, SUBLANES, d_conv), lambda b, t: (b, 0, 0)),
                   pl.BlockSpec((1, N_HEADS, HEAD_K, HEAD_V), lambda b, t: (b, 0, 0, 0))],
        out_shape=[jax.ShapeDtypeStruct((n_tok, d_mix), BF16),
                   jax.ShapeDtypeStruct((bsz, SUBLANES, d_conv), F32),
                   jax.ShapeDtypeStruct((bsz, N_HEADS, HEAD_K, HEAD_V), F32)],
        scratch_shapes=[pltpu.VMEM((N_HEADS, HEAD_K, HEAD_V), F32),
                        pltpu.VMEM((SUBLANES, d_conv), F32)],
        compiler_params=_cparams(("parallel", "arbitrary"), 40),
        name="mix_prompt",
    )(proj, conv_w, lb, rms_g, cmat)


def _mix_sample_kernel(proj_ref, cst_ref, rst_ref, convw_ref, lb_ref, rmsg_ref, sel_ref,
                       mix_ref, cnew_ref, rnew_ref, *, d_conv):
    nb = SAMPLE_BLOCK
    u = proj_ref[:, 0:d_conv] * proj_ref[:, 2 * d_conv:3 * d_conv]
    buf0 = cst_ref[:, 0:d_conv]
    buf1 = cst_ref[:, d_conv:2 * d_conv]
    y = convw_ref[0:1, :] * buf0 + convw_ref[1:2, :] * buf1 + convw_ref[2:3, :] * u
    mix_ref[:, 0:d_conv] = proj_ref[:, d_conv:2 * d_conv] * y
    cnew_ref[:, 0:d_conv] = buf1
    cnew_ref[:, d_conv:2 * d_conv] = u

    o0 = 3 * d_conv
    d_rec = N_HEADS * HEAD_K
    q = proj_ref[:, o0:o0 + d_rec]
    fz = proj_ref[:, o0 + d_rec:o0 + 2 * d_rec]
    v = proj_ref[:, o0 + 2 * d_rec:o0 + 3 * d_rec]
    g = proj_ref[:, o0 + 3 * d_rec:o0 + 4 * d_rec]
    f, kk = _forget_gates(fz, lb_ref[...])
    contract0 = (((0,), (0,)), ((), ()))
    sel = sel_ref[...]
    row = lax.broadcasted_iota(jnp.int32, (nb, HEAD_V), 0)

    def columns(a):
        return lax.dot_general(a, sel, contract0, precision=HIGHEST, preferred_element_type=F32)

    for h in range(N_HEADS):
        sl = slice(h * HEAD_K, (h + 1) * HEAD_K)
        f_c, k_c, q_c = columns(f[:, sl]), columns(kk[:, sl]), columns(q[:, sl])
        o = jnp.zeros((nb, HEAD_V), F32)
        for n in range(nb):
            nl = slice(n * HEAD_V, (n + 1) * HEAD_V)
            s_new = f_c[:, nl] * rst_ref[n, h] + k_c[:, nl] * v[n:n + 1, sl]
            rnew_ref[n, h] = s_new
            o_row = jnp.sum(q_c[:, nl] * s_new, axis=0, keepdims=True)
            o = jnp.where(row == n, o_row, o)
        on = o * lax.rsqrt(jnp.mean(o * o, axis=1, keepdims=True) + RMS_EPS) * rmsg_ref[:, sl]
        gh = g[:, sl]
        mix_ref[:, d_conv + h * HEAD_V:d_conv + (h + 1) * HEAD_V] = on * (gh * _sigmoid(gh))


def _mix_sample(proj, conv_state, rec_state, conv_w, lb, rms_g, d_conv):
    n_seq = conv_state.shape[0]
    d_in = proj.shape[1]
    d_mix = d_conv + N_HEADS * HEAD_V
    nb = SAMPLE_BLOCK
    sel = jnp.asarray(np.kron(np.eye(nb, dtype=np.float32), np.ones((1, HEAD_V), np.float32)))
    kern = functools.partial(_mix_sample_kernel, d_conv=d_conv)
    return pl.pallas_call(
        kern,
        grid=(n_seq // nb,),
        in_specs=[pl.BlockSpec((nb, d_in), lambda i: (i, 0)),
                  pl.BlockSpec((nb, 2 * d_conv), lambda i: (i, 0)),
                  pl.BlockSpec((nb, N_HEADS, HEAD_K, HEAD_V), lambda i: (i, 0, 0, 0)),
                  pl.BlockSpec((CONV_W, d_conv), lambda i: (0, 0)),
                  pl.BlockSpec((1, N_HEADS * HEAD_K), lambda i: (0, 0)),
                  pl.BlockSpec((1, N_HEADS * HEAD_V), lambda i: (0, 0)),
                  pl.BlockSpec(sel.shape, lambda i: (0, 0))],
        out_specs=[pl.BlockSpec((nb, d_mix), lambda i: (i, 0)),
                   pl.BlockSpec((nb, 2 * d_conv), lambda i: (i, 0)),
                   pl.BlockSpec((nb, N_HEADS, HEAD_K, HEAD_V), lambda i: (i, 0, 0, 0))],
        out_shape=[jax.ShapeDtypeStruct((n_seq, d_mix), F32),
                   jax.ShapeDtypeStruct((n_seq, 2 * d_conv), F32),
                   jax.ShapeDtypeStruct(rec_state.shape, F32)],
        compiler_params=_cparams(("parallel",), 52),
        name="mix_sample",
    )(proj, conv_state, rec_state, conv_w, lb, rms_g, sel)


def _post_mix_kernel(mixp_ref, mixs_ref, xp_ref, xs_ref, woutb_ref, woutf_ref, ling_ref, linb_ref,
                     l1g_ref, l1b_ref, wr_ref, br_ref,
                     x1_ref, x1p_ref, route_ref, gate_ref, cnt_ref, run_ref, h_ref, *, alpha, n_pt):
    i = pl.program_id(0)

    @pl.when(i == 0)
    def _():
        run_ref[...] = jnp.zeros_like(run_ref)

    @pl.when(i < n_pt)
    def _():
        h_ref[...] = jnp.dot(mixp_ref[...], woutb_ref[...], preferred_element_type=F32)

    @pl.when(i >= n_pt)
    def _():
        h_ref[...] = jnp.dot(mixs_ref[...], woutf_ref[...], precision=HIGHEST,
                             preferred_element_type=F32)

    x = jnp.where(i < n_pt, xp_ref[...], xs_ref[...])
    xn = _layer_norm(x, ling_ref[...], linb_ref[...])
    x1 = _layer_norm(alpha * xn + h_ref[...], l1g_ref[...], l1b_ref[...])
    x1_ref[...] = x1
    half = x1.shape[1] // 2
    bits = pltpu.bitcast(x1.astype(BF16).astype(F32), jnp.uint32)
    x1p_ref[...] = (bits[:, half:] & jnp.uint32(0xFFFF0000)) | (bits[:, :half] >> 16)

    tm = x1.shape[0]
    logits = jnp.dot(x1, wr_ref[...], precision=HIGHEST, preferred_element_type=F32) + br_ref[...]
    lane = lax.broadcasted_iota(jnp.int32, (tm, LANES), 1)
    lane_f = lane.astype(F32)
    work = logits
    vals, idxs = [], []
    for _ in range(TOP_K):
        m = jnp.max(work, axis=1, keepdims=True)
        ix = jnp.min(jnp.where(work == m, lane_f, float(LANES)), axis=1, keepdims=True)
        vals.append(m)
        idxs.append(ix)
        work = jnp.where(lane_f == ix, NEG_BIG, work)
    ex = [jnp.exp(vv - vals[0]) for vv in vals]
    den = ex[0] + ex[1] + ex[2] + ex[3]
    onehots = [(lane_f == ix).astype(F32) for ix in idxs]
    oh = onehots[0] + onehots[1] + onehots[2] + onehots[3]
    tr = lax.broadcasted_iota(jnp.int32, (tm, tm), 0)
    tc = lax.broadcasted_iota(jnp.int32, (tm, tm), 1)
    before = jnp.dot((tc < tr).astype(BF16), oh.astype(BF16), preferred_element_type=F32)
    pos = before + run_ref[...]
    route = jnp.zeros((tm, LANES), F32)
    gates = jnp.zeros((tm, LANES), F32)
    for k in range(TOP_K):
        rank = jnp.sum(onehots[k] * pos, axis=1, keepdims=True)
        route = jnp.where(lane == k, idxs[k], route)
        route = jnp.where(lane == TOP_K + k, rank, route)
        gates = jnp.where(lane == k, ex[k] / den, gates)
    route_ref[...] = route.astype(jnp.int32)
    gate_ref[...] = gates
    run_ref[...] = run_ref[...] + jnp.sum(oh, axis=0, keepdims=True)
    cnt_ref[...] = run_ref[...].astype(jnp.int32)


def _post_mix(mix_p, mix_s, x_p, x_s, w_out_b, w_out_f, lin_g, lin_b, l1_g, l1_b, w_r, b_r, alpha, tm):
    d = x_p.shape[1]
    n = x_p.shape[0] + x_s.shape[0]
    d_mix = mix_p.shape[1]
    n_pt = mix_p.shape[0] // tm
    assert mix_p.shape[0] % tm == 0 and mix_s.shape[0] % tm == 0
    row = lambda i: (i, 0)
    fixed = lambda i: (0, 0)
    prompt_row = lambda i: (jnp.minimum(i, n_pt - 1), 0)
    sample_row = lambda i: (jnp.maximum(i - n_pt, 0), 0)
    once = pl.Buffered(1)
    kern = functools.partial(_post_mix_kernel, alpha=alpha, n_pt=n_pt)
    return pl.pallas_call(
        kern,
        grid=(n // tm,),
        in_specs=[pl.BlockSpec((tm, d_mix), prompt_row),
                  pl.BlockSpec((tm, d_mix), sample_row),
                  pl.BlockSpec((tm, d), prompt_row),
                  pl.BlockSpec((tm, d), sample_row),
                  pl.BlockSpec(w_out_b.shape, fixed, pipeline_mode=once),
                  pl.BlockSpec(w_out_f.shape, fixed, pipeline_mode=once),
                  pl.BlockSpec((1, d), fixed), pl.BlockSpec((1, d), fixed),
                  pl.BlockSpec((1, d), fixed), pl.BlockSpec((1, d), fixed),
                  pl.BlockSpec((d, LANES), fixed), pl.BlockSpec((1, LANES), fixed)],
        out_specs=[pl.BlockSpec((tm, d), row),
                   pl.BlockSpec((tm, d // 2), row),
                   pl.BlockSpec((tm, LANES), row),
                   pl.BlockSpec((tm, LANES), row),
                   pl.BlockSpec((1, LANES), fixed)],
        out_shape=[jax.ShapeDtypeStruct((n, d), F32),
                   jax.ShapeDtypeStruct((n, d // 2), jnp.uint32),
                   jax.ShapeDtypeStruct((n, LANES), jnp.int32),
                   jax.ShapeDtypeStruct((n, LANES), F32),
                   jax.ShapeDtypeStruct((1, LANES), jnp.int32)],
        scratch_shapes=[pltpu.VMEM((1, LANES), F32), pltpu.VMEM((tm, d), F32)],
        compiler_params=_cparams(("arbitrary",), 48),
        name="post_mix",
    )(mix_p, mix_s, x_p, x_s, w_out_b, w_out_f, lin_g, lin_b, l1_g, l1_b, w_r, b_r)


def _dispatch_kernel(dest_ref, padlo_ref, padhi_ref, x_ref, xs_hbm, zero_ref, sem, *, tm, n_experts):
    i = pl.program_id(0)

    def row_copy(src, dst_row):
        return pltpu.make_async_copy(src, xs_hbm.at[pl.ds(dst_row, 1), :], sem)

    @pl.when(i == 0)
    def _():
        zero_ref[...] = jnp.zeros_like(zero_ref)

        def per_expert(e, carry):
            def start(r, c):
                row_copy(zero_ref.at[pl.ds(0, 1), :], r).start()
                return c

            def wait(r, c):
                row_copy(zero_ref.at[pl.ds(0, 1), :], r).wait()
                return c

            lax.fori_loop(padlo_ref[e], padhi_ref[e], start, 0)
            lax.fori_loop(padlo_ref[e], padhi_ref[e], wait, 0)
            return carry

        lax.fori_loop(0, n_experts, per_expert, 0)

        tail0 = padhi_ref[n_experts - 1]
        n_tail = (xs_hbm.shape[0] - tail0) // ROW_BLOCK

        def tail_copy(c):
            r0 = pl.multiple_of(tail0 + c * ROW_BLOCK, ROW_BLOCK)
            return pltpu.make_async_copy(zero_ref, xs_hbm.at[pl.ds(r0, ROW_BLOCK), :], sem)

        def tail_start(c, carry):
            tail_copy(c).start()
            return carry

        def tail_wait(c, carry):
            tail_copy(c).wait()
            return carry

        lax.fori_loop(0, n_tail, tail_start, 0)
        lax.fori_loop(0, n_tail, tail_wait, 0)

    def start(t, c):
        src = x_ref.at[pl.ds(t, 1), :]
        for k in range(TOP_K):
            row_copy(src, dest_ref[t * TOP_K + k]).start()
        return c

    def wait(t, c):
        src = x_ref.at[pl.ds(t, 1), :]
        for k in range(TOP_K):
            row_copy(src, dest_ref[t * TOP_K + k]).wait()
        return c

    lax.fori_loop(0, tm, start, 0)
    lax.fori_loop(0, tm, wait, 0)


def _dispatch(x1p, dest_flat, pad_lo, pad_hi, n_slots, tm):
    n, dh = x1p.shape
    n_experts = pad_lo.shape[0]
    kern = functools.partial(_dispatch_kernel, tm=tm, n_experts=n_experts)
    return pl.pallas_call(
        kern,
        grid=(n // tm,),
        in_specs=[pl.BlockSpec((tm * TOP_K,), lambda i: (i,), memory_space=pltpu.SMEM),
                  pl.BlockSpec(memory_space=pltpu.SMEM),
                  pl.BlockSpec(memory_space=pltpu.SMEM),
                  pl.BlockSpec((tm, dh), lambda i: (i, 0))],
        out_specs=pl.BlockSpec(memory_space=pl.ANY),
        out_shape=jax.ShapeDtypeStruct((n_slots, dh), jnp.uint32),
        scratch_shapes=[pltpu.VMEM((ROW_BLOCK, dh), jnp.uint32), pltpu.SemaphoreType.DMA(())],
        compiler_params=_cparams(("arbitrary",), 32),
        name="dispatch",
    )(dest_flat, pad_lo, pad_hi, x1p)


def _expert_kernel(ie_ref, is_ref, ir_ref, xs_hbm, wg_ref, wu_ref, wd_ref, bg_ref, bu_ref, bd_ref,
                   ys_hbm, xbuf, ybuf, wgu_bf, wd_bf, sem_in, sem_out):
    del ie_ref
    i = pl.program_id(0)
    j = pl.program_id(1)
    n_j = pl.num_programs(1)
    rows = ir_ref[i]
    start = is_ref[i]
    n_chunk = rows // ROW_BLOCK
    tf = wg_ref.shape[1]

    def in_copy(c):
        r0 = pl.multiple_of(c * ROW_BLOCK, ROW_BLOCK)
        g0 = pl.multiple_of(start + c * ROW_BLOCK, ROW_BLOCK)
        return pltpu.make_async_copy(xs_hbm.at[pl.ds(g0, ROW_BLOCK), :],
                                     xbuf.at[pl.ds(r0, ROW_BLOCK), :], sem_in)

    def out_copy(c):
        r0 = pl.multiple_of(c * ROW_BLOCK, ROW_BLOCK)
        g0 = pl.multiple_of(start + c * ROW_BLOCK, ROW_BLOCK)
        return pltpu.make_async_copy(ybuf.at[pl.ds(r0, ROW_BLOCK), :],
                                     ys_hbm.at[pl.ds(g0, ROW_BLOCK), :], sem_out)

    def for_chunks(fn):
        def body(c, carry):
            fn(c)
            return carry
        lax.fori_loop(0, n_chunk, body, 0)

    @pl.when(rows > 0)
    def _():
        @pl.when(j == 0)
        def _():
            for_chunks(lambda c: in_copy(c).start())

        wgu_bf[:, 0:tf] = wg_ref[...].astype(BF16)
        wgu_bf[:, tf:2 * tf] = wu_ref[...].astype(BF16)
        wd_bf[...] = wd_ref[...].astype(BF16)

        @pl.when(j == 0)
        def _():
            for_chunks(lambda c: in_copy(c).wait())

        def chunk(c, first, last):
            r0 = pl.multiple_of(c * ROW_BLOCK, ROW_BLOCK)
            xu = xbuf[pl.ds(r0, ROW_BLOCK), :]
            lo = pltpu.bitcast(xu << 16, F32).astype(BF16)
            hi = pltpu.bitcast(xu & jnp.uint32(0xFFFF0000), F32).astype(BF16)
            x = jnp.concatenate([lo, hi], axis=1)
            gu = jnp.dot(x, wgu_bf[...], preferred_element_type=F32)
            gg = jnp.minimum(gu[:, 0:tf] + bg_ref[...], SWIGLU_LIMIT)
            uu = jnp.clip(gu[:, tf:2 * tf] + bu_ref[...], -SWIGLU_LIMIT, SWIGLU_LIMIT)
            hid = gg * _sigmoid(SWIGLU_ALPHA * gg) * (uu + 1.0)
            y = jnp.dot(hid.astype(BF16), wd_bf[...], preferred_element_type=F32)
            if not first:
                y = y + ybuf[pl.ds(r0, ROW_BLOCK), :]
            if last:
                y = y + bd_ref[...]
            ybuf[pl.ds(r0, ROW_BLOCK), :] = y
            if last:
                out_copy(c).start()

        @pl.when(j == 0)
        def _():
            for_chunks(lambda c: chunk(c, True, False))

        @pl.when(jnp.logical_and(j > 0, j < n_j - 1))
        def _():
            for_chunks(lambda c: chunk(c, False, False))

        @pl.when(j == n_j - 1)
        def _():
            for_chunks(lambda c: chunk(c, False, True))
            for_chunks(lambda c: out_copy(c).wait())

    @pl.when(jnp.logical_and(i == pl.num_programs(0) - 1, j == n_j - 1))
    def _():
        tail0 = is_ref[pl.num_programs(0)]
        n_tail = (ys_hbm.shape[0] - tail0) // ROW_BLOCK
        ybuf[0:ROW_BLOCK, :] = jnp.zeros((ROW_BLOCK, ybuf.shape[1]), F32)

        def tail_copy(c):
            g0 = pl.multiple_of(tail0 + c * ROW_BLOCK, ROW_BLOCK)
            return pltpu.make_async_copy(ybuf.at[pl.ds(0, ROW_BLOCK), :],
                                         ys_hbm.at[pl.ds(g0, ROW_BLOCK), :], sem_out)

        def tail_start(c, carry):
            tail_copy(c).start()
            return carry

        def tail_wait(c, carry):
            tail_copy(c).wait()
            return carry

        lax.fori_loop(0, n_tail, tail_start, 0)
        lax.fori_loop(0, n_tail, tail_wait, 0)


def _experts(xs, item_e, item_start, item_rows, w_gate, w_up, w_down, b_gate, b_up, b_down):
    n_slots, dh = xs.shape
    n_exp, d, d_ff = w_gate.shape
    n_items = item_e.shape[0]
    n_j = d_ff // FF_TILE
    assert n_j >= 2 and d == 2 * dh

    def jj(i, j, ir):
        return jnp.where(ir[i] > 0, j, n_j - 1)

    grid_spec = pltpu.PrefetchScalarGridSpec(
        num_scalar_prefetch=3,
        grid=(n_items, n_j),
        in_specs=[pl.BlockSpec(memory_space=pl.ANY),
                  pl.BlockSpec((None, d, FF_TILE), lambda i, j, ie, is_, ir: (ie[i], 0, jj(i, j, ir))),
                  pl.BlockSpec((None, d, FF_TILE), lambda i, j, ie, is_, ir: (ie[i], 0, jj(i, j, ir))),
                  pl.BlockSpec((None, FF_TILE, d), lambda i, j, ie, is_, ir: (ie[i], jj(i, j, ir), 0)),
                  pl.BlockSpec((None, 1, FF_TILE), lambda i, j, ie, is_, ir: (ie[i], 0, jj(i, j, ir))),
                  pl.BlockSpec((None, 1, FF_TILE), lambda i, j, ie, is_, ir: (ie[i], 0, jj(i, j, ir))),
                  pl.BlockSpec((None, 1, d), lambda i, j, ie, is_, ir: (ie[i], 0, 0))],
        out_specs=pl.BlockSpec(memory_space=pl.ANY),
        scratch_shapes=[pltpu.VMEM((ITEM_ROWS, dh), jnp.uint32),
                        pltpu.VMEM((ITEM_ROWS, d), F32),
                        pltpu.VMEM((d, 2 * FF_TILE), BF16),
                        pltpu.VMEM((FF_TILE, d), BF16),
                        pltpu.SemaphoreType.DMA(()),
                        pltpu.SemaphoreType.DMA(())],
    )
    return pl.pallas_call(
        _expert_kernel,
        grid_spec=grid_spec,
        out_shape=jax.ShapeDtypeStruct((n_slots, d), F32),
        compiler_params=_cparams(("arbitrary", "arbitrary"), 58),
        name="experts",
    )(item_e, item_start, item_rows, xs, w_gate, w_up, w_down,
      b_gate.reshape(n_exp, 1, d_ff), b_up.reshape(n_exp, 1, d_ff), b_down.reshape(n_exp, 1, d))


def _combine_kernel(dcur_ref, dnext_ref, ys_hbm, x1_ref, gate_ref, p_ref, wpg_ref, wpp_ref,
                    l2g_ref, l2b_ref, l3g_ref, l3b_ref, o_ref, gbuf, sems, *, alpha, tm):
    i = pl.program_id(0)
    n_i = pl.num_programs(0)
    slot = lax.rem(i, 2)

    def row_copy(dref, t, k, s):
        return pltpu.make_async_copy(ys_hbm.at[pl.ds(dref[t * TOP_K + k], 1), :],
                                     gbuf.at[s, k, pl.ds(t, 1), :], sems.at[s])

    def gather(dref, s, wait):
        def body(t, c):
            for k in range(TOP_K):
                cp = row_copy(dref, t, k, s)
                if wait:
                    cp.wait()
                else:
                    cp.start()
            return c
        lax.fori_loop(0, tm, body, 0)

    @pl.when(i == 0)
    def _():
        gather(dcur_ref, 0, False)

    @pl.when(i + 1 < n_i)
    def _():
        gather(dnext_ref, 1 - slot, False)

    gather(dcur_ref, slot, True)

    x1 = x1_ref[...]
    lane = lax.broadcasted_iota(jnp.int32, gate_ref.shape, 1)
    gates = gate_ref[...]
    ff = jnp.zeros_like(x1)
    for k in range(TOP_K):
        gk = jnp.sum(jnp.where(lane == k, gates, 0.0), axis=1, keepdims=True)
        ff = ff + gk * gbuf[slot, k]
    x2 = _layer_norm(alpha * x1 + ff, l2g_ref[...], l2b_ref[...])
    eg = _sigmoid(jnp.dot(x2.astype(BF16), wpg_ref[...], preferred_element_type=F32))
    ep = jnp.dot(p_ref[...].astype(BF16), wpp_ref[...], preferred_element_type=F32)
    o_ref[...] = _layer_norm(alpha * x2 + eg * ep, l3g_ref[...], l3b_ref[...])


def _combine(ys, dest_flat, x1, gates, p, w_pg, w_pp, l2_g, l2_b, l3_g, l3_b, alpha, tm):
    n, d = x1.shape
    n_i = n // tm
    row = lambda i: (i, 0)
    fixed = lambda i: (0, 0)
    kern = functools.partial(_combine_kernel, alpha=alpha, tm=tm)
    return pl.pallas_call(
        kern,
        grid=(n_i,),
        in_specs=[pl.BlockSpec((tm * TOP_K,), lambda i: (i,), memory_space=pltpu.SMEM),
                  pl.BlockSpec((tm * TOP_K,), lambda i: (jnp.minimum(i + 1, n_i - 1),),
                               memory_space=pltpu.SMEM),
                  pl.BlockSpec(memory_space=pl.ANY),
                  pl.BlockSpec((tm, d), row),
                  pl.BlockSpec((tm, LANES), row),
                  pl.BlockSpec((tm, p.shape[1]), row),
                  pl.BlockSpec(w_pg.shape, fixed),
                  pl.BlockSpec(w_pp.shape, fixed),
                  pl.BlockSpec((1, d), fixed), pl.BlockSpec((1, d), fixed),
                  pl.BlockSpec((1, d), fixed), pl.BlockSpec((1, d), fixed)],
        out_specs=pl.BlockSpec((tm, d), row),
        out_shape=jax.ShapeDtypeStruct((n, d), F32),
        scratch_shapes=[pltpu.VMEM((2, TOP_K, tm, d), F32), pltpu.SemaphoreType.DMA((2,))],
        compiler_params=_cparams(("arbitrary",), 48),
        name="combine",
    )(dest_flat, dest_flat, ys, x1, gates, p, w_pg, w_pp, l2_g, l2_b, l3_g, l3_b)


def _routing_tables(expert_idx, rank, counts, n_items):
    n_exp = counts.shape[0]
    padded = (counts + ROW_BLOCK - 1) // ROW_BLOCK * ROW_BLOCK
    pad_end = jnp.cumsum(padded)
    pad_start = pad_end - padded
    dest = (pad_start[expert_idx] + rank).reshape(-1).astype(jnp.int32)
    items_per = (padded + ITEM_ROWS - 1) // ITEM_ROWS
    item_end = jnp.cumsum(items_per)
    ids = jnp.arange(n_items, dtype=jnp.int32)
    total = item_end[-1]
    last_valid = jnp.maximum(total - 1, 0)
    eff = jnp.minimum(ids, last_valid)
    e_of = jnp.minimum(jnp.searchsorted(item_end, eff, side='right'), n_exp - 1).astype(jnp.int32)
    sub = eff - (item_end[e_of] - items_per[e_of])
    start = pad_start[e_of] + sub * ITEM_ROWS
    rows = jnp.clip(padded[e_of] - sub * ITEM_ROWS, 0, ITEM_ROWS)
    rows = jnp.where(ids < total, rows, 0)
    start = jnp.concatenate([start, pad_end[-1:]])
    return (dest, (pad_start + counts).astype(jnp.int32), pad_end.astype(jnp.int32),
            e_of, start.astype(jnp.int32), rows.astype(jnp.int32))


def kernel(x_prompt, x_sample, state_conv, state_rec, p_prompt, p_sample, ln_in_g, ln_in_b, w_in, conv_w,
           lb_theta, rms_g, w_out, ln1_g, ln1_b, w_router, b_router, w_gate, b_gate, w_up, b_up, w_down,
           b_down, ln2_g, ln2_b, w_ple_gate, w_ple_proj, ln3_g, ln3_b):
    bp, sp, d = x_prompt.shape
    bs, ss, _ = x_sample.shape
    depth = w_in.shape[0]
    assert depth == 1 and ss == 1 and sp % CHUNK == 0
    d_conv = state_conv.shape[-1]
    n_exp = w_router.shape[-1]
    n_p = bp * sp
    n = n_p + bs
    alpha = (2 * depth) ** 0.25
    row2 = lambda a: a.reshape(1, -1)

    lb = jnp.cumsum(jax.nn.softmax(lb_theta.astype(F32), axis=0), axis=0)[0]
    x_p = x_prompt.reshape(n_p, d)
    x_s = x_sample.reshape(bs, d)
    p_all = jnp.concatenate([p_prompt[0].reshape(n_p, -1), p_sample[0].reshape(bs, -1)], axis=0)

    xn_p = _ln_in(x_p, row2(ln_in_g), row2(ln_in_b), tm=512)
    proj_p = _in_proj(xn_p, w_in[0].astype(BF16), tm=1024, tn=1024)
    proj_s = _in_proj_sample(x_s, row2(ln_in_g), row2(ln_in_b), w_in[0], tn=1024)

    mix_p, conv_tail, rec_p = _mix_prompt(proj_p, conv_w[0], row2(lb), row2(rms_g[0]), n_p, bp, sp, d_conv)
    mix_s, conv_s, rec_s = _mix_sample(proj_s, state_conv[0].reshape(bs, -1), state_rec[0], conv_w[0],
                                       row2(lb), row2(rms_g[0]), d_conv)

    w_r = jnp.zeros((d, LANES), F32).at[:, :n_exp].set(w_router[0])
    b_r = jnp.full((1, LANES), NEG_BIG, F32).at[0, :n_exp].set(b_router[0])
    x1, x1p, route, gates, counts = _post_mix(
        mix_p, mix_s, x_p, x_s, w_out[0].astype(BF16), w_out[0], row2(ln_in_g), row2(ln_in_b),
        row2(ln1_g[0]), row2(ln1_b[0]), w_r, b_r, alpha, tm=128)

    n_slots = n * TOP_K + n_exp * ROW_BLOCK
    n_items = n_exp + n_slots // ITEM_ROWS
    dest, pad_lo, pad_hi, item_e, item_start, item_rows = _routing_tables(
        route[:, 0:TOP_K], route[:, TOP_K:2 * TOP_K], counts[0, :n_exp], n_items)

    xs = _dispatch(x1p, dest, pad_lo, pad_hi, n_slots, tm=128)
    ys = _experts(xs, item_e, item_start, item_rows, w_gate[0], w_up[0], w_down[0],
                  b_gate[0], b_up[0], b_down[0])
    y_all = _combine(ys, dest, x1, gates, p_all, w_ple_gate[0].astype(BF16), w_ple_proj[0].astype(BF16),
                     row2(ln2_g[0]), row2(ln2_b[0]), row2(ln3_g[0]), row2(ln3_b[0]), alpha, tm=128)

    return (y_all[:n_p].reshape(bp, sp, d),
            y_all[n_p:].reshape(bs, ss, d),
            conv_tail[:, SUBLANES - (CONV_W - 1):, :][None],
            rec_p[None],
            conv_s.reshape(bs, CONV_W - 1, d_conv)[None],
            rec_s[None])
```

```python
import functools

import numpy as np
import jax
import jax.numpy as jnp
from jax import lax
from jax.experimental import pallas as pl
from jax.experimental.pallas import tpu as pltpu

F32 = jnp.float32
BF16 = jnp.bfloat16
HIGHEST = lax.Precision.HIGHEST

CONV_W = 3
N_HEADS = 8
HEAD_K = 128
HEAD_V = 128
TOP_K = 4
SWIGLU_LIMIT = 7.0
SWIGLU_ALPHA = 1.702
LN_EPS = 1e-5
RMS_EPS = 1e-6

LANES = 128
SUBLANES = 8
VMEM_PHYSICAL_BYTES = 64 * 1024 * 1024

CHUNK = 64
SAMPLE_BLOCK = 16
ROW_BLOCK = 128
MAX_BLOCK = 512
ITEM_ROWS = 1280
FF_TILE = 512
NEG_BIG = -1e30


def _cparams(sem, vmem_mb):
    return pltpu.CompilerParams(dimension_semantics=sem, vmem_limit_bytes=vmem_mb * 1024 * 1024)


def _layer_norm(x, g, b):
    mu = jnp.mean(x, axis=-1, keepdims=True)
    xc = x - mu
    var = jnp.mean(xc * xc, axis=-1, keepdims=True)
    return xc * lax.rsqrt(var + LN_EPS) * g + b


def _sigmoid(x):
    return 1.0 / (1.0 + jnp.exp(-x))


def _split3(x, axis):
    p1 = x.astype(BF16)
    r1 = x - p1.astype(F32)
    p2 = r1.astype(BF16)
    p3 = (r1 - p2.astype(F32)).astype(BF16)
    return jnp.concatenate([p1, p2, p3], axis=axis)


def _ln_in_kernel(x_ref, g_ref, b_ref, o_ref):
    o_ref[...] = _layer_norm(x_ref[...], g_ref[...], b_ref[...]).astype(BF16)


def _ln_in(x, g, b, tm):
    n, d = x.shape
    return pl.pallas_call(
        _ln_in_kernel,
        grid=(n // tm,),
        in_specs=[pl.BlockSpec((tm, d), lambda i: (i, 0)),
                  pl.BlockSpec((1, d), lambda i: (0, 0)),
                  pl.BlockSpec((1, d), lambda i: (0, 0))],
        out_specs=pl.BlockSpec((tm, d), lambda i: (i, 0)),
        out_shape=jax.ShapeDtypeStruct((n, d), BF16),
        compiler_params=_cparams(("parallel",), 40),
        name="ln_in",
    )(x, g, b)


def _matmul_kernel(x_ref, w_ref, o_ref):
    o_ref[...] = jnp.dot(x_ref[...], w_ref[...], preferred_element_type=F32)


def _in_proj(xn, w, tm, tn):
    n, d = xn.shape
    d_in = w.shape[1]
    return pl.pallas_call(
        _matmul_kernel,
        grid=(d_in // tn, n // tm),
        in_specs=[pl.BlockSpec((tm, d), lambda j, i: (i, 0)),
                  pl.BlockSpec((d, tn), lambda j, i: (0, j))],
        out_specs=pl.BlockSpec((tm, tn), lambda j, i: (i, j)),
        out_shape=jax.ShapeDtypeStruct((n, d_in), F32),
        compiler_params=_cparams(("parallel", "parallel"), 48),
        name="in_proj",
    )(xn, w)


def _in_proj_sample_kernel(x_ref, g_ref, b_ref, w_ref, o_ref):
    xn = _layer_norm(x_ref[...], g_ref[...], b_ref[...])
    o_ref[...] = jnp.dot(xn, w_ref[...], precision=HIGHEST, preferred_element_type=F32)


def _in_proj_sample(x, g, b, w, tn):
    n, d = x.shape
    d_in = w.shape[1]
    return pl.pallas_call(
        _in_proj_sample_kernel,
        grid=(d_in // tn,),
        in_specs=[pl.BlockSpec((n, d), lambda j: (0, 0)),
                  pl.BlockSpec((1, d), lambda j: (0, 0)),
                  pl.BlockSpec((1, d), lambda j: (0, 0)),
                  pl.BlockSpec((d, tn), lambda j: (0, j))],
        out_specs=pl.BlockSpec((n, tn), lambda j: (0, j)),
        out_shape=jax.ShapeDtypeStruct((n, d_in), F32),
        compiler_params=_cparams(("parallel",), 40),
        name="in_proj_sample",
    )(x, g, b, w)


def _forget_gates(fz, lb):
    e = jnp.exp(-jnp.abs(fz))
    r = 1.0 / (1.0 + e)
    er = e * r
    pos = fz >= 0
    sig_p = jnp.where(pos, r, er)
    sig_n = jnp.where(pos, er, r)
    oml = 1.0 - lb
    return lb + oml * sig_p, oml * sig_n


def _chunk_matrices(c):
    t = np.arange(c)[:, None]
    j = np.arange(c)[None, :]
    mats = [(j <= t), (j > t)]
    blk = c
    while blk >= 2:
        half = blk // 2
        mid = (t // blk) * blk + half
        second = (t % blk) >= half
        m_q = (j >= mid) & (j <= t)
        m_k = (j > t) & (j < mid)
        mats.append(np.where(second, m_q, m_k))
        blk = half
    return np.concatenate(mats, axis=0).astype(np.float32)


def _mix_prompt_kernel(proj_ref, convw_ref, lb_ref, rmsg_ref, cmat_ref,
                       mix_ref, convst_ref, recst_ref, s_ref, carry_ref, *, d_conv):
    c = CHUNK
    tb = pl.program_id(1)
    n_tb = pl.num_programs(1)

    @pl.when(tb == 0)
    def _():
        s_ref[...] = jnp.zeros_like(s_ref)
        carry_ref[...] = jnp.zeros_like(carry_ref)

    u = proj_ref[:, 0:d_conv] * proj_ref[:, 2 * d_conv:3 * d_conv]
    row = lax.broadcasted_iota(jnp.int32, u.shape, 0)
    prev1 = carry_ref[SUBLANES - 1:SUBLANES, :]
    prev2 = carry_ref[SUBLANES - 2:SUBLANES - 1, :]
    u1 = jnp.where(row == 0, prev1, pltpu.roll(u, 1, 0))
    u2 = jnp.where(row == 0, prev2, jnp.where(row == 1, prev1, pltpu.roll(u, 2, 0)))
    y = convw_ref[0:1, :] * u2 + convw_ref[1:2, :] * u1 + convw_ref[2:3, :] * u
    mix_ref[:, 0:d_conv] = (proj_ref[:, d_conv:2 * d_conv] * y).astype(BF16)
    carry_ref[...] = u[c - SUBLANES:c, :]
    convst_ref[0] = u[c - SUBLANES:c, :]

    o0 = 3 * d_conv
    d_rec = N_HEADS * HEAD_K
    q = proj_ref[:, o0:o0 + d_rec]
    fz = proj_ref[:, o0 + d_rec:o0 + 2 * d_rec]
    v = proj_ref[:, o0 + 2 * d_rec:o0 + 3 * d_rec]
    g = proj_ref[:, o0 + 3 * d_rec:o0 + 4 * d_rec]
    f, kk = _forget_gates(fz, lb_ref[...])
    contract0 = (((0,), (0,)), ((), ()))
    logf3 = _split3(jnp.log(f), axis=0)
    ex = jnp.dot(cmat_ref[...], logf3, preferred_element_type=F32)
    b_cum = ex[0:c]
    d_end = ex[c:2 * c]
    n_lev = cmat_ref.shape[0] // c - 2
    b_cols = lax.dot_general(logf3, jnp.ones((3 * c, HEAD_V), BF16), contract0,
                             preferred_element_type=F32)

    trow = lax.broadcasted_iota(jnp.int32, (c, c), 0)
    tcol = lax.broadcasted_iota(jnp.int32, (c, c), 1)
    prow = lax.broadcasted_iota(jnp.int32, (c, HEAD_K), 0)
    contract1 = (((1,), (1,)), ((), ()))

    for h in range(N_HEADS):
        sl = slice(h * HEAD_K, (h + 1) * HEAD_K)
        qh, kh, vh = q[:, sl], kk[:, sl], v[:, sl]
        vb = vh.astype(BF16)
        s_old = s_ref[h]
        o = jnp.dot((qh * jnp.exp(b_cum[:, sl])).astype(BF16), s_old.astype(BF16),
                    preferred_element_type=F32)
        sc = jnp.zeros((c, c), F32)
        for lev in range(n_lev):
            blk = c >> lev
            sh = blk.bit_length() - 1
            dl = jnp.exp(ex[(2 + lev) * c:(3 + lev) * c, sl])
            second = (prow & (blk - 1)) >= (blk // 2)
            qt = jnp.where(second, qh * dl, 0.0).astype(BF16)
            kt = jnp.where(second, 0.0, kh * dl).astype(BF16)
            s_l = lax.dot_general(qt, kt, contract1, preferred_element_type=F32)
            sc = sc + jnp.where((trow >> sh) == (tcol >> sh), s_l, 0.0)
        o = o + jnp.dot(sc.astype(BF16), vb, preferred_element_type=F32)
        o = o + jnp.sum(qh * kh, axis=1, keepdims=True) * vh
        khat = (kh * jnp.exp(d_end[:, sl])).astype(BF16)
        upd = lax.dot_general(khat, vb, contract0, preferred_element_type=F32)
        s_ref[h] = jnp.exp(b_cols[sl, :]) * s_old + upd
        on = o * lax.rsqrt(jnp.mean(o * o, axis=1, keepdims=True) + RMS_EPS) * rmsg_ref[:, sl]
        gh = g[:, sl]
        mix_ref[:, d_conv + h * HEAD_V:d_conv + (h + 1) * HEAD_V] = (
            on * (gh * _sigmoid(gh))).astype(BF16)

    @pl.when(tb == n_tb - 1)
    def _():
        recst_ref[0] = s_ref[...]


def _mix_prompt(proj, conv_w, lb, rms_g, n_tok, bsz, seq, d_conv):
    d_in = proj.shape[1]
    d_mix = d_conv + N_HEADS * HEAD_V
    n_tb = seq // CHUNK
    cmat = jnp.asarray(np.tile(_chunk_matrices(CHUNK), (1, 3)), dtype=BF16)
    kern = functools.partial(_mix_prompt_kernel, d_conv=d_conv)
    return pl.pallas_call(
        kern,
        grid=(bsz, n_tb),
        in_specs=[pl.BlockSpec((CHUNK, d_in), lambda b, t: (b * n_tb + t, 0)),
                  pl.BlockSpec((CONV_W, d_conv), lambda b, t: (0, 0)),
                  pl.BlockSpec((1, N_HEADS * HEAD_K), lambda b, t: (0, 0)),
                  pl.BlockSpec((1, N_HEADS * HEAD_V), lambda b, t: (0, 0)),
                  pl.BlockSpec(cmat.shape, lambda b, t: (0, 0))],
        out_specs=[pl.BlockSpec((CHUNK, d_mix), lambda b, t: (b * n_tb + t, 0)),
                   pl.BlockSpec((1, SUBLANES, d_conv), lambda b, t: (b, 0, 0)),
                   pl.BlockSpec((1, N_HEADS, HEAD_K, HEAD_V), lambda b, t: (b, 0, 0, 0))],
        out_shape=[jax.ShapeDtypeStruct((n_tok, d_mix), BF16),
                   jax.ShapeDtypeStruct((bsz, SUBLANES, d_conv), F32),
                   jax.ShapeDtypeStruct((bsz, N_HEADS, HEAD_K, HEAD_V), F32)],
        scratch_shapes=[pltpu.VMEM((N_HEADS, HEAD_K, HEAD_V), F32),
                        pltpu.VMEM((SUBLANES, d_conv), F32)],
        compiler_params=_cparams(("parallel", "arbitrary"), 40),
        name="mix_prompt",
    )(proj, conv_w, lb, rms_g, cmat)


def _mix_sample_kernel(proj_ref, cst_ref, rst_ref, convw_ref, lb_ref, rmsg_ref, sel_ref,
                       mix_ref, cnew_ref, rnew_ref, *, d_conv):
    nb = SAMPLE_BLOCK
    u = proj_ref[:, 0:d_conv] * proj_ref[:, 2 * d_conv:3 * d_conv]
    buf0 = cst_ref[:, 0:d_conv]
    buf1 = cst_ref[:, d_conv:2 * d_conv]
    y = convw_ref[0:1, :] * buf0 + convw_ref[1:2, :] * buf1 + convw_ref[2:3, :] * u
    mix_ref[:, 0:d_conv] = proj_ref[:, d_conv:2 * d_conv] * y
    cnew_ref[:, 0:d_conv] = buf1
    cnew_ref[:, d_conv:2 * d_conv] = u

    o0 = 3 * d_conv
    d_rec = N_HEADS * HEAD_K
    q = proj_ref[:, o0:o0 + d_rec]
    fz = proj_ref[:, o0 + d_rec:o0 + 2 * d_rec]
    v = proj_ref[:, o0 + 2 * d_rec:o0 + 3 * d_rec]
    g = proj_ref[:, o0 + 3 * d_rec:o0 + 4 * d_rec]
    f, kk = _forget_gates(fz, lb_ref[...])
    contract0 = (((0,), (0,)), ((), ()))
    sel = sel_ref[...]
    row = lax.broadcasted_iota(jnp.int32, (nb, HEAD_V), 0)

    def columns(a):
        return lax.dot_general(_split3(a, axis=0), sel, contract0, preferred_element_type=F32)

    for h in range(N_HEADS):
        sl = slice(h * HEAD_K, (h + 1) * HEAD_K)
        f_c, k_c, q_c = columns(f[:, sl]), columns(kk[:, sl]), columns(q[:, sl])
        o = jnp.zeros((nb, HEAD_V), F32)
        for n in range(nb):
            nl = slice(n * HEAD_V, (n + 1) * HEAD_V)
            s_new = f_c[:, nl] * rst_ref[n, h] + k_c[:, nl] * v[n:n + 1, sl]
            rnew_ref[n, h] = s_new
            o_row = jnp.sum(q_c[:, nl] * s_new, axis=0, keepdims=True)
            o = jnp.where(row == n, o_row, o)
        on = o * lax.rsqrt(jnp.mean(o * o, axis=1, keepdims=True) + RMS_EPS) * rmsg_ref[:, sl]
        gh = g[:, sl]
        mix_ref[:, d_conv + h * HEAD_V:d_conv + (h + 1) * HEAD_V] = on * (gh * _sigmoid(gh))


def _mix_sample(proj, conv_state, rec_state, conv_w, lb, rms_g, d_conv):
    n_seq = conv_state.shape[0]
    d_in = proj.shape[1]
    d_mix = d_conv + N_HEADS * HEAD_V
    nb = SAMPLE_BLOCK
    sel = jnp.asarray(np.tile(np.kron(np.eye(nb), np.ones((1, HEAD_V))), (3, 1)), dtype=BF16)
    kern = functools.partial(_mix_sample_kernel, d_conv=d_conv)
    return pl.pallas_call(
        kern,
        grid=(n_seq // nb,),
        in_specs=[pl.BlockSpec((nb, d_in), lambda i: (i, 0)),
                  pl.BlockSpec((nb, 2 * d_conv), lambda i: (i, 0)),
                  pl.BlockSpec((nb, N_HEADS, HEAD_K, HEAD_V), lambda i: (i, 0, 0, 0)),
                  pl.BlockSpec((CONV_W, d_conv), lambda i: (0, 0)),
                  pl.BlockSpec((1, N_HEADS * HEAD_K), lambda i: (0, 0)),
                  pl.BlockSpec((1, N_HEADS * HEAD_V), lambda i: (0, 0)),
                  pl.BlockSpec(sel.shape, lambda i: (0, 0))],
        out_specs=[pl.BlockSpec((nb, d_mix), lambda i: (i, 0)),
                   pl.BlockSpec((nb, 2 * d_conv), lambda i: (i, 0)),
                   pl.BlockSpec((nb, N_HEADS, HEAD_K, HEAD_V), lambda i: (i, 0, 0, 0))],
        out_shape=[jax.ShapeDtypeStruct((n_seq, d_mix), F32),
                   jax.ShapeDtypeStruct((n_seq, 2 * d_conv), F32),
                   jax.ShapeDtypeStruct(rec_state.shape, F32)],
        compiler_params=_cparams(("parallel",), 52),
        name="mix_sample",
    )(proj, conv_state, rec_state, conv_w, lb, rms_g, sel)


def _post_mix_kernel(mixp_ref, mixs_ref, xp_ref, xs_ref, woutb_ref, woutf_ref, ling_ref, linb_ref,
                     l1g_ref, l1b_ref, wr_ref, wr2_ref, br_ref,
                     x1_ref, x1p_ref, route_ref, gate_ref, cnt_ref, run_ref, h_ref, lg_ref,
                     *, alpha, n_pt):
    i = pl.program_id(0)

    @pl.when(i == 0)
    def _():
        run_ref[...] = jnp.zeros_like(run_ref)

    @pl.when(i < n_pt)
    def _():
        h_ref[...] = jnp.dot(mixp_ref[...], woutb_ref[...], preferred_element_type=F32)

    @pl.when(i >= n_pt)
    def _():
        h_ref[...] = jnp.dot(mixs_ref[...], woutf_ref[...], precision=HIGHEST,
                             preferred_element_type=F32)

    x = jnp.where(i < n_pt, xp_ref[...], xs_ref[...])
    xn = _layer_norm(x, ling_ref[...], linb_ref[...])
    x1 = _layer_norm(alpha * xn + h_ref[...], l1g_ref[...], l1b_ref[...])
    x1_ref[...] = x1
    half = x1.shape[1] // 2
    bits = pltpu.bitcast(x1.astype(BF16).astype(F32), jnp.uint32)
    x1p_ref[...] = (bits[:, half:] & jnp.uint32(0xFFFF0000)) | (bits[:, :half] >> 16)

    tm = x1.shape[0]

    @pl.when(i < n_pt)
    def _():
        xh = x1.astype(BF16)
        xl = (x1 - xh.astype(F32)).astype(BF16)
        pr = jnp.dot(jnp.concatenate([xh, xl], axis=0), wr2_ref[...], preferred_element_type=F32)
        lg_ref[...] = (pr[0:tm, 0:LANES] + pr[0:tm, LANES:2 * LANES]
                       + pr[tm:2 * tm, 0:LANES] + pr[tm:2 * tm, LANES:2 * LANES])

    @pl.when(i >= n_pt)
    def _():
        lg_ref[...] = jnp.dot(x1, wr_ref[...], precision=HIGHEST, preferred_element_type=F32)

    logits = lg_ref[...] + br_ref[...]
    lane = lax.broadcasted_iota(jnp.int32, (tm, LANES), 1)
    lane_f = lane.astype(F32)
    work = logits
    vals, idxs = [], []
    for _ in range(TOP_K):
        m = jnp.max(work, axis=1, keepdims=True)
        ix = jnp.min(jnp.where(work == m, lane_f, float(LANES)), axis=1, keepdims=True)
        vals.append(m)
        idxs.append(ix)
        work = jnp.where(lane_f == ix, NEG_BIG, work)
    ex = [jnp.exp(vv - vals[0]) for vv in vals]
    den = ex[0] + ex[1] + ex[2] + ex[3]
    onehots = [(lane_f == ix).astype(F32) for ix in idxs]
    oh = onehots[0] + onehots[1] + onehots[2] + onehots[3]
    tr = lax.broadcasted_iota(jnp.int32, (tm, tm), 0)
    tc = lax.broadcasted_iota(jnp.int32, (tm, tm), 1)
    before = jnp.dot((tc < tr).astype(BF16), oh.astype(BF16), preferred_element_type=F32)
    pos = before + run_ref[...]
    route = jnp.zeros((tm, LANES), F32)
    gates = jnp.zeros((tm, LANES), F32)
    for k in range(TOP_K):
        rank = jnp.sum(onehots[k] * pos, axis=1, keepdims=True)
        route = jnp.where(lane == k, idxs[k], route)
        route = jnp.where(lane == TOP_K + k, rank, route)
        gates = jnp.where(lane == k, ex[k] / den, gates)
    route_ref[...] = route.astype(jnp.int32)
    gate_ref[...] = gates
    run_ref[...] = run_ref[...] + jnp.sum(oh, axis=0, keepdims=True)
    cnt_ref[...] = run_ref[...].astype(jnp.int32)


def _post_mix(mix_p, mix_s, x_p, x_s, w_out_b, w_out_f, lin_g, lin_b, l1_g, l1_b, w_r, w_r2, b_r, alpha, tm):
    d = x_p.shape[1]
    n = x_p.shape[0] + x_s.shape[0]
    d_mix = mix_p.shape[1]
    n_pt = mix_p.shape[0] // tm
    assert mix_p.shape[0] % tm == 0 and mix_s.shape[0] % tm == 0
    row = lambda i: (i, 0)
    fixed = lambda i: (0, 0)
    prompt_row = lambda i: (jnp.minimum(i, n_pt - 1), 0)
    sample_row = lambda i: (jnp.maximum(i - n_pt, 0), 0)
    once = pl.Buffered(1)
    kern = functools.partial(_post_mix_kernel, alpha=alpha, n_pt=n_pt)
    return pl.pallas_call(
        kern,
        grid=(n // tm,),
        in_specs=[pl.BlockSpec((tm, d_mix), prompt_row),
                  pl.BlockSpec((tm, d_mix), sample_row),
                  pl.BlockSpec((tm, d), prompt_row),
                  pl.BlockSpec((tm, d), sample_row),
                  pl.BlockSpec(w_out_b.shape, fixed, pipeline_mode=once),
                  pl.BlockSpec(w_out_f.shape, fixed, pipeline_mode=once),
                  pl.BlockSpec((1, d), fixed), pl.BlockSpec((1, d), fixed),
                  pl.BlockSpec((1, d), fixed), pl.BlockSpec((1, d), fixed),
                  pl.BlockSpec((d, LANES), fixed), pl.BlockSpec((d, 2 * LANES), fixed),
                  pl.BlockSpec((1, LANES), fixed)],
        out_specs=[pl.BlockSpec((tm, d), row),
                   pl.BlockSpec((tm, d // 2), row),
                   pl.BlockSpec((tm, LANES), row),
                   pl.BlockSpec((tm, LANES), row),
                   pl.BlockSpec((1, LANES), fixed)],
        out_shape=[jax.ShapeDtypeStruct((n, d), F32),
                   jax.ShapeDtypeStruct((n, d // 2), jnp.uint32),
                   jax.ShapeDtypeStruct((n, LANES), jnp.int32),
                   jax.ShapeDtypeStruct((n, LANES), F32),
                   jax.ShapeDtypeStruct((1, LANES), jnp.int32)],
        scratch_shapes=[pltpu.VMEM((1, LANES), F32), pltpu.VMEM((tm, d), F32),
                        pltpu.VMEM((tm, LANES), F32)],
        compiler_params=_cparams(("arbitrary",), 48),
        name="post_mix",
    )(mix_p, mix_s, x_p, x_s, w_out_b, w_out_f, lin_g, lin_b, l1_g, l1_b, w_r, w_r2, b_r)


def _dispatch_kernel(dest_ref, padlo_ref, padhi_ref, x_ref, xs_hbm, zero_ref, sem, *, tm, n_experts):
    i = pl.program_id(0)

    def row_copy(src, dst_row):
        return pltpu.make_async_copy(src, xs_hbm.at[pl.ds(dst_row, 1), :], sem)

    @pl.when(i == 0)
    def _():
        zero_ref[...] = jnp.zeros_like(zero_ref)

        def per_expert(e, carry):
            def start(r, c):
                row_copy(zero_ref.at[pl.ds(0, 1), :], r).start()
                return c

            def wait(r, c):
                row_copy(zero_ref.at[pl.ds(0, 1), :], r).wait()
                return c

            lax.fori_loop(padlo_ref[e], padhi_ref[e], start, 0)
            lax.fori_loop(padlo_ref[e], padhi_ref[e], wait, 0)
            return carry

        lax.fori_loop(0, n_experts, per_expert, 0)

        tail0 = padhi_ref[n_experts - 1]
        n_tail = (xs_hbm.shape[0] - tail0) // ROW_BLOCK

        def tail_copy(c):
            r0 = pl.multiple_of(tail0 + c * ROW_BLOCK, ROW_BLOCK)
            return pltpu.make_async_copy(zero_ref, xs_hbm.at[pl.ds(r0, ROW_BLOCK), :], sem)

        def tail_start(c, carry):
            tail_copy(c).start()
            return carry

        def tail_wait(c, carry):
            tail_copy(c).wait()
            return carry

        lax.fori_loop(0, n_tail, tail_start, 0)
        lax.fori_loop(0, n_tail, tail_wait, 0)

    def start(t, c):
        src = x_ref.at[pl.ds(t, 1), :]
        for k in range(TOP_K):
            row_copy(src, dest_ref[t * TOP_K + k]).start()
        return c

    def wait(t, c):
        src = x_ref.at[pl.ds(t, 1), :]
        for k in range(TOP_K):
            row_copy(src, dest_ref[t * TOP_K + k]).wait()
        return c

    lax.fori_loop(0, tm, start, 0, unroll=4)
    lax.fori_loop(0, tm, wait, 0, unroll=4)


def _dispatch(x1p, dest_flat, pad_lo, pad_hi, n_slots, tm):
    n, dh = x1p.shape
    n_experts = pad_lo.shape[0]
    kern = functools.partial(_dispatch_kernel, tm=tm, n_experts=n_experts)
    return pl.pallas_call(
        kern,
        grid=(n // tm,),
        in_specs=[pl.BlockSpec((tm * TOP_K,), lambda i: (i,), memory_space=pltpu.SMEM),
                  pl.BlockSpec(memory_space=pltpu.SMEM),
                  pl.BlockSpec(memory_space=pltpu.SMEM),
                  pl.BlockSpec((tm, dh), lambda i: (i, 0))],
        out_specs=pl.BlockSpec(memory_space=pl.ANY),
        out_shape=jax.ShapeDtypeStruct((n_slots, dh), jnp.uint32),
        scratch_shapes=[pltpu.VMEM((ROW_BLOCK, dh), jnp.uint32), pltpu.SemaphoreType.DMA(())],
        compiler_params=_cparams(("arbitrary",), 32),
        name="dispatch",
    )(dest_flat, pad_lo, pad_hi, x1p)


def _expert_kernel(ie_ref, is_ref, ir_ref, xs_hbm, wg_ref, wu_ref, wd_ref, bg_ref, bu_ref, bd_ref,
                   ys_hbm, xbuf, ybuf, wgu_bf, wd_bf, sem_in, sem_out):
    del ie_ref
    i = pl.program_id(0)
    j = pl.program_id(1)
    n_j = pl.num_programs(1)
    rows = ir_ref[i]
    start = is_ref[i]
    tf = wg_ref.shape[1]

    def in_copy(r0, size):
        g0 = pl.multiple_of(start + r0, ROW_BLOCK)
        return pltpu.make_async_copy(xs_hbm.at[pl.ds(g0, size), :], xbuf.at[pl.ds(r0, size), :], sem_in)

    def out_copy(r0, size):
        g0 = pl.multiple_of(start + r0, ROW_BLOCK)
        return pltpu.make_async_copy(ybuf.at[pl.ds(r0, size), :], ys_hbm.at[pl.ds(g0, size), :], sem_out)

    def for_blocks(fn):
        n_big = rows // MAX_BLOCK

        def body(c, carry):
            fn(pl.multiple_of(c * MAX_BLOCK, MAX_BLOCK), MAX_BLOCK)
            return carry
        lax.fori_loop(0, n_big, body, 0)
        base = n_big * MAX_BLOCK
        size = MAX_BLOCK // 2
        while size >= ROW_BLOCK:
            has = (rows & size) != 0

            @pl.when(has)
            def _(base=base, size=size):
                fn(pl.multiple_of(base, ROW_BLOCK), size)
            base = base + jnp.where(has, size, 0)
            size //= 2

    @pl.when(rows > 0)
    def _():
        @pl.when(j == 0)
        def _():
            for_blocks(lambda r0, size: in_copy(r0, size).start())

        wgu_bf[:, 0:tf] = wg_ref[...].astype(BF16)
        wgu_bf[:, tf:2 * tf] = wu_ref[...].astype(BF16)
        wd_bf[...] = wd_ref[...].astype(BF16)

        @pl.when(j == 0)
        def _():
            for_blocks(lambda r0, size: in_copy(r0, size).wait())

        def block(r0, size, first, last):
            xu = xbuf[pl.ds(r0, size), :]
            lo = pltpu.bitcast(xu << 16, F32).astype(BF16)
            hi = pltpu.bitcast(xu & jnp.uint32(0xFFFF0000), F32).astype(BF16)
            x = jnp.concatenate([lo, hi], axis=1)
            gu = jnp.dot(x, wgu_bf[...], preferred_element_type=F32)
            gg = jnp.minimum(gu[:, 0:tf] + bg_ref[...], SWIGLU_LIMIT)
            uu = jnp.clip(gu[:, tf:2 * tf] + bu_ref[...], -SWIGLU_LIMIT, SWIGLU_LIMIT)
            hid = gg * _sigmoid(SWIGLU_ALPHA * gg) * (uu + 1.0)
            y = jnp.dot(hid.astype(BF16), wd_bf[...], preferred_element_type=F32)
            if not first:
                y = y + ybuf[pl.ds(r0, size), :]
            if last:
                y = y + bd_ref[...]
            ybuf[pl.ds(r0, size), :] = y
            if last:
                out_copy(r0, size).start()

        @pl.when(j == 0)
        def _():
            for_blocks(lambda r0, size: block(r0, size, True, False))

        @pl.when(jnp.logical_and(j > 0, j < n_j - 1))
        def _():
            for_blocks(lambda r0, size: block(r0, size, False, False))

        @pl.when(j == n_j - 1)
        def _():
            for_blocks(lambda r0, size: block(r0, size, False, True))
            for_blocks(lambda r0, size: out_copy(r0, size).wait())

    @pl.when(jnp.logical_and(i == pl.num_programs(0) - 1, j == n_j - 1))
    def _():
        tail0 = is_ref[pl.num_programs(0)]
        n_tail = (ys_hbm.shape[0] - tail0) // ROW_BLOCK
        ybuf[0:ROW_BLOCK, :] = jnp.zeros((ROW_BLOCK, ybuf.shape[1]), F32)

        def tail_copy(c):
            g0 = pl.multiple_of(tail0 + c * ROW_BLOCK, ROW_BLOCK)
            return pltpu.make_async_copy(ybuf.at[pl.ds(0, ROW_BLOCK), :],
                                         ys_hbm.at[pl.ds(g0, ROW_BLOCK), :], sem_out)

        def tail_start(c, carry):
            tail_copy(c).start()
            return carry

        def tail_wait(c, carry):
            tail_copy(c).wait()
            return carry

        lax.fori_loop(0, n_tail, tail_start, 0)
        lax.fori_loop(0, n_tail, tail_wait, 0)


def _experts(xs, item_e, item_start, item_rows, w_gate, w_up, w_down, b_gate, b_up, b_down):
    n_slots, dh = xs.shape
    n_exp, d, d_ff = w_gate.shape
    n_items = item_e.shape[0]
    n_j = d_ff // FF_TILE
    assert n_j >= 2 and d == 2 * dh

    def jj(i, j, ir):
        return jnp.where(ir[i] > 0, j, n_j - 1)

    grid_spec = pltpu.PrefetchScalarGridSpec(
        num_scalar_prefetch=3,
        grid=(n_items, n_j),
        in_specs=[pl.BlockSpec(memory_space=pl.ANY),
                  pl.BlockSpec((None, d, FF_TILE), lambda i, j, ie, is_, ir: (ie[i], 0, jj(i, j, ir))),
                  pl.BlockSpec((None, d, FF_TILE), lambda i, j, ie, is_, ir: (ie[i], 0, jj(i, j, ir))),
                  pl.BlockSpec((None, FF_TILE, d), lambda i, j, ie, is_, ir: (ie[i], jj(i, j, ir), 0)),
                  pl.BlockSpec((None, 1, FF_TILE), lambda i, j, ie, is_, ir: (ie[i], 0, jj(i, j, ir))),
                  pl.BlockSpec((None, 1, FF_TILE), lambda i, j, ie, is_, ir: (ie[i], 0, jj(i, j, ir))),
                  pl.BlockSpec((None, 1, d), lambda i, j, ie, is_, ir: (ie[i], 0, 0))],
        out_specs=pl.BlockSpec(memory_space=pl.ANY),
        scratch_shapes=[pltpu.VMEM((ITEM_ROWS, dh), jnp.uint32),
                        pltpu.VMEM((ITEM_ROWS, d), F32),
                        pltpu.VMEM((d, 2 * FF_TILE), BF16),
                        pltpu.VMEM((FF_TILE, d), BF16),
                        pltpu.SemaphoreType.DMA(()),
                        pltpu.SemaphoreType.DMA(())],
    )
    return pl.pallas_call(
        _expert_kernel,
        grid_spec=grid_spec,
        out_shape=jax.ShapeDtypeStruct((n_slots, d), F32),
        compiler_params=_cparams(("arbitrary", "arbitrary"), 58),
        name="experts",
    )(item_e, item_start, item_rows, xs, w_gate, w_up, w_down,
      b_gate.reshape(n_exp, 1, d_ff), b_up.reshape(n_exp, 1, d_ff), b_down.reshape(n_exp, 1, d))


def _combine_kernel(dcur_ref, dnext_ref, ys_hbm, x1_ref, gate_ref, p_ref, wpg_ref, wpp_ref,
                    l2g_ref, l2b_ref, l3g_ref, l3b_ref, o_ref, gbuf, sems, *, alpha, tm):
    i = pl.program_id(0)
    n_i = pl.num_programs(0)
    slot = lax.rem(i, 2)

    def row_copy(dref, t, k, s):
        return pltpu.make_async_copy(ys_hbm.at[pl.ds(dref[t * TOP_K + k], 1), :],
                                     gbuf.at[s, k, pl.ds(t, 1), :], sems.at[s])

    def gather(dref, s, wait):
        def body(t, c):
            for k in range(TOP_K):
                cp = row_copy(dref, t, k, s)
                if wait:
                    cp.wait()
                else:
                    cp.start()
            return c
        lax.fori_loop(0, tm, body, 0, unroll=4)

    @pl.when(i == 0)
    def _():
        gather(dcur_ref, 0, False)

    @pl.when(i + 1 < n_i)
    def _():
        gather(dnext_ref, 1 - slot, False)

    gather(dcur_ref, slot, True)

    x1 = x1_ref[...]
    lane = lax.broadcasted_iota(jnp.int32, gate_ref.shape, 1)
    gates = gate_ref[...]
    ff = jnp.zeros_like(x1)
    for k in range(TOP_K):
        gk = jnp.sum(jnp.where(lane == k, gates, 0.0), axis=1, keepdims=True)
        ff = ff + gk * gbuf[slot, k]
    x2 = _layer_norm(alpha * x1 + ff, l2g_ref[...], l2b_ref[...])
    eg = _sigmoid(jnp.dot(x2.astype(BF16), wpg_ref[...], preferred_element_type=F32))
    ep = jnp.dot(p_ref[...].astype(BF16), wpp_ref[...], preferred_element_type=F32)
    o_ref[...] = _layer_norm(alpha * x2 + eg * ep, l3g_ref[...], l3b_ref[...])


def _combine(ys, dest_flat, x1, gates, p, w_pg, w_pp, l2_g, l2_b, l3_g, l3_b, alpha, tm):
    n, d = x1.shape
    n_i = n // tm
    row = lambda i: (i, 0)
    fixed = lambda i: (0, 0)
    kern = functools.partial(_combine_kernel, alpha=alpha, tm=tm)
    return pl.pallas_call(
        kern,
        grid=(n_i,),
        in_specs=[pl.BlockSpec((tm * TOP_K,), lambda i: (i,), memory_space=pltpu.SMEM),
                  pl.BlockSpec((tm * TOP_K,), lambda i: (jnp.minimum(i + 1, n_i - 1),),
                               memory_space=pltpu.SMEM),
                  pl.BlockSpec(memory_space=pl.ANY),
                  pl.BlockSpec((tm, d), row),
                  pl.BlockSpec((tm, LANES), row),
                  pl.BlockSpec((tm, p.shape[1]), row),
                  pl.BlockSpec(w_pg.shape, fixed),
                  pl.BlockSpec(w_pp.shape, fixed),
                  pl.BlockSpec((1, d), fixed), pl.BlockSpec((1, d), fixed),
                  pl.BlockSpec((1, d), fixed), pl.BlockSpec((1, d), fixed)],
        out_specs=pl.BlockSpec((tm, d), row),
        out_shape=jax.ShapeDtypeStruct((n, d), F32),
        scratch_shapes=[pltpu.VMEM((2, TOP_K, tm, d), F32), pltpu.SemaphoreType.DMA((2,))],
        compiler_params=_cparams(("arbitrary",), 48),
        name="combine",
    )(dest_flat, dest_flat, ys, x1, gates, p, w_pg, w_pp, l2_g, l2_b, l3_g, l3_b)


def _routing_tables(expert_idx, rank, counts, n_items):
    n_exp = counts.shape[0]
    padded = (counts + ROW_BLOCK - 1) // ROW_BLOCK * ROW_BLOCK
    pad_end = jnp.cumsum(padded)
    pad_start = pad_end - padded
    dest = (pad_start[expert_idx] + rank).reshape(-1).astype(jnp.int32)
    items_per = (padded + ITEM_ROWS - 1) // ITEM_ROWS
    item_end = jnp.cumsum(items_per)
    ids = jnp.arange(n_items, dtype=jnp.int32)
    total = item_end[-1]
    last_valid = jnp.maximum(total - 1, 0)
    eff = jnp.minimum(ids, last_valid)
    e_of = jnp.minimum(jnp.searchsorted(item_end, eff, side='right'), n_exp - 1).astype(jnp.int32)
    sub = eff - (item_end[e_of] - items_per[e_of])
    start = pad_start[e_of] + sub * ITEM_ROWS
    rows = jnp.clip(padded[e_of] - sub * ITEM_ROWS, 0, ITEM_ROWS)
    rows = jnp.where(ids < total, rows, 0)
    start = jnp.concatenate([start, pad_end[-1:]])
    return (dest, (pad_start + counts).astype(jnp.int32), pad_end.astype(jnp.int32),
            e_of, start.astype(jnp.int32), rows.astype(jnp.int32))


def kernel(x_prompt, x_sample, state_conv, state_rec, p_prompt, p_sample, ln_in_g, ln_in_b, w_in, conv_w,
           lb_theta, rms_g, w_out, ln1_g, ln1_b, w_router, b_router, w_gate, b_gate, w_up, b_up, w_down,
           b_down, ln2_g, ln2_b, w_ple_gate, w_ple_proj, ln3_g, ln3_b):
    bp, sp, d = x_prompt.shape
    bs, ss, _ = x_sample.shape
    depth = w_in.shape[0]
    assert depth == 1 and ss == 1 and sp % CHUNK == 0
    d_conv = state_conv.shape[-1]
    n_exp = w_router.shape[-1]
    n_p = bp * sp
    n = n_p + bs
    alpha = (2 * depth) ** 0.25
    row2 = lambda a: a.reshape(1, -1)

    lb = jnp.cumsum(jax.nn.softmax(lb_theta.astype(F32), axis=0), axis=0)[0]
    x_p = x_prompt.reshape(n_p, d)
    x_s = x_sample.reshape(bs, d)
    p_all = jnp.concatenate([p_prompt[0].reshape(n_p, -1), p_sample[0].reshape(bs, -1)], axis=0)

    xn_p = _ln_in(x_p, row2(ln_in_g), row2(ln_in_b), tm=512)
    proj_p = _in_proj(xn_p, w_in[0].astype(BF16), tm=1024, tn=1024)
    proj_s = _in_proj_sample(x_s, row2(ln_in_g), row2(ln_in_b), w_in[0], tn=1024)

    mix_p, conv_tail, rec_p = _mix_prompt(proj_p, conv_w[0], row2(lb), row2(rms_g[0]), n_p, bp, sp, d_conv)
    mix_s, conv_s, rec_s = _mix_sample(proj_s, state_conv[0].reshape(bs, -1), state_rec[0], conv_w[0],
                                       row2(lb), row2(rms_g[0]), d_conv)

    w_r = jnp.zeros((d, LANES), F32).at[:, :n_exp].set(w_router[0])
    b_r = jnp.full((1, LANES), NEG_BIG, F32).at[0, :n_exp].set(b_router[0])
    w_r_hi = w_r.astype(BF16)
    w_r2 = jnp.concatenate([w_r_hi, (w_r - w_r_hi.astype(F32)).astype(BF16)], axis=1)
    x1, x1p, route, gates, counts = _post_mix(
        mix_p, mix_s, x_p, x_s, w_out[0].astype(BF16), w_out[0], row2(ln_in_g), row2(ln_in_b),
        row2(ln1_g[0]), row2(ln1_b[0]), w_r, w_r2, b_r, alpha, tm=128)

    n_slots = n * TOP_K + n_exp * ROW_BLOCK
    n_items = n_exp + n_slots // ITEM_ROWS
    dest, pad_lo, pad_hi, item_e, item_start, item_rows = _routing_tables(
        route[:, 0:TOP_K], route[:, TOP_K:2 * TOP_K], counts[0, :n_exp], n_items)

    xs = _dispatch(x1p, dest, pad_lo, pad_hi, n_slots, tm=128)
    ys = _experts(xs, item_e, item_start, item_rows, w_gate[0], w_up[0], w_down[0],
                  b_gate[0], b_up[0], b_down[0])
    y_all = _combine(ys, dest, x1, gates, p_all, w_ple_gate[0].astype(BF16), w_ple_proj[0].astype(BF16),
                     row2(ln2_g[0]), row2(ln2_b[0]), row2(ln3_g[0]), row2(ln3_b[0]), alpha, tm=128)

    return (y_all[:n_p].reshape(bp, sp, d),
            y_all[n_p:].reshape(bs, ss, d),
            conv_tail[:, SUBLANES - (CONV_W - 1):, :][None],
            rec_p[None],
            conv_s.reshape(bs, CONV_W - 1, d_conv)[None],
            rec_s[None])
```

```python
import functools

import numpy as np
import jax
import jax.numpy as jnp
from jax import lax
from jax.experimental import pallas as pl
from jax.experimental.pallas import tpu as pltpu

F32 = jnp.float32
BF16 = jnp.bfloat16
HIGHEST = lax.Precision.HIGHEST

CONV_W = 3
N_HEADS = 8
HEAD_K = 128
HEAD_V = 128
TOP_K = 4
SWIGLU_LIMIT = 7.0
SWIGLU_ALPHA = 1.702
LN_EPS = 1e-5
RMS_EPS = 1e-6

LANES = 128
SUBLANES = 8
VMEM_PHYSICAL_BYTES = 64 * 1024 * 1024

CHUNK = 128
SAMPLE_BLOCK = 16
ROW_BLOCK = 128
MAX_BLOCK = 512
ITEM_ROWS = 1280
FF_TILE = 512
NEG_BIG = -1e30


def _cparams(sem, vmem_mb):
    return pltpu.CompilerParams(dimension_semantics=sem, vmem_limit_bytes=vmem_mb * 1024 * 1024)


def _layer_norm(x, g, b):
    mu = jnp.mean(x, axis=-1, keepdims=True)
    xc = x - mu
    var = jnp.mean(xc * xc, axis=-1, keepdims=True)
    return xc * lax.rsqrt(var + LN_EPS) * g + b


def _sigmoid(x):
    return 1.0 / (1.0 + jnp.exp(-x))


def _split3(x, axis):
    p1 = x.astype(BF16)
    r1 = x - p1.astype(F32)
    p2 = r1.astype(BF16)
    p3 = (r1 - p2.astype(F32)).astype(BF16)
    return jnp.concatenate([p1, p2, p3], axis=axis)


def _ln_in_kernel(x_ref, g_ref, b_ref, o_ref):
    o_ref[...] = _layer_norm(x_ref[...], g_ref[...], b_ref[...]).astype(BF16)


def _ln_in(x, g, b, tm):
    n, d = x.shape
    return pl.pallas_call(
        _ln_in_kernel,
        grid=(n // tm,),
        in_specs=[pl.BlockSpec((tm, d), lambda i: (i, 0)),
                  pl.BlockSpec((1, d), lambda i: (0, 0)),
                  pl.BlockSpec((1, d), lambda i: (0, 0))],
        out_specs=pl.BlockSpec((tm, d), lambda i: (i, 0)),
        out_shape=jax.ShapeDtypeStruct((n, d), BF16),
        compiler_params=_cparams(("parallel",), 40),
        name="ln_in",
    )(x, g, b)


def _matmul_kernel(x_ref, w_ref, o_ref):
    o_ref[...] = jnp.dot(x_ref[...], w_ref[...], preferred_element_type=F32)


def _in_proj(xn, w, tm, tn):
    n, d = xn.shape
    d_in = w.shape[1]
    return pl.pallas_call(
        _matmul_kernel,
        grid=(d_in // tn, n // tm),
        in_specs=[pl.BlockSpec((tm, d), lambda j, i: (i, 0)),
                  pl.BlockSpec((d, tn), lambda j, i: (0, j))],
        out_specs=pl.BlockSpec((tm, tn), lambda j, i: (i, j)),
        out_shape=jax.ShapeDtypeStruct((n, d_in), F32),
        compiler_params=_cparams(("parallel", "parallel"), 48),
        name="in_proj",
    )(xn, w)


def _in_proj_sample_kernel(x_ref, g_ref, b_ref, w_ref, o_ref):
    xn = _layer_norm(x_ref[...], g_ref[...], b_ref[...])
    o_ref[...] = jnp.dot(xn, w_ref[...], precision=HIGHEST, preferred_element_type=F32)


def _in_proj_sample(x, g, b, w, tn):
    n, d = x.shape
    d_in = w.shape[1]
    return pl.pallas_call(
        _in_proj_sample_kernel,
        grid=(d_in // tn,),
        in_specs=[pl.BlockSpec((n, d), lambda j: (0, 0)),
                  pl.BlockSpec((1, d), lambda j: (0, 0)),
                  pl.BlockSpec((1, d), lambda j: (0, 0)),
                  pl.BlockSpec((d, tn), lambda j: (0, j))],
        out_specs=pl.BlockSpec((n, tn), lambda j: (0, j)),
        out_shape=jax.ShapeDtypeStruct((n, d_in), F32),
        compiler_params=_cparams(("parallel",), 40),
        name="in_proj_sample",
    )(x, g, b, w)


def _forget_gates(fz, lb):
    e = jnp.exp(-jnp.abs(fz))
    r = 1.0 / (1.0 + e)
    er = e * r
    pos = fz >= 0
    sig_p = jnp.where(pos, r, er)
    sig_n = jnp.where(pos, er, r)
    oml = 1.0 - lb
    return lb + oml * sig_p, oml * sig_n


def _chunk_matrices(c):
    t = np.arange(c)[:, None]
    j = np.arange(c)[None, :]
    mats = [(j <= t), (j > t)]
    blk = c
    while blk >= 2:
        half = blk // 2
        mid = (t // blk) * blk + half
        second = (t % blk) >= half
        m_q = (j >= mid) & (j <= t)
        m_k = (j > t) & (j < mid)
        mats.append(np.where(second, m_q, m_k))
        blk = half
    return np.concatenate(mats, axis=0).astype(np.float32)


def _mix_prompt_kernel(proj_ref, convw_ref, lb_ref, rmsg_ref, cmat_ref,
                       mix_ref, convst_ref, recst_ref, s_ref, carry_ref, *, d_conv):
    c = CHUNK
    tb = pl.program_id(1)
    n_tb = pl.num_programs(1)

    @pl.when(tb == 0)
    def _():
        s_ref[...] = jnp.zeros_like(s_ref)
        carry_ref[...] = jnp.zeros_like(carry_ref)

    u = proj_ref[:, 0:d_conv] * proj_ref[:, 2 * d_conv:3 * d_conv]
    row = lax.broadcasted_iota(jnp.int32, u.shape, 0)
    prev1 = carry_ref[SUBLANES - 1:SUBLANES, :]
    prev2 = carry_ref[SUBLANES - 2:SUBLANES - 1, :]
    u1 = jnp.where(row == 0, prev1, pltpu.roll(u, 1, 0))
    u2 = jnp.where(row == 0, prev2, jnp.where(row == 1, prev1, pltpu.roll(u, 2, 0)))
    y = convw_ref[0:1, :] * u2 + convw_ref[1:2, :] * u1 + convw_ref[2:3, :] * u
    mix_ref[:, 0:d_conv] = (proj_ref[:, d_conv:2 * d_conv] * y).astype(BF16)
    carry_ref[...] = u[c - SUBLANES:c, :]
    convst_ref[0] = u[c - SUBLANES:c, :]

    o0 = 3 * d_conv
    d_rec = N_HEADS * HEAD_K
    q = proj_ref[:, o0:o0 + d_rec]
    fz = proj_ref[:, o0 + d_rec:o0 + 2 * d_rec]
    v = proj_ref[:, o0 + 2 * d_rec:o0 + 3 * d_rec]
    g = proj_ref[:, o0 + 3 * d_rec:o0 + 4 * d_rec]
    f, kk = _forget_gates(fz, lb_ref[...])
    contract0 = (((0,), (0,)), ((), ()))
    logf3 = _split3(jnp.log(f), axis=0)
    ex = jnp.dot(cmat_ref[...], logf3, preferred_element_type=F32)
    b_cum = ex[0:c]
    d_end = ex[c:2 * c]
    n_lev = cmat_ref.shape[0] // c - 2
    b_cols = lax.dot_general(logf3, jnp.ones((3 * c, HEAD_V), BF16), contract0,
                             preferred_element_type=F32)

    trow = lax.broadcasted_iota(jnp.int32, (c, c), 0)
    tcol = lax.broadcasted_iota(jnp.int32, (c, c), 1)
    prow = lax.broadcasted_iota(jnp.int32, (c, HEAD_K), 0)
    contract1 = (((1,), (1,)), ((), ()))

    for h in range(N_HEADS):
        sl = slice(h * HEAD_K, (h + 1) * HEAD_K)
        qh, kh, vh = q[:, sl], kk[:, sl], v[:, sl]
        vb = vh.astype(BF16)
        s_old = s_ref[h]
        o = jnp.dot((qh * jnp.exp(b_cum[:, sl])).astype(BF16), s_old.astype(BF16),
                    preferred_element_type=F32)
        sc = jnp.zeros((c, c), F32)
        for lev in range(n_lev):
            blk = c >> lev
            sh = blk.bit_length() - 1
            dl = jnp.exp(ex[(2 + lev) * c:(3 + lev) * c, sl])
            second = (prow & (blk - 1)) >= (blk // 2)
            qt = jnp.where(second, qh * dl, 0.0).astype(BF16)
            kt = jnp.where(second, 0.0, kh * dl).astype(BF16)
            s_l = lax.dot_general(qt, kt, contract1, preferred_element_type=F32)
            sc = sc + jnp.where((trow >> sh) == (tcol >> sh), s_l, 0.0)
        o = o + jnp.dot(sc.astype(BF16), vb, preferred_element_type=F32)
        o = o + jnp.sum(qh * kh, axis=1, keepdims=True) * vh
        khat = (kh * jnp.exp(d_end[:, sl])).astype(BF16)
        upd = lax.dot_general(khat, vb, contract0, preferred_element_type=F32)
        s_ref[h] = jnp.exp(b_cols[sl, :]) * s_old + upd
        on = o * lax.rsqrt(jnp.mean(o * o, axis=1, keepdims=True) + RMS_EPS) * rmsg_ref[:, sl]
        gh = g[:, sl]
        mix_ref[:, d_conv + h * HEAD_V:d_conv + (h + 1) * HEAD_V] = (
            on * (gh * _sigmoid(gh))).astype(BF16)

    @pl.when(tb == n_tb - 1)
    def _():
        recst_ref[0] = s_ref[...]


def _mix_prompt(proj, conv_w, lb, rms_g, n_tok, bsz, seq, d_conv):
    d_in = proj.shape[1]
    d_mix = d_conv + N_HEADS * HEAD_V
    n_tb = seq // CHUNK
    cmat = jnp.asarray(np.tile(_chunk_matrices(CHUNK), (1, 3)), dtype=BF16)
    kern = functools.partial(_mix_prompt_kernel, d_conv=d_conv)
    return pl.pallas_call(
        kern,
        grid=(bsz, n_tb),
        in_specs=[pl.BlockSpec((CHUNK, d_in), lambda b, t: (b * n_tb + t, 0)),
                  pl.BlockSpec((CONV_W, d_conv), lambda b, t: (0, 0)),
                  pl.BlockSpec((1, N_HEADS * HEAD_K), lambda b, t: (0, 0)),
                  pl.BlockSpec((1, N_HEADS * HEAD_V), lambda b, t: (0, 0)),
                  pl.BlockSpec(cmat.shape, lambda b, t: (0, 0))],
        out_specs=[pl.BlockSpec((CHUNK, d_mix), lambda b, t: (b * n_tb + t, 0)),
                   pl.BlockSpec((1, SUBLANES, d_conv), lambda b, t: (b, 0, 0)),
                   pl.BlockSpec((1, N_HEADS, HEAD_K, HEAD_V), lambda b, t: (b, 0, 0, 0))],
        out_shape=[jax.ShapeDtypeStruct((n_tok, d_mix), BF16),
                   jax.ShapeDtypeStruct((bsz, SUBLANES, d_conv), F32),
                   jax.ShapeDtypeStruct((bsz, N_HEADS, HEAD_K, HEAD_V), F32)],
        scratch_shapes=[pltpu.VMEM((N_HEADS, HEAD_K, HEAD_V), F32),
                        pltpu.VMEM((SUBLANES, d_conv), F32)],
        compiler_params=_cparams(("parallel", "arbitrary"), 40),
        name="mix_prompt",
    )(proj, conv_w, lb, rms_g, cmat)


def _mix_sample_kernel(proj_ref, cst_ref, rst_ref, convw_ref, lb_ref, rmsg_ref, sel_ref,
                       mix_ref, cnew_ref, rnew_ref, *, d_conv):
    nb = SAMPLE_BLOCK
    u = proj_ref[:, 0:d_conv] * proj_ref[:, 2 * d_conv:3 * d_conv]
    buf0 = cst_ref[:, 0:d_conv]
    buf1 = cst_ref[:, d_conv:2 * d_conv]
    y = convw_ref[0:1, :] * buf0 + convw_ref[1:2, :] * buf1 + convw_ref[2:3, :] * u
    mix_ref[:, 0:d_conv] = proj_ref[:, d_conv:2 * d_conv] * y
    cnew_ref[:, 0:d_conv] = buf1
    cnew_ref[:, d_conv:2 * d_conv] = u

    o0 = 3 * d_conv
    d_rec = N_HEADS * HEAD_K
    q = proj_ref[:, o0:o0 + d_rec]
    fz = proj_ref[:, o0 + d_rec:o0 + 2 * d_rec]
    v = proj_ref[:, o0 + 2 * d_rec:o0 + 3 * d_rec]
    g = proj_ref[:, o0 + 3 * d_rec:o0 + 4 * d_rec]
    f, kk = _forget_gates(fz, lb_ref[...])
    contract0 = (((0,), (0,)), ((), ()))
    sel = sel_ref[...]
    row = lax.broadcasted_iota(jnp.int32, (nb, HEAD_V), 0)

    def columns(a):
        return lax.dot_general(_split3(a, axis=0), sel, contract0, preferred_element_type=F32)

    for h in range(N_HEADS):
        sl = slice(h * HEAD_K, (h + 1) * HEAD_K)
        f_c, k_c, q_c = columns(f[:, sl]), columns(kk[:, sl]), columns(q[:, sl])
        o = jnp.zeros((nb, HEAD_V), F32)
        for n in range(nb):
            nl = slice(n * HEAD_V, (n + 1) * HEAD_V)
            s_new = f_c[:, nl] * rst_ref[n, h] + k_c[:, nl] * v[n:n + 1, sl]
            rnew_ref[n, h] = s_new
            o_row = jnp.sum(q_c[:, nl] * s_new, axis=0, keepdims=True)
            o = jnp.where(row == n, o_row, o)
        on = o * lax.rsqrt(jnp.mean(o * o, axis=1, keepdims=True) + RMS_EPS) * rmsg_ref[:, sl]
        gh = g[:, sl]
        mix_ref[:, d_conv + h * HEAD_V:d_conv + (h + 1) * HEAD_V] = on * (gh * _sigmoid(gh))


def _mix_sample(proj, conv_state, rec_state, conv_w, lb, rms_g, d_conv):
    n_seq = conv_state.shape[0]
    d_in = proj.shape[1]
    d_mix = d_conv + N_HEADS * HEAD_V
    nb = SAMPLE_BLOCK
    sel = jnp.asarray(np.tile(np.kron(np.eye(nb), np.ones((1, HEAD_V))), (3, 1)), dtype=BF16)
    kern = functools.partial(_mix_sample_kernel, d_conv=d_conv)
    return pl.pallas_call(
        kern,
        grid=(n_seq // nb,),
        in_specs=[pl.BlockSpec((nb, d_in), lambda i: (i, 0)),
                  pl.BlockSpec((nb, 2 * d_conv), lambda i: (i, 0)),
                  pl.BlockSpec((nb, N_HEADS, HEAD_K, HEAD_V), lambda i: (i, 0, 0, 0)),
                  pl.BlockSpec((CONV_W, d_conv), lambda i: (0, 0)),
                  pl.BlockSpec((1, N_HEADS * HEAD_K), lambda i: (0, 0)),
                  pl.BlockSpec((1, N_HEADS * HEAD_V), lambda i: (0, 0)),
                  pl.BlockSpec(sel.shape, lambda i: (0, 0))],
        out_specs=[pl.BlockSpec((nb, d_mix), lambda i: (i, 0)),
                   pl.BlockSpec((nb, 2 * d_conv), lambda i: (i, 0)),
                   pl.BlockSpec((nb, N_HEADS, HEAD_K, HEAD_V), lambda i: (i, 0, 0, 0))],
        out_shape=[jax.ShapeDtypeStruct((n_seq, d_mix), F32),
                   jax.ShapeDtypeStruct((n_seq, 2 * d_conv), F32),
                   jax.ShapeDtypeStruct(rec_state.shape, F32)],
        compiler_params=_cparams(("parallel",), 52),
        name="mix_sample",
    )(proj, conv_state, rec_state, conv_w, lb, rms_g, sel)


def _post_mix_kernel(mixp_ref, mixs_ref, xp_ref, xs_ref, woutb_ref, woutf_ref, ling_ref, linb_ref,
                     l1g_ref, l1b_ref, wr_ref, wr2_ref, br_ref,
                     x1_ref, x1p_ref, route_ref, gate_ref, cnt_ref, run_ref, h_ref, lg_ref,
                     *, alpha, n_pt):
    i = pl.program_id(0)

    @pl.when(i == 0)
    def _():
        run_ref[...] = jnp.zeros_like(run_ref)

    @pl.when(i < n_pt)
    def _():
        h_ref[...] = jnp.dot(mixp_ref[...], woutb_ref[...], preferred_element_type=F32)

    @pl.when(i >= n_pt)
    def _():
        h_ref[...] = jnp.dot(mixs_ref[...], woutf_ref[...], precision=HIGHEST,
                             preferred_element_type=F32)

    x = jnp.where(i < n_pt, xp_ref[...], xs_ref[...])
    xn = _layer_norm(x, ling_ref[...], linb_ref[...])
    x1 = _layer_norm(alpha * xn + h_ref[...], l1g_ref[...], l1b_ref[...])
    x1_ref[...] = x1
    half = x1.shape[1] // 2
    bits = pltpu.bitcast(x1.astype(BF16).astype(F32), jnp.uint32)
    x1p_ref[...] = (bits[:, half:] & jnp.uint32(0xFFFF0000)) | (bits[:, :half] >> 16)

    tm = x1.shape[0]

    @pl.when(i < n_pt)
    def _():
        xh = x1.astype(BF16)
        xl = (x1 - xh.astype(F32)).astype(BF16)
        pr = jnp.dot(jnp.concatenate([xh, xl], axis=0), wr2_ref[...], preferred_element_type=F32)
        lg_ref[...] = (pr[0:tm, 0:LANES] + pr[0:tm, LANES:2 * LANES]
                       + pr[tm:2 * tm, 0:LANES] + pr[tm:2 * tm, LANES:2 * LANES])

    @pl.when(i >= n_pt)
    def _():
        lg_ref[...] = jnp.dot(x1, wr_ref[...], precision=HIGHEST, preferred_element_type=F32)

    logits = lg_ref[...] + br_ref[...]
    lane = lax.broadcasted_iota(jnp.int32, (tm, LANES), 1)
    lane_f = lane.astype(F32)
    work = logits
    vals, idxs = [], []
    for _ in range(TOP_K):
        m = jnp.max(work, axis=1, keepdims=True)
        ix = jnp.min(jnp.where(work == m, lane_f, float(LANES)), axis=1, keepdims=True)
        vals.append(m)
        idxs.append(ix)
        work = jnp.where(lane_f == ix, NEG_BIG, work)
    ex = [jnp.exp(vv - vals[0]) for vv in vals]
    den = ex[0] + ex[1] + ex[2] + ex[3]
    onehots = [(lane_f == ix).astype(F32) for ix in idxs]
    oh = onehots[0] + onehots[1] + onehots[2] + onehots[3]
    tr = lax.broadcasted_iota(jnp.int32, (tm, tm), 0)
    tc = lax.broadcasted_iota(jnp.int32, (tm, tm), 1)
    before = jnp.dot((tc < tr).astype(BF16), oh.astype(BF16), preferred_element_type=F32)
    pos = before + run_ref[...]
    route = jnp.zeros((tm, LANES), F32)
    gates = jnp.zeros((tm, LANES), F32)
    for k in range(TOP_K):
        rank = jnp.sum(onehots[k] * pos, axis=1, keepdims=True)
        route = jnp.where(lane == k, idxs[k], route)
        route = jnp.where(lane == TOP_K + k, rank, route)
        gates = jnp.where(lane == k, ex[k] / den, gates)
    route_ref[...] = route.astype(jnp.int32)
    gate_ref[...] = gates
    run_ref[...] = run_ref[...] + jnp.sum(oh, axis=0, keepdims=True)
    cnt_ref[...] = run_ref[...].astype(jnp.int32)


def _post_mix(mix_p, mix_s, x_p, x_s, w_out_b, w_out_f, lin_g, lin_b, l1_g, l1_b, w_r, w_r2, b_r, alpha, tm):
    d = x_p.shape[1]
    n = x_p.shape[0] + x_s.shape[0]
    d_mix = mix_p.shape[1]
    n_pt = mix_p.shape[0] // tm
    assert mix_p.shape[0] % tm == 0 and mix_s.shape[0] % tm == 0
    row = lambda i: (i, 0)
    fixed = lambda i: (0, 0)
    prompt_row = lambda i: (jnp.minimum(i, n_pt - 1), 0)
    sample_row = lambda i: (jnp.maximum(i - n_pt, 0), 0)
    once = pl.Buffered(1)
    kern = functools.partial(_post_mix_kernel, alpha=alpha, n_pt=n_pt)
    return pl.pallas_call(
        kern,
        grid=(n // tm,),
        in_specs=[pl.BlockSpec((tm, d_mix), prompt_row),
                  pl.BlockSpec((tm, d_mix), sample_row),
                  pl.BlockSpec((tm, d), prompt_row),
                  pl.BlockSpec((tm, d), sample_row),
                  pl.BlockSpec(w_out_b.shape, fixed, pipeline_mode=once),
                  pl.BlockSpec(w_out_f.shape, fixed, pipeline_mode=once),
                  pl.BlockSpec((1, d), fixed), pl.BlockSpec((1, d), fixed),
                  pl.BlockSpec((1, d), fixed), pl.BlockSpec((1, d), fixed),
                  pl.BlockSpec((d, LANES), fixed), pl.BlockSpec((d, 2 * LANES), fixed),
                  pl.BlockSpec((1, LANES), fixed)],
        out_specs=[pl.BlockSpec((tm, d), row),
                   pl.BlockSpec((tm, d // 2), row),
                   pl.BlockSpec((tm, LANES), row),
                   pl.BlockSpec((tm, LANES), row),
                   pl.BlockSpec((1, LANES), fixed)],
        out_shape=[jax.ShapeDtypeStruct((n, d), F32),
                   jax.ShapeDtypeStruct((n, d // 2), jnp.uint32),
                   jax.ShapeDtypeStruct((n, LANES), jnp.int32),
                   jax.ShapeDtypeStruct((n, LANES), F32),
                   jax.ShapeDtypeStruct((1, LANES), jnp.int32)],
        scratch_shapes=[pltpu.VMEM((1, LANES), F32), pltpu.VMEM((tm, d), F32),
                        pltpu.VMEM((tm, LANES), F32)],
        compiler_params=_cparams(("arbitrary",), 48),
        name="post_mix",
    )(mix_p, mix_s, x_p, x_s, w_out_b, w_out_f, lin_g, lin_b, l1_g, l1_b, w_r, w_r2, b_r)


def _dispatch_kernel(dest_ref, padlo_ref, padhi_ref, x_ref, xs_hbm, zero_ref, sem, *, tm, n_experts):
    i = pl.program_id(0)

    def row_copy(src, dst_row):
        return pltpu.make_async_copy(src, xs_hbm.at[pl.ds(dst_row, 1), :], sem)

    @pl.when(i == 0)
    def _():
        zero_ref[...] = jnp.zeros_like(zero_ref)

        def per_expert(e, carry):
            def start(r, c):
                row_copy(zero_ref.at[pl.ds(0, 1), :], r).start()
                return c

            def wait(r, c):
                row_copy(zero_ref.at[pl.ds(0, 1), :], r).wait()
                return c

            lax.fori_loop(padlo_ref[e], padhi_ref[e], start, 0)
            lax.fori_loop(padlo_ref[e], padhi_ref[e], wait, 0)
            return carry

        lax.fori_loop(0, n_experts, per_expert, 0)

        tail0 = padhi_ref[n_experts - 1]
        n_tail = (xs_hbm.shape[0] - tail0) // ROW_BLOCK

        def tail_copy(c):
            r0 = pl.multiple_of(tail0 + c * ROW_BLOCK, ROW_BLOCK)
            return pltpu.make_async_copy(zero_ref, xs_hbm.at[pl.ds(r0, ROW_BLOCK), :], sem)

        def tail_start(c, carry):
            tail_copy(c).start()
            return carry

        def tail_wait(c, carry):
            tail_copy(c).wait()
            return carry

        lax.fori_loop(0, n_tail, tail_start, 0)
        lax.fori_loop(0, n_tail, tail_wait, 0)

    def start(t, c):
        src = x_ref.at[pl.ds(t, 1), :]
        for k in range(TOP_K):
            row_copy(src, dest_ref[t * TOP_K + k]).start()
        return c

    def wait(t, c):
        src = x_ref.at[pl.ds(t, 1), :]
        for k in range(TOP_K):
            row_copy(src, dest_ref[t * TOP_K + k]).wait()
        return c

    lax.fori_loop(0, tm, start, 0, unroll=4)
    lax.fori_loop(0, tm, wait, 0, unroll=4)


def _dispatch(x1p, dest_flat, pad_lo, pad_hi, n_slots, tm):
    n, dh = x1p.shape
    n_experts = pad_lo.shape[0]
    kern = functools.partial(_dispatch_kernel, tm=tm, n_experts=n_experts)
    return pl.pallas_call(
        kern,
        grid=(n // tm,),
        in_specs=[pl.BlockSpec((tm * TOP_K,), lambda i: (i,), memory_space=pltpu.SMEM),
                  pl.BlockSpec(memory_space=pltpu.SMEM),
                  pl.BlockSpec(memory_space=pltpu.SMEM),
                  pl.BlockSpec((tm, dh), lambda i: (i, 0))],
        out_specs=pl.BlockSpec(memory_space=pl.ANY),
        out_shape=jax.ShapeDtypeStruct((n_slots, dh), jnp.uint32),
        scratch_shapes=[pltpu.VMEM((ROW_BLOCK, dh), jnp.uint32), pltpu.SemaphoreType.DMA(())],
        compiler_params=_cparams(("arbitrary",), 32),
        name="dispatch",
    )(dest_flat, pad_lo, pad_hi, x1p)


def _expert_kernel(ie_ref, is_ref, ir_ref, xs_hbm, wg_ref, wu_ref, wd_ref, bg_ref, bu_ref, bd_ref,
                   ys_hbm, xbuf, ybuf, wgu_bf, wd_bf, pend_ref, sem_in, sem_out):
    del ie_ref
    i = pl.program_id(0)
    j = pl.program_id(1)
    n_j = pl.num_programs(1)
    rows = ir_ref[i]
    start = is_ref[i]
    tf = wg_ref.shape[1]

    def in_copy(r0, size):
        g0 = pl.multiple_of(start + r0, ROW_BLOCK)
        return pltpu.make_async_copy(xs_hbm.at[pl.ds(g0, size), :], xbuf.at[pl.ds(r0, size), :], sem_in)

    def out_copy(r0, size):
        g0 = pl.multiple_of(start + r0, ROW_BLOCK)
        return pltpu.make_async_copy(ybuf.at[pl.ds(r0, size), :], ys_hbm.at[pl.ds(g0, size), :], sem_out)

    def for_blocks(fn, n_rows=rows):
        n_big = n_rows // MAX_BLOCK

        def body(c, carry):
            fn(pl.multiple_of(c * MAX_BLOCK, MAX_BLOCK), MAX_BLOCK)
            return carry
        lax.fori_loop(0, n_big, body, 0)
        base = n_big * MAX_BLOCK
        size = MAX_BLOCK // 2
        while size >= ROW_BLOCK:
            has = (n_rows & size) != 0

            @pl.when(has)
            def _(base=base, size=size):
                fn(pl.multiple_of(base, ROW_BLOCK), size)
            base = base + jnp.where(has, size, 0)
            size //= 2

    @pl.when(jnp.logical_and(i == 0, j == 0))
    def _():
        pend_ref[0] = 0

    def wait_pending():
        pending = pend_ref[0]

        @pl.when(pending > 0)
        def _():
            for_blocks(lambda r0, size: out_copy(r0, size).wait(), pending)
            pend_ref[0] = 0

    @pl.when(rows > 0)
    def _():
        @pl.when(j == 0)
        def _():
            for_blocks(lambda r0, size: in_copy(r0, size).start())

        wgu_bf[:, 0:tf] = wg_ref[...].astype(BF16)
        wgu_bf[:, tf:2 * tf] = wu_ref[...].astype(BF16)
        wd_bf[...] = wd_ref[...].astype(BF16)

        @pl.when(j == 0)
        def _():
            for_blocks(lambda r0, size: in_copy(r0, size).wait())

        def block(r0, size, first, last):
            xu = xbuf[pl.ds(r0, size), :]
            lo = pltpu.bitcast(xu << 16, F32).astype(BF16)
            hi = pltpu.bitcast(xu & jnp.uint32(0xFFFF0000), F32).astype(BF16)
            x = jnp.concatenate([lo, hi], axis=1)
            gu = jnp.dot(x, wgu_bf[...], preferred_element_type=F32)
            gg = jnp.minimum(gu[:, 0:tf] + bg_ref[...], SWIGLU_LIMIT)
            uu = jnp.clip(gu[:, tf:2 * tf] + bu_ref[...], -SWIGLU_LIMIT, SWIGLU_LIMIT)
            hid = gg * _sigmoid(SWIGLU_ALPHA * gg) * (uu + 1.0)
            y = jnp.dot(hid.astype(BF16), wd_bf[...], preferred_element_type=F32)
            if not first:
                y = y + ybuf[pl.ds(r0, size), :]
            if last:
                y = y + bd_ref[...]
            if first:
                wait_pending()
            ybuf[pl.ds(r0, size), :] = y
            if last:
                out_copy(r0, size).start()

        @pl.when(j == 0)
        def _():
            for_blocks(lambda r0, size: block(r0, size, True, False))

        @pl.when(jnp.logical_and(j > 0, j < n_j - 1))
        def _():
            for_blocks(lambda r0, size: block(r0, size, False, False))

        @pl.when(j == n_j - 1)
        def _():
            for_blocks(lambda r0, size: block(r0, size, False, True))
            pend_ref[0] = rows

    @pl.when(jnp.logical_and(i == pl.num_programs(0) - 1, j == n_j - 1))
    def _():
        wait_pending()
        tail0 = is_ref[pl.num_programs(0)]
        n_tail = (ys_hbm.shape[0] - tail0) // ROW_BLOCK
        ybuf[0:ROW_BLOCK, :] = jnp.zeros((ROW_BLOCK, ybuf.shape[1]), F32)

        def tail_copy(c):
            g0 = pl.multiple_of(tail0 + c * ROW_BLOCK, ROW_BLOCK)
            return pltpu.make_async_copy(ybuf.at[pl.ds(0, ROW_BLOCK), :],
                                         ys_hbm.at[pl.ds(g0, ROW_BLOCK), :], sem_out)

        def tail_start(c, carry):
            tail_copy(c).start()
            return carry

        def tail_wait(c, carry):
            tail_copy(c).wait()
            return carry

        lax.fori_loop(0, n_tail, tail_start, 0)
        lax.fori_loop(0, n_tail, tail_wait, 0)


def _experts(xs, item_e, item_start, item_rows, w_gate, w_up, w_down, b_gate, b_up, b_down):
    n_slots, dh = xs.shape
    n_exp, d, d_ff = w_gate.shape
    n_items = item_e.shape[0]
    n_j = d_ff // FF_TILE
    assert n_j >= 2 and d == 2 * dh

    def jj(i, j, ir):
        return jnp.where(ir[i] > 0, j, n_j - 1)

    grid_spec = pltpu.PrefetchScalarGridSpec(
        num_scalar_prefetch=3,
        grid=(n_items, n_j),
        in_specs=[pl.BlockSpec(memory_space=pl.ANY),
                  pl.BlockSpec((None, d, FF_TILE), lambda i, j, ie, is_, ir: (ie[i], 0, jj(i, j, ir))),
                  pl.BlockSpec((None, d, FF_TILE), lambda i, j, ie, is_, ir: (ie[i], 0, jj(i, j, ir))),
                  pl.BlockSpec((None, FF_TILE, d), lambda i, j, ie, is_, ir: (ie[i], jj(i, j, ir), 0)),
                  pl.BlockSpec((None, 1, FF_TILE), lambda i, j, ie, is_, ir: (ie[i], 0, jj(i, j, ir))),
                  pl.BlockSpec((None, 1, FF_TILE), lambda i, j, ie, is_, ir: (ie[i], 0, jj(i, j, ir))),
                  pl.BlockSpec((None, 1, d), lambda i, j, ie, is_, ir: (ie[i], 0, 0))],
        out_specs=pl.BlockSpec(memory_space=pl.ANY),
        scratch_shapes=[pltpu.VMEM((ITEM_ROWS, dh), jnp.uint32),
                        pltpu.VMEM((ITEM_ROWS, d), F32),
                        pltpu.VMEM((d, 2 * FF_TILE), BF16),
                        pltpu.VMEM((FF_TILE, d), BF16),
                        pltpu.SMEM((1,), jnp.int32),
                        pltpu.SemaphoreType.DMA(()),
                        pltpu.SemaphoreType.DMA(())],
    )
    return pl.pallas_call(
        _expert_kernel,
        grid_spec=grid_spec,
        out_shape=jax.ShapeDtypeStruct((n_slots, d), F32),
        compiler_params=_cparams(("arbitrary", "arbitrary"), 58),
        name="experts",
    )(item_e, item_start, item_rows, xs, w_gate, w_up, w_down,
      b_gate.reshape(n_exp, 1, d_ff), b_up.reshape(n_exp, 1, d_ff), b_down.reshape(n_exp, 1, d))


def _combine_kernel(dcur_ref, dnext_ref, ys_hbm, x1_ref, gate_ref, p_ref, wpg_ref, wpp_ref,
                    l2g_ref, l2b_ref, l3g_ref, l3b_ref, op_ref, os_ref, gbuf, sems, *, alpha, tm, n_pt):
    i = pl.program_id(0)
    n_i = pl.num_programs(0)
    slot = lax.rem(i, 2)

    def row_copy(dref, t, k, s):
        return pltpu.make_async_copy(ys_hbm.at[pl.ds(dref[t * TOP_K + k], 1), :],
                                     gbuf.at[s, k, pl.ds(t, 1), :], sems.at[s])

    def gather(dref, s, wait):
        def body(t, c):
            for k in range(TOP_K):
                cp = row_copy(dref, t, k, s)
                if wait:
                    cp.wait()
                else:
                    cp.start()
            return c
        lax.fori_loop(0, tm, body, 0, unroll=4)

    @pl.when(i == 0)
    def _():
        gather(dcur_ref, 0, False)

    @pl.when(i + 1 < n_i)
    def _():
        gather(dnext_ref, 1 - slot, False)

    gather(dcur_ref, slot, True)

    x1 = x1_ref[...]
    lane = lax.broadcasted_iota(jnp.int32, gate_ref.shape, 1)
    gates = gate_ref[...]
    ff = jnp.zeros_like(x1)
    for k in range(TOP_K):
        gk = jnp.sum(jnp.where(lane == k, gates, 0.0), axis=1, keepdims=True)
        ff = ff + gk * gbuf[slot, k]
    x2 = _layer_norm(alpha * x1 + ff, l2g_ref[...], l2b_ref[...])
    eg = _sigmoid(jnp.dot(x2.astype(BF16), wpg_ref[...], preferred_element_type=F32))
    ep = jnp.dot(p_ref[...].astype(BF16), wpp_ref[...], preferred_element_type=F32)
    out = _layer_norm(alpha * x2 + eg * ep, l3g_ref[...], l3b_ref[...])

    @pl.when(i < n_pt)
    def _():
        op_ref[...] = out

    @pl.when(i >= n_pt)
    def _():
        os_ref[...] = out


def _combine(ys, dest_flat, x1, gates, p, w_pg, w_pp, l2_g, l2_b, l3_g, l3_b, alpha, tm, n_prompt):
    n, d = x1.shape
    n_i = n // tm
    n_pt = n_prompt // tm
    assert n_prompt % tm == 0 and n % tm == 0
    row = lambda i: (i, 0)
    fixed = lambda i: (0, 0)
    kern = functools.partial(_combine_kernel, alpha=alpha, tm=tm, n_pt=n_pt)
    return pl.pallas_call(
        kern,
        grid=(n_i,),
        in_specs=[pl.BlockSpec((tm * TOP_K,), lambda i: (i,), memory_space=pltpu.SMEM),
                  pl.BlockSpec((tm * TOP_K,), lambda i: (jnp.minimum(i + 1, n_i - 1),),
                               memory_space=pltpu.SMEM),
                  pl.BlockSpec(memory_space=pl.ANY),
                  pl.BlockSpec((tm, d), row),
                  pl.BlockSpec((tm, LANES), row),
                  pl.BlockSpec((tm, p.shape[1]), row),
                  pl.BlockSpec(w_pg.shape, fixed),
                  pl.BlockSpec(w_pp.shape, fixed),
                  pl.BlockSpec((1, d), fixed), pl.BlockSpec((1, d), fixed),
                  pl.BlockSpec((1, d), fixed), pl.BlockSpec((1, d), fixed)],
        out_specs=[pl.BlockSpec((tm, d), lambda i: (jnp.minimum(i, n_pt - 1), 0)),
                   pl.BlockSpec((tm, d), lambda i: (jnp.maximum(i - n_pt, 0), 0))],
        out_shape=[jax.ShapeDtypeStruct((n_prompt, d), F32),
                   jax.ShapeDtypeStruct((n - n_prompt, d), F32)],
        scratch_shapes=[pltpu.VMEM((2, TOP_K, tm, d), F32), pltpu.SemaphoreType.DMA((2,))],
        compiler_params=_cparams(("arbitrary",), 48),
        name="combine",
    )(dest_flat, dest_flat, ys, x1, gates, p, w_pg, w_pp, l2_g, l2_b, l3_g, l3_b)


def _routing_tables(expert_idx, rank, counts, n_items):
    n_exp = counts.shape[0]
    padded = (counts + ROW_BLOCK - 1) // ROW_BLOCK * ROW_BLOCK
    pad_end = jnp.cumsum(padded)
    pad_start = pad_end - padded
    dest = (pad_start[expert_idx] + rank).reshape(-1).astype(jnp.int32)
    items_per = (padded + ITEM_ROWS - 1) // ITEM_ROWS
    item_end = jnp.cumsum(items_per)
    ids = jnp.arange(n_items, dtype=jnp.int32)
    total = item_end[-1]
    last_valid = jnp.maximum(total - 1, 0)
    eff = jnp.minimum(ids, last_valid)
    e_of = jnp.minimum(jnp.searchsorted(item_end, eff, side='right'), n_exp - 1).astype(jnp.int32)
    sub = eff - (item_end[e_of] - items_per[e_of])
    start = pad_start[e_of] + sub * ITEM_ROWS
    rows = jnp.clip(padded[e_of] - sub * ITEM_ROWS, 0, ITEM_ROWS)
    rows = jnp.where(ids < total, rows, 0)
    start = jnp.concatenate([start, pad_end[-1:]])
    return (dest, (pad_start + counts).astype(jnp.int32), pad_end.astype(jnp.int32),
            e_of, start.astype(jnp.int32), rows.astype(jnp.int32))


def kernel(x_prompt, x_sample, state_conv, state_rec, p_prompt, p_sample, ln_in_g, ln_in_b, w_in, conv_w,
           lb_theta, rms_g, w_out, ln1_g, ln1_b, w_router, b_router, w_gate, b_gate, w_up, b_up, w_down,
           b_down, ln2_g, ln2_b, w_ple_gate, w_ple_proj, ln3_g, ln3_b):
    bp, sp, d = x_prompt.shape
    bs, ss, _ = x_sample.shape
    depth = w_in.shape[0]
    assert depth == 1 and ss == 1 and sp % CHUNK == 0
    d_conv = state_conv.shape[-1]
    n_exp = w_router.shape[-1]
    n_p = bp * sp
    n = n_p + bs
    alpha = (2 * depth) ** 0.25
    row2 = lambda a: a.reshape(1, -1)

    lb = jnp.cumsum(jax.nn.softmax(lb_theta.astype(F32), axis=0), axis=0)[0]
    x_p = x_prompt.reshape(n_p, d)
    x_s = x_sample.reshape(bs, d)
    p_all = jnp.concatenate([p_prompt[0].reshape(n_p, -1), p_sample[0].reshape(bs, -1)], axis=0)

    xn_p = _ln_in(x_p, row2(ln_in_g), row2(ln_in_b), tm=512)
    proj_p = _in_proj(xn_p, w_in[0].astype(BF16), tm=1024, tn=1024)
    proj_s = _in_proj_sample(x_s, row2(ln_in_g), row2(ln_in_b), w_in[0], tn=1024)

    mix_p, conv_tail, rec_p = _mix_prompt(proj_p, conv_w[0], row2(lb), row2(rms_g[0]), n_p, bp, sp, d_conv)
    mix_s, conv_s, rec_s = _mix_sample(proj_s, state_conv[0].reshape(bs, -1), state_rec[0], conv_w[0],
                                       row2(lb), row2(rms_g[0]), d_conv)

    w_r = jnp.zeros((d, LANES), F32).at[:, :n_exp].set(w_router[0])
    b_r = jnp.full((1, LANES), NEG_BIG, F32).at[0, :n_exp].set(b_router[0])
    w_r_hi = w_r.astype(BF16)
    w_r2 = jnp.concatenate([w_r_hi, (w_r - w_r_hi.astype(F32)).astype(BF16)], axis=1)
    x1, x1p, route, gates, counts = _post_mix(
        mix_p, mix_s, x_p, x_s, w_out[0].astype(BF16), w_out[0], row2(ln_in_g), row2(ln_in_b),
        row2(ln1_g[0]), row2(ln1_b[0]), w_r, w_r2, b_r, alpha, tm=128)

    n_slots = n * TOP_K + n_exp * ROW_BLOCK
    n_items = n_exp + n_slots // ITEM_ROWS
    dest, pad_lo, pad_hi, item_e, item_start, item_rows = _routing_tables(
        route[:, 0:TOP_K], route[:, TOP_K:2 * TOP_K], counts[0, :n_exp], n_items)

    xs = _dispatch(x1p, dest, pad_lo, pad_hi, n_slots, tm=128)
    ys = _experts(xs, item_e, item_start, item_rows, w_gate[0], w_up[0], w_down[0],
                  b_gate[0], b_up[0], b_down[0])
    y_p, y_s = _combine(ys, dest, x1, gates, p_all, w_ple_gate[0].astype(BF16), w_ple_proj[0].astype(BF16),
                        row2(ln2_g[0]), row2(ln2_b[0]), row2(ln3_g[0]), row2(ln3_b[0]), alpha, tm=128,
                        n_prompt=n_p)

    return (y_p.reshape(bp, sp, d),
            y_s.reshape(bs, ss, d),
            conv_tail[:, SUBLANES - (CONV_W - 1):, :][None],
            rec_p[None],
            conv_s.reshape(bs, CONV_W - 1, d_conv)[None],
            rec_s[None])
```

```python
import functools

import numpy as np
import jax
import jax.numpy as jnp
from jax import lax
from jax.experimental import pallas as pl
from jax.experimental.pallas import tpu as pltpu

F32 = jnp.float32
BF16 = jnp.bfloat16
HIGHEST = lax.Precision.HIGHEST

CONV_W = 3
N_HEADS = 8
HEAD_K = 128
HEAD_V = 128
TOP_K = 4
SWIGLU_LIMIT = 7.0
SWIGLU_ALPHA = 1.702
LN_EPS = 1e-5
RMS_EPS = 1e-6

LANES = 128
SUBLANES = 8
VMEM_PHYSICAL_BYTES = 64 * 1024 * 1024

CHUNK = 128
SAMPLE_BLOCK = 16
ROW_BLOCK = 128
MAX_BLOCK = 512
ITEM_ROWS = 1280
FF_TILE = 512
NEG_BIG = -1e30


def _cparams(sem, vmem_mb):
    return pltpu.CompilerParams(dimension_semantics=sem, vmem_limit_bytes=vmem_mb * 1024 * 1024)


def _layer_norm(x, g, b):
    mu = jnp.mean(x, axis=-1, keepdims=True)
    xc = x - mu
    var = jnp.mean(xc * xc, axis=-1, keepdims=True)
    return xc * lax.rsqrt(var + LN_EPS) * g + b


def _sigmoid(x):
    return 1.0 / (1.0 + jnp.exp(-x))


def _split3(x, axis):
    p1 = x.astype(BF16)
    r1 = x - p1.astype(F32)
    p2 = r1.astype(BF16)
    p3 = (r1 - p2.astype(F32)).astype(BF16)
    return jnp.concatenate([p1, p2, p3], axis=axis)


def _ln_in_kernel(x_ref, g_ref, b_ref, o_ref):
    o_ref[...] = _layer_norm(x_ref[...], g_ref[...], b_ref[...]).astype(BF16)


def _ln_in(x, g, b, tm):
    n, d = x.shape
    return pl.pallas_call(
        _ln_in_kernel,
        grid=(n // tm,),
        in_specs=[pl.BlockSpec((tm, d), lambda i: (i, 0)),
                  pl.BlockSpec((1, d), lambda i: (0, 0)),
                  pl.BlockSpec((1, d), lambda i: (0, 0))],
        out_specs=pl.BlockSpec((tm, d), lambda i: (i, 0)),
        out_shape=jax.ShapeDtypeStruct((n, d), BF16),
        compiler_params=_cparams(("parallel",), 40),
        name="ln_in",
    )(x, g, b)


def _matmul_kernel(x_ref, w_ref, o_ref):
    o_ref[...] = jnp.dot(x_ref[...], w_ref[...], preferred_element_type=F32)


def _in_proj(xn, w, tm, tn):
    n, d = xn.shape
    d_in = w.shape[1]
    return pl.pallas_call(
        _matmul_kernel,
        grid=(d_in // tn, n // tm),
        in_specs=[pl.BlockSpec((tm, d), lambda j, i: (i, 0)),
                  pl.BlockSpec((d, tn), lambda j, i: (0, j))],
        out_specs=pl.BlockSpec((tm, tn), lambda j, i: (i, j)),
        out_shape=jax.ShapeDtypeStruct((n, d_in), F32),
        compiler_params=_cparams(("parallel", "parallel"), 48),
        name="in_proj",
    )(xn, w)


def _in_proj_sample_kernel(x_ref, g_ref, b_ref, w_ref, o_ref):
    xn = _layer_norm(x_ref[...], g_ref[...], b_ref[...])
    o_ref[...] = jnp.dot(xn, w_ref[...], precision=HIGHEST, preferred_element_type=F32)


def _in_proj_sample(x, g, b, w, tn):
    n, d = x.shape
    d_in = w.shape[1]
    return pl.pallas_call(
        _in_proj_sample_kernel,
        grid=(d_in // tn,),
        in_specs=[pl.BlockSpec((n, d), lambda j: (0, 0)),
                  pl.BlockSpec((1, d), lambda j: (0, 0)),
                  pl.BlockSpec((1, d), lambda j: (0, 0)),
                  pl.BlockSpec((d, tn), lambda j: (0, j))],
        out_specs=pl.BlockSpec((n, tn), lambda j: (0, j)),
        out_shape=jax.ShapeDtypeStruct((n, d_in), F32),
        compiler_params=_cparams(("parallel",), 40),
        name="in_proj_sample",
    )(x, g, b, w)


def _forget_gates(fz, lb):
    e = jnp.exp(-jnp.abs(fz))
    r = 1.0 / (1.0 + e)
    er = e * r
    pos = fz >= 0
    sig_p = jnp.where(pos, r, er)
    sig_n = jnp.where(pos, er, r)
    oml = 1.0 - lb
    return lb + oml * sig_p, oml * sig_n


def _chunk_matrices(c):
    t = np.arange(c)[:, None]
    j = np.arange(c)[None, :]
    mats = [(j <= t), (j > t)]
    blk = c
    while blk >= 2:
        half = blk // 2
        mid = (t // blk) * blk + half
        second = (t % blk) >= half
        m_q = (j >= mid) & (j <= t)
        m_k = (j > t) & (j < mid)
        mats.append(np.where(second, m_q, m_k))
        blk = half
    return np.concatenate(mats, axis=0).astype(np.float32)


def _mix_prompt_kernel(proj_ref, convw_ref, lb_ref, rmsg_ref, cmat_ref,
                       mix_ref, convst_ref, recst_ref, s_ref, carry_ref, *, d_conv):
    c = CHUNK
    tb = pl.program_id(1)
    n_tb = pl.num_programs(1)

    @pl.when(tb == 0)
    def _():
        s_ref[...] = jnp.zeros_like(s_ref)
        carry_ref[...] = jnp.zeros_like(carry_ref)

    u = proj_ref[:, 0:d_conv] * proj_ref[:, 2 * d_conv:3 * d_conv]
    row = lax.broadcasted_iota(jnp.int32, u.shape, 0)
    prev1 = carry_ref[SUBLANES - 1:SUBLANES, :]
    prev2 = carry_ref[SUBLANES - 2:SUBLANES - 1, :]
    u1 = jnp.where(row == 0, prev1, pltpu.roll(u, 1, 0))
    u2 = jnp.where(row == 0, prev2, jnp.where(row == 1, prev1, pltpu.roll(u, 2, 0)))
    y = convw_ref[0:1, :] * u2 + convw_ref[1:2, :] * u1 + convw_ref[2:3, :] * u
    mix_ref[:, 0:d_conv] = (proj_ref[:, d_conv:2 * d_conv] * y).astype(BF16)
    carry_ref[...] = u[c - SUBLANES:c, :]
    convst_ref[0] = u[c - SUBLANES:c, :]

    o0 = 3 * d_conv
    d_rec = N_HEADS * HEAD_K
    q = proj_ref[:, o0:o0 + d_rec]
    fz = proj_ref[:, o0 + d_rec:o0 + 2 * d_rec]
    v = proj_ref[:, o0 + 2 * d_rec:o0 + 3 * d_rec]
    g = proj_ref[:, o0 + 3 * d_rec:o0 + 4 * d_rec]
    f, kk = _forget_gates(fz, lb_ref[...])
    contract0 = (((0,), (0,)), ((), ()))
    logf3 = _split3(jnp.log(f), axis=0)
    ex = jnp.dot(cmat_ref[...], logf3, preferred_element_type=F32)
    b_cum = ex[0:c]
    d_end = ex[c:2 * c]
    n_lev = cmat_ref.shape[0] // c - 2
    b_cols = lax.dot_general(logf3, jnp.ones((3 * c, HEAD_V), BF16), contract0,
                             preferred_element_type=F32)

    trow = lax.broadcasted_iota(jnp.int32, (c, c), 0)
    tcol = lax.broadcasted_iota(jnp.int32, (c, c), 1)
    prow = lax.broadcasted_iota(jnp.int32, (c, HEAD_K), 0)
    contract1 = (((1,), (1,)), ((), ()))

    for h in range(N_HEADS):
        sl = slice(h * HEAD_K, (h + 1) * HEAD_K)
        qh, kh, vh = q[:, sl], kk[:, sl], v[:, sl]
        vb = vh.astype(BF16)
        s_old = s_ref[h]
        o = jnp.dot((qh * jnp.exp(b_cum[:, sl])).astype(BF16), s_old.astype(BF16),
                    preferred_element_type=F32)
        sc = jnp.zeros((c, c), F32)
        for lev in range(n_lev):
            blk = c >> lev
            sh = blk.bit_length() - 1
            dl = jnp.exp(ex[(2 + lev) * c:(3 + lev) * c, sl])
            second = (prow & (blk - 1)) >= (blk // 2)
            qt = jnp.where(second, qh * dl, 0.0).astype(BF16)
            kt = jnp.where(second, 0.0, kh * dl).astype(BF16)
            s_l = lax.dot_general(qt, kt, contract1, preferred_element_type=F32)
            sc = sc + jnp.where((trow >> sh) == (tcol >> sh), s_l, 0.0)
        o = o + jnp.dot(sc.astype(BF16), vb, preferred_element_type=F32)
        o = o + jnp.sum(qh * kh, axis=1, keepdims=True) * vh
        khat = (kh * jnp.exp(d_end[:, sl])).astype(BF16)
        upd = lax.dot_general(khat, vb, contract0, preferred_element_type=F32)
        s_ref[h] = jnp.exp(b_cols[sl, :]) * s_old + upd
        on = o * lax.rsqrt(jnp.mean(o * o, axis=1, keepdims=True) + RMS_EPS) * rmsg_ref[:, sl]
        gh = g[:, sl]
        mix_ref[:, d_conv + h * HEAD_V:d_conv + (h + 1) * HEAD_V] = (
            on * (gh * _sigmoid(gh))).astype(BF16)

    @pl.when(tb == n_tb - 1)
    def _():
        recst_ref[0] = s_ref[...]


def _mix_prompt(proj, conv_w, lb, rms_g, n_tok, bsz, seq, d_conv):
    d_in = proj.shape[1]
    d_mix = d_conv + N_HEADS * HEAD_V
    n_tb = seq // CHUNK
    cmat = jnp.asarray(np.tile(_chunk_matrices(CHUNK), (1, 3)), dtype=BF16)
    kern = functools.partial(_mix_prompt_kernel, d_conv=d_conv)
    return pl.pallas_call(
        kern,
        grid=(bsz, n_tb),
        in_specs=[pl.BlockSpec((CHUNK, d_in), lambda b, t: (b * n_tb + t, 0)),
                  pl.BlockSpec((CONV_W, d_conv), lambda b, t: (0, 0)),
                  pl.BlockSpec((1, N_HEADS * HEAD_K), lambda b, t: (0, 0)),
                  pl.BlockSpec((1, N_HEADS * HEAD_V), lambda b, t: (0, 0)),
                  pl.BlockSpec(cmat.shape, lambda b, t: (0, 0))],
        out_specs=[pl.BlockSpec((CHUNK, d_mix), lambda b, t: (b * n_tb + t, 0)),
                   pl.BlockSpec((1, SUBLANES, d_conv), lambda b, t: (b, 0, 0)),
                   pl.BlockSpec((1, N_HEADS, HEAD_K, HEAD_V), lambda b, t: (b, 0, 0, 0))],
        out_shape=[jax.ShapeDtypeStruct((n_tok, d_mix), BF16),
                   jax.ShapeDtypeStruct((bsz, SUBLANES, d_conv), F32),
                   jax.ShapeDtypeStruct((bsz, N_HEADS, HEAD_K, HEAD_V), F32)],
        scratch_shapes=[pltpu.VMEM((N_HEADS, HEAD_K, HEAD_V), F32),
                        pltpu.VMEM((SUBLANES, d_conv), F32)],
        compiler_params=_cparams(("parallel", "arbitrary"), 40),
        name="mix_prompt",
    )(proj, conv_w, lb, rms_g, cmat)


def _mix_sample_kernel(proj_ref, cst_ref, rst_ref, convw_ref, lb_ref, rmsg_ref, sel_ref,
                       mix_ref, cnew_ref, rnew_ref, *, d_conv):
    nb = SAMPLE_BLOCK
    u = proj_ref[:, 0:d_conv] * proj_ref[:, 2 * d_conv:3 * d_conv]
    buf0 = cst_ref[:, 0:d_conv]
    buf1 = cst_ref[:, d_conv:2 * d_conv]
    y = convw_ref[0:1, :] * buf0 + convw_ref[1:2, :] * buf1 + convw_ref[2:3, :] * u
    mix_ref[:, 0:d_conv] = proj_ref[:, d_conv:2 * d_conv] * y
    cnew_ref[:, 0:d_conv] = buf1
    cnew_ref[:, d_conv:2 * d_conv] = u

    o0 = 3 * d_conv
    d_rec = N_HEADS * HEAD_K
    q = proj_ref[:, o0:o0 + d_rec]
    fz = proj_ref[:, o0 + d_rec:o0 + 2 * d_rec]
    v = proj_ref[:, o0 + 2 * d_rec:o0 + 3 * d_rec]
    g = proj_ref[:, o0 + 3 * d_rec:o0 + 4 * d_rec]
    f, kk = _forget_gates(fz, lb_ref[...])
    contract0 = (((0,), (0,)), ((), ()))
    sel = sel_ref[...]
    row = lax.broadcasted_iota(jnp.int32, (nb, HEAD_V), 0)

    def columns(a):
        return lax.dot_general(_split3(a, axis=0), sel, contract0, preferred_element_type=F32)

    for h in range(N_HEADS):
        sl = slice(h * HEAD_K, (h + 1) * HEAD_K)
        f_c, k_c, q_c = columns(f[:, sl]), columns(kk[:, sl]), columns(q[:, sl])
        o = jnp.zeros((nb, HEAD_V), F32)
        for n in range(nb):
            nl = slice(n * HEAD_V, (n + 1) * HEAD_V)
            s_new = f_c[:, nl] * rst_ref[n, h] + k_c[:, nl] * v[n:n + 1, sl]
            rnew_ref[n, h] = s_new
            o_row = jnp.sum(q_c[:, nl] * s_new, axis=0, keepdims=True)
            o = jnp.where(row == n, o_row, o)
        on = o * lax.rsqrt(jnp.mean(o * o, axis=1, keepdims=True) + RMS_EPS) * rmsg_ref[:, sl]
        gh = g[:, sl]
        mix_ref[:, d_conv + h * HEAD_V:d_conv + (h + 1) * HEAD_V] = on * (gh * _sigmoid(gh))


def _mix_sample(proj, conv_state, rec_state, conv_w, lb, rms_g, d_conv):
    n_seq = conv_state.shape[0]
    d_in = proj.shape[1]
    d_mix = d_conv + N_HEADS * HEAD_V
    nb = SAMPLE_BLOCK
    sel = jnp.asarray(np.tile(np.kron(np.eye(nb), np.ones((1, HEAD_V))), (3, 1)), dtype=BF16)
    kern = functools.partial(_mix_sample_kernel, d_conv=d_conv)
    return pl.pallas_call(
        kern,
        grid=(n_seq // nb,),
        in_specs=[pl.BlockSpec((nb, d_in), lambda i: (i, 0)),
                  pl.BlockSpec((nb, 2 * d_conv), lambda i: (i, 0)),
                  pl.BlockSpec((nb, N_HEADS, HEAD_K, HEAD_V), lambda i: (i, 0, 0, 0)),
                  pl.BlockSpec((CONV_W, d_conv), lambda i: (0, 0)),
                  pl.BlockSpec((1, N_HEADS * HEAD_K), lambda i: (0, 0)),
                  pl.BlockSpec((1, N_HEADS * HEAD_V), lambda i: (0, 0)),
                  pl.BlockSpec(sel.shape, lambda i: (0, 0))],
        out_specs=[pl.BlockSpec((nb, d_mix), lambda i: (i, 0)),
                   pl.BlockSpec((nb, 2 * d_conv), lambda i: (i, 0)),
                   pl.BlockSpec((nb, N_HEADS, HEAD_K, HEAD_V), lambda i: (i, 0, 0, 0))],
        out_shape=[jax.ShapeDtypeStruct((n_seq, d_mix), F32),
                   jax.ShapeDtypeStruct((n_seq, 2 * d_conv), F32),
                   jax.ShapeDtypeStruct(rec_state.shape, F32)],
        compiler_params=_cparams(("parallel",), 52),
        name="mix_sample",
    )(proj, conv_state, rec_state, conv_w, lb, rms_g, sel)


def _post_mix_kernel(mixp_ref, mixs_ref, xp_ref, xs_ref, woutb_ref, woutf_ref, ling_ref, linb_ref,
                     l1g_ref, l1b_ref, wr_ref, wr2_ref, br_ref,
                     x1_ref, x1p_ref, route_ref, gate_ref, cnt_ref, run_ref, h_ref, lg_ref,
                     *, alpha, n_pt):
    i = pl.program_id(0)

    @pl.when(i == 0)
    def _():
        run_ref[...] = jnp.zeros_like(run_ref)

    @pl.when(i < n_pt)
    def _():
        h_ref[...] = jnp.dot(mixp_ref[...], woutb_ref[...], preferred_element_type=F32)

    @pl.when(i >= n_pt)
    def _():
        h_ref[...] = jnp.dot(mixs_ref[...], woutf_ref[...], precision=HIGHEST,
                             preferred_element_type=F32)

    x = jnp.where(i < n_pt, xp_ref[...], xs_ref[...])
    xn = _layer_norm(x, ling_ref[...], linb_ref[...])
    x1 = _layer_norm(alpha * xn + h_ref[...], l1g_ref[...], l1b_ref[...])
    x1_ref[...] = x1
    half = x1.shape[1] // 2
    bits = pltpu.bitcast(x1.astype(BF16).astype(F32), jnp.uint32)
    x1p_ref[...] = (bits[:, half:] & jnp.uint32(0xFFFF0000)) | (bits[:, :half] >> 16)

    tm = x1.shape[0]

    @pl.when(i < n_pt)
    def _():
        xh = x1.astype(BF16)
        xl = (x1 - xh.astype(F32)).astype(BF16)
        pr = jnp.dot(jnp.concatenate([xh, xl], axis=0), wr2_ref[...], preferred_element_type=F32)
        lg_ref[...] = (pr[0:tm, 0:LANES] + pr[0:tm, LANES:2 * LANES]
                       + pr[tm:2 * tm, 0:LANES] + pr[tm:2 * tm, LANES:2 * LANES])

    @pl.when(i >= n_pt)
    def _():
        lg_ref[...] = jnp.dot(x1, wr_ref[...], precision=HIGHEST, preferred_element_type=F32)

    logits = lg_ref[...] + br_ref[...]
    lane = lax.broadcasted_iota(jnp.int32, (tm, LANES), 1)
    lane_f = lane.astype(F32)
    work = logits
    vals, idxs = [], []
    for _ in range(TOP_K):
        m = jnp.max(work, axis=1, keepdims=True)
        ix = jnp.min(jnp.where(work == m, lane_f, float(LANES)), axis=1, keepdims=True)
        vals.append(m)
        idxs.append(ix)
        work = jnp.where(lane_f == ix, NEG_BIG, work)
    ex = [jnp.exp(vv - vals[0]) for vv in vals]
    den = ex[0] + ex[1] + ex[2] + ex[3]
    onehots = [(lane_f == ix).astype(F32) for ix in idxs]
    oh = onehots[0] + onehots[1] + onehots[2] + onehots[3]
    tr = lax.broadcasted_iota(jnp.int32, (tm, tm), 0)
    tc = lax.broadcasted_iota(jnp.int32, (tm, tm), 1)
    before = jnp.dot((tc < tr).astype(BF16), oh.astype(BF16), preferred_element_type=F32)
    pos = before + run_ref[...]
    route = jnp.zeros((tm, LANES), F32)
    gates = jnp.zeros((tm, LANES), F32)
    for k in range(TOP_K):
        rank = jnp.sum(onehots[k] * pos, axis=1, keepdims=True)
        route = jnp.where(lane == k, idxs[k], route)
        route = jnp.where(lane == TOP_K + k, rank, route)
        gates = jnp.where(lane == k, ex[k] / den, gates)
    route_ref[...] = route.astype(jnp.int32)
    gate_ref[...] = gates
    run_ref[...] = run_ref[...] + jnp.sum(oh, axis=0, keepdims=True)
    cnt_ref[...] = run_ref[...].astype(jnp.int32)


def _post_mix(mix_p, mix_s, x_p, x_s, w_out_b, w_out_f, lin_g, lin_b, l1_g, l1_b, w_r, w_r2, b_r, alpha, tm):
    d = x_p.shape[1]
    n = x_p.shape[0] + x_s.shape[0]
    d_mix = mix_p.shape[1]
    n_pt = mix_p.shape[0] // tm
    assert mix_p.shape[0] % tm == 0 and mix_s.shape[0] % tm == 0
    row = lambda i: (i, 0)
    fixed = lambda i: (0, 0)
    prompt_row = lambda i: (jnp.minimum(i, n_pt - 1), 0)
    sample_row = lambda i: (jnp.maximum(i - n_pt, 0), 0)
    once = pl.Buffered(1)
    kern = functools.partial(_post_mix_kernel, alpha=alpha, n_pt=n_pt)
    return pl.pallas_call(
        kern,
        grid=(n // tm,),
        in_specs=[pl.BlockSpec((tm, d_mix), prompt_row),
                  pl.BlockSpec((tm, d_mix), sample_row),
                  pl.BlockSpec((tm, d), prompt_row),
                  pl.BlockSpec((tm, d), sample_row),
                  pl.BlockSpec(w_out_b.shape, fixed, pipeline_mode=once),
                  pl.BlockSpec(w_out_f.shape, fixed, pipeline_mode=once),
                  pl.BlockSpec((1, d), fixed), pl.BlockSpec((1, d), fixed),
                  pl.BlockSpec((1, d), fixed), pl.BlockSpec((1, d), fixed),
                  pl.BlockSpec((d, LANES), fixed), pl.BlockSpec((d, 2 * LANES), fixed),
                  pl.BlockSpec((1, LANES), fixed)],
        out_specs=[pl.BlockSpec((tm, d), row),
                   pl.BlockSpec((tm, d // 2), row),
                   pl.BlockSpec((tm, LANES), row),
                   pl.BlockSpec((tm, LANES), row),
                   pl.BlockSpec((1, LANES), fixed)],
        out_shape=[jax.ShapeDtypeStruct((n, d), F32),
                   jax.ShapeDtypeStruct((n, d // 2), jnp.uint32),
                   jax.ShapeDtypeStruct((n, LANES), jnp.int32),
                   jax.ShapeDtypeStruct((n, LANES), F32),
                   jax.ShapeDtypeStruct((1, LANES), jnp.int32)],
        scratch_shapes=[pltpu.VMEM((1, LANES), F32), pltpu.VMEM((tm, d), F32),
                        pltpu.VMEM((tm, LANES), F32)],
        compiler_params=_cparams(("arbitrary",), 48),
        name="post_mix",
    )(mix_p, mix_s, x_p, x_s, w_out_b, w_out_f, lin_g, lin_b, l1_g, l1_b, w_r, w_r2, b_r)


def _dispatch_kernel(dest_ref, padlo_ref, padhi_ref, x_ref, xs_hbm, zero_ref, sem, *, tm, n_experts):
    i = pl.program_id(0)

    def row_copy(src, dst_row):
        return pltpu.make_async_copy(src, xs_hbm.at[pl.ds(dst_row, 1), :], sem)

    @pl.when(i == 0)
    def _():
        zero_ref[...] = jnp.zeros_like(zero_ref)

        def per_expert(e, carry):
            def start(r, c):
                row_copy(zero_ref.at[pl.ds(0, 1), :], r).start()
                return c

            def wait(r, c):
                row_copy(zero_ref.at[pl.ds(0, 1), :], r).wait()
                return c

            lax.fori_loop(padlo_ref[e], padhi_ref[e], start, 0)
            lax.fori_loop(padlo_ref[e], padhi_ref[e], wait, 0)
            return carry

        lax.fori_loop(0, n_experts, per_expert, 0)

        tail0 = padhi_ref[n_experts - 1]
        n_tail = (xs_hbm.shape[0] - tail0) // ROW_BLOCK

        def tail_copy(c):
            r0 = pl.multiple_of(tail0 + c * ROW_BLOCK, ROW_BLOCK)
            return pltpu.make_async_copy(zero_ref, xs_hbm.at[pl.ds(r0, ROW_BLOCK), :], sem)

        def tail_start(c, carry):
            tail_copy(c).start()
            return carry

        def tail_wait(c, carry):
            tail_copy(c).wait()
            return carry

        lax.fori_loop(0, n_tail, tail_start, 0)
        lax.fori_loop(0, n_tail, tail_wait, 0)

    def start(t, c):
        src = x_ref.at[pl.ds(t, 1), :]
        for k in range(TOP_K):
            row_copy(src, dest_ref[t * TOP_K + k]).start()
        return c

    def wait(t, c):
        src = x_ref.at[pl.ds(t, 1), :]
        for k in range(TOP_K):
            row_copy(src, dest_ref[t * TOP_K + k]).wait()
        return c

    lax.fori_loop(0, tm, start, 0, unroll=4)
    lax.fori_loop(0, tm, wait, 0, unroll=4)


def _dispatch(x1p, dest_flat, pad_lo, pad_hi, n_slots, tm):
    n, dh = x1p.shape
    n_experts = pad_lo.shape[0]
    kern = functools.partial(_dispatch_kernel, tm=tm, n_experts=n_experts)
    return pl.pallas_call(
        kern,
        grid=(n // tm,),
        in_specs=[pl.BlockSpec((tm * TOP_K,), lambda i: (i,), memory_space=pltpu.SMEM),
                  pl.BlockSpec(memory_space=pltpu.SMEM),
                  pl.BlockSpec(memory_space=pltpu.SMEM),
                  pl.BlockSpec((tm, dh), lambda i: (i, 0))],
        out_specs=pl.BlockSpec(memory_space=pl.ANY),
        out_shape=jax.ShapeDtypeStruct((n_slots, dh), jnp.uint32),
        scratch_shapes=[pltpu.VMEM((ROW_BLOCK, dh), jnp.uint32), pltpu.SemaphoreType.DMA(())],
        compiler_params=_cparams(("arbitrary",), 32),
        name="dispatch",
    )(dest_flat, pad_lo, pad_hi, x1p)


def _expert_kernel(ie_ref, is_ref, ir_ref, xs_hbm, wg_ref, wu_ref, wd_ref, bg_ref, bu_ref, bd_ref,
                   ys_hbm, xbuf, ybuf, wgu_bf, wd_bf, sem_in, sem_out):
    i = pl.program_id(0)
    j = pl.program_id(1)
    n_j = pl.num_programs(1)
    rows = ir_ref[i]
    start = is_ref[i]
    tf = wg_ref.shape[1]
    b_row = ie_ref[i] * n_j + j
    bg = bg_ref[pl.ds(b_row, 1), :]
    bu = bu_ref[pl.ds(b_row, 1), :]
    bd = bd_ref[pl.ds(ie_ref[i], 1), :]

    def in_copy(r0, size):
        g0 = pl.multiple_of(start + r0, ROW_BLOCK)
        return pltpu.make_async_copy(xs_hbm.at[pl.ds(g0, size), :], xbuf.at[pl.ds(r0, size), :], sem_in)

    def out_copy(r0, size):
        g0 = pl.multiple_of(start + r0, ROW_BLOCK)
        return pltpu.make_async_copy(ybuf.at[pl.ds(r0, size), :], ys_hbm.at[pl.ds(g0, size), :], sem_out)

    def for_blocks(fn):
        n_big = rows // MAX_BLOCK

        def body(c, carry):
            fn(pl.multiple_of(c * MAX_BLOCK, MAX_BLOCK), MAX_BLOCK)
            return carry
        lax.fori_loop(0, n_big, body, 0)
        base = n_big * MAX_BLOCK
        size = MAX_BLOCK // 2
        while size >= ROW_BLOCK:
            has = (rows & size) != 0

            @pl.when(has)
            def _(base=base, size=size):
                fn(pl.multiple_of(base, ROW_BLOCK), size)
            base = base + jnp.where(has, size, 0)
            size //= 2

    @pl.when(rows > 0)
    def _():
        @pl.when(j == 0)
        def _():
            for_blocks(lambda r0, size: in_copy(r0, size).start())

        wgu_bf[:, 0:tf] = wg_ref[...].astype(BF16)
        wgu_bf[:, tf:2 * tf] = wu_ref[...].astype(BF16)
        wd_bf[...] = wd_ref[...].astype(BF16)

        @pl.when(j == 0)
        def _():
            for_blocks(lambda r0, size: in_copy(r0, size).wait())

        def block(r0, size, first, last):
            xu = xbuf[pl.ds(r0, size), :]
            lo = pltpu.bitcast(xu << 16, F32).astype(BF16)
            hi = pltpu.bitcast(xu & jnp.uint32(0xFFFF0000), F32).astype(BF16)
            x = jnp.concatenate([lo, hi], axis=1)
            gu = jnp.dot(x, wgu_bf[...], preferred_element_type=F32)
            gg = jnp.minimum(gu[:, 0:tf] + bg, SWIGLU_LIMIT)
            uu = jnp.clip(gu[:, tf:2 * tf] + bu, -SWIGLU_LIMIT, SWIGLU_LIMIT)
            hid = gg * _sigmoid(SWIGLU_ALPHA * gg) * (uu + 1.0)
            y = jnp.dot(hid.astype(BF16), wd_bf[...], preferred_element_type=F32)
            if not first:
                y = y + ybuf[pl.ds(r0, size), :]
            if last:
                y = y + bd
            ybuf[pl.ds(r0, size), :] = y
            if last:
                out_copy(r0, size).start()

        @pl.when(j == 0)
        def _():
            for_blocks(lambda r0, size: block(r0, size, True, False))

        @pl.when(jnp.logical_and(j > 0, j < n_j - 1))
        def _():
            for_blocks(lambda r0, size: block(r0, size, False, False))

        @pl.when(j == n_j - 1)
        def _():
            for_blocks(lambda r0, size: block(r0, size, False, True))
            for_blocks(lambda r0, size: out_copy(r0, size).wait())

    @pl.when(jnp.logical_and(i == pl.num_programs(0) - 1, j == n_j - 1))
    def _():
        tail0 = is_ref[pl.num_programs(0)]
        n_tail = (ys_hbm.shape[0] - tail0) // ROW_BLOCK
        ybuf[0:ROW_BLOCK, :] = jnp.zeros((ROW_BLOCK, ybuf.shape[1]), F32)

        def tail_copy(c):
            g0 = pl.multiple_of(tail0 + c * ROW_BLOCK, ROW_BLOCK)
            return pltpu.make_async_copy(ybuf.at[pl.ds(0, ROW_BLOCK), :],
                                         ys_hbm.at[pl.ds(g0, ROW_BLOCK), :], sem_out)

        def tail_start(c, carry):
            tail_copy(c).start()
            return carry

        def tail_wait(c, carry):
            tail_copy(c).wait()
            return carry

        lax.fori_loop(0, n_tail, tail_start, 0)
        lax.fori_loop(0, n_tail, tail_wait, 0)


def _experts(xs, item_e, item_start, item_rows, w_gate, w_up, w_down, b_gate, b_up, b_down):
    n_slots, dh = xs.shape
    n_exp, d, d_ff = w_gate.shape
    n_items = item_e.shape[0]
    n_j = d_ff // FF_TILE
    assert n_j >= 2 and d == 2 * dh

    def jj(i, j, ir):
        return jnp.where(ir[i] > 0, j, n_j - 1)

    grid_spec = pltpu.PrefetchScalarGridSpec(
        num_scalar_prefetch=3,
        grid=(n_items, n_j),
        in_specs=[pl.BlockSpec(memory_space=pl.ANY),
                  pl.BlockSpec((None, d, FF_TILE), lambda i, j, ie, is_, ir: (ie[i], 0, jj(i, j, ir))),
                  pl.BlockSpec((None, d, FF_TILE), lambda i, j, ie, is_, ir: (ie[i], 0, jj(i, j, ir))),
                  pl.BlockSpec((None, FF_TILE, d), lambda i, j, ie, is_, ir: (ie[i], jj(i, j, ir), 0)),
                  pl.BlockSpec((n_exp * n_j, FF_TILE), lambda i, j, ie, is_, ir: (0, 0)),
                  pl.BlockSpec((n_exp * n_j, FF_TILE), lambda i, j, ie, is_, ir: (0, 0)),
                  pl.BlockSpec((n_exp, d), lambda i, j, ie, is_, ir: (0, 0))],
        out_specs=pl.BlockSpec(memory_space=pl.ANY),
        scratch_shapes=[pltpu.VMEM((ITEM_ROWS, dh), jnp.uint32),
                        pltpu.VMEM((ITEM_ROWS, d), F32),
                        pltpu.VMEM((d, 2 * FF_TILE), BF16),
                        pltpu.VMEM((FF_TILE, d), BF16),
                        pltpu.SemaphoreType.DMA(()),
                        pltpu.SemaphoreType.DMA(())],
    )
    return pl.pallas_call(
        _expert_kernel,
        grid_spec=grid_spec,
        out_shape=jax.ShapeDtypeStruct((n_slots, d), F32),
        compiler_params=_cparams(("arbitrary", "arbitrary"), 58),
        name="experts",
    )(item_e, item_start, item_rows, xs, w_gate, w_up, w_down,
      b_gate.reshape(n_exp * n_j, FF_TILE), b_up.reshape(n_exp * n_j, FF_TILE), b_down)


def _combine_kernel(dcur_ref, dnext_ref, ys_hbm, x1_ref, gate_ref, p_ref, wpg_ref, wpp_ref,
                    l2g_ref, l2b_ref, l3g_ref, l3b_ref, op_ref, os_ref, gbuf_a, gbuf_b, sems,
                    *, alpha, tm, n_pt):
    i = pl.program_id(0)
    n_i = pl.num_programs(0)

    def row_copy(dref, t, k, buf, sem):
        return pltpu.make_async_copy(ys_hbm.at[pl.ds(dref[t * TOP_K + k], 1), :],
                                     buf.at[k, pl.ds(t, 1), :], sem)

    def gather_loop(dref, buf, sem, wait):
        def body(t, c):
            for k in range(TOP_K):
                cp = row_copy(dref, t, k, buf, sem)
                if wait:
                    cp.wait()
                else:
                    cp.start()
            return c
        lax.fori_loop(0, tm, body, 0, unroll=4)

    def step(cur, cur_sem, nxt, nxt_sem):
        @pl.when(i == 0)
        def _():
            gather_loop(dcur_ref, cur, cur_sem, False)

        gather_loop(dcur_ref, cur, cur_sem, True)

        for t in range(tm):
            for k in range(TOP_K):
                row_copy(dnext_ref, t, k, nxt, nxt_sem).start()

        x1 = x1_ref[...]
        lane = lax.broadcasted_iota(jnp.int32, gate_ref.shape, 1)
        gates = gate_ref[...]
        ff = jnp.zeros_like(x1)
        for k in range(TOP_K):
            gk = jnp.sum(jnp.where(lane == k, gates, 0.0), axis=1, keepdims=True)
            ff = ff + gk * cur[k]
        x2 = _layer_norm(alpha * x1 + ff, l2g_ref[...], l2b_ref[...])
        eg = _sigmoid(jnp.dot(x2.astype(BF16), wpg_ref[...], preferred_element_type=F32))
        ep = jnp.dot(p_ref[...].astype(BF16), wpp_ref[...], preferred_element_type=F32)
        out = _layer_norm(alpha * x2 + eg * ep, l3g_ref[...], l3b_ref[...])

        @pl.when(i < n_pt)
        def _():
            op_ref[...] = out

        @pl.when(i >= n_pt)
        def _():
            os_ref[...] = out

        @pl.when(i == n_i - 1)
        def _():
            gather_loop(dnext_ref, nxt, nxt_sem, True)

    @pl.when(lax.rem(i, 2) == 0)
    def _():
        step(gbuf_a, sems.at[0], gbuf_b, sems.at[1])

    @pl.when(lax.rem(i, 2) == 1)
    def _():
        step(gbuf_b, sems.at[1], gbuf_a, sems.at[0])


def _combine(ys, dest_flat, x1, gates, p, w_pg, w_pp, l2_g, l2_b, l3_g, l3_b, alpha, tm, n_prompt):
    n, d = x1.shape
    n_i = n // tm
    n_pt = n_prompt // tm
    assert n_prompt % tm == 0 and n % tm == 0
    row = lambda i: (i, 0)
    fixed = lambda i: (0, 0)
    kern = functools.partial(_combine_kernel, alpha=alpha, tm=tm, n_pt=n_pt)
    return pl.pallas_call(
        kern,
        grid=(n_i,),
        in_specs=[pl.BlockSpec((tm * TOP_K,), lambda i: (i,), memory_space=pltpu.SMEM),
                  pl.BlockSpec((tm * TOP_K,), lambda i: (jnp.minimum(i + 1, n_i - 1),),
                               memory_space=pltpu.SMEM),
                  pl.BlockSpec(memory_space=pl.ANY),
                  pl.BlockSpec((tm, d), row),
                  pl.BlockSpec((tm, LANES), row),
                  pl.BlockSpec((tm, p.shape[1]), row),
                  pl.BlockSpec(w_pg.shape, fixed),
                  pl.BlockSpec(w_pp.shape, fixed),
                  pl.BlockSpec((1, d), fixed), pl.BlockSpec((1, d), fixed),
                  pl.BlockSpec((1, d), fixed), pl.BlockSpec((1, d), fixed)],
        out_specs=[pl.BlockSpec((tm, d), lambda i: (jnp.minimum(i, n_pt - 1), 0)),
                   pl.BlockSpec((tm, d), lambda i: (jnp.maximum(i - n_pt, 0), 0))],
        out_shape=[jax.ShapeDtypeStruct((n_prompt, d), F32),
                   jax.ShapeDtypeStruct((n - n_prompt, d), F32)],
        scratch_shapes=[pltpu.VMEM((TOP_K, tm, d), F32), pltpu.VMEM((TOP_K, tm, d), F32),
                        pltpu.SemaphoreType.DMA((2,))],
        compiler_params=_cparams(("arbitrary",), 48),
        name="combine",
    )(dest_flat, dest_flat, ys, x1, gates, p, w_pg, w_pp, l2_g, l2_b, l3_g, l3_b)


def _routing_tables(expert_idx, rank, counts, n_items):
    n_exp = counts.shape[0]
    padded = (counts + ROW_BLOCK - 1) // ROW_BLOCK * ROW_BLOCK
    pad_end = jnp.cumsum(padded)
    pad_start = pad_end - padded
    dest = (pad_start[expert_idx] + rank).reshape(-1).astype(jnp.int32)
    items_per = (padded + ITEM_ROWS - 1) // ITEM_ROWS
    item_end = jnp.cumsum(items_per)
    ids = jnp.arange(n_items, dtype=jnp.int32)
    total = item_end[-1]
    last_valid = jnp.maximum(total - 1, 0)
    eff = jnp.minimum(ids, last_valid)
    e_of = jnp.minimum(jnp.searchsorted(item_end, eff, side='right'), n_exp - 1).astype(jnp.int32)
    sub = eff - (item_end[e_of] - items_per[e_of])
    start = pad_start[e_of] + sub * ITEM_ROWS
    rows = jnp.clip(padded[e_of] - sub * ITEM_ROWS, 0, ITEM_ROWS)
    rows = jnp.where(ids < total, rows, 0)
    start = jnp.concatenate([start, pad_end[-1:]])
    return (dest, (pad_start + counts).astype(jnp.int32), pad_end.astype(jnp.int32),
            e_of, start.astype(jnp.int32), rows.astype(jnp.int32))


def kernel(x_prompt, x_sample, state_conv, state_rec, p_prompt, p_sample, ln_in_g, ln_in_b, w_in, conv_w,
           lb_theta, rms_g, w_out, ln1_g, ln1_b, w_router, b_router, w_gate, b_gate, w_up, b_up, w_down,
           b_down, ln2_g, ln2_b, w_ple_gate, w_ple_proj, ln3_g, ln3_b):
    bp, sp, d = x_prompt.shape
    bs, ss, _ = x_sample.shape
    depth = w_in.shape[0]
    assert depth == 1 and ss == 1 and sp % CHUNK == 0
    d_conv = state_conv.shape[-1]
    n_exp = w_router.shape[-1]
    n_p = bp * sp
    n = n_p + bs
    alpha = (2 * depth) ** 0.25
    row2 = lambda a: a.reshape(1, -1)

    lb = jnp.cumsum(jax.nn.softmax(lb_theta.astype(F32), axis=0), axis=0)[0]
    x_p = x_prompt.reshape(n_p, d)
    x_s = x_sample.reshape(bs, d)
    p_all = jnp.concatenate([p_prompt[0].reshape(n_p, -1), p_sample[0].reshape(bs, -1)], axis=0)

    xn_p = _ln_in(x_p, row2(ln_in_g), row2(ln_in_b), tm=512)
    proj_p = _in_proj(xn_p, w_in[0].astype(BF16), tm=1024, tn=1024)
    proj_s = _in_proj_sample(x_s, row2(ln_in_g), row2(ln_in_b), w_in[0], tn=1024)

    mix_p, conv_tail, rec_p = _mix_prompt(proj_p, conv_w[0], row2(lb), row2(rms_g[0]), n_p, bp, sp, d_conv)
    mix_s, conv_s, rec_s = _mix_sample(proj_s, state_conv[0].reshape(bs, -1), state_rec[0], conv_w[0],
                                       row2(lb), row2(rms_g[0]), d_conv)

    w_r = jnp.zeros((d, LANES), F32).at[:, :n_exp].set(w_router[0])
    b_r = jnp.full((1, LANES), NEG_BIG, F32).at[0, :n_exp].set(b_router[0])
    w_r_hi = w_r.astype(BF16)
    w_r2 = jnp.concatenate([w_r_hi, (w_r - w_r_hi.astype(F32)).astype(BF16)], axis=1)
    x1, x1p, route, gates, counts = _post_mix(
        mix_p, mix_s, x_p, x_s, w_out[0].astype(BF16), w_out[0], row2(ln_in_g), row2(ln_in_b),
        row2(ln1_g[0]), row2(ln1_b[0]), w_r, w_r2, b_r, alpha, tm=128)

    n_slots = n * TOP_K + n_exp * ROW_BLOCK
    n_items = n_exp + n_slots // ITEM_ROWS
    dest, pad_lo, pad_hi, item_e, item_start, item_rows = _routing_tables(
        route[:, 0:TOP_K], route[:, TOP_K:2 * TOP_K], counts[0, :n_exp], n_items)

    xs = _dispatch(x1p, dest, pad_lo, pad_hi, n_slots, tm=128)
    ys = _experts(xs, item_e, item_start, item_rows, w_gate[0], w_up[0], w_down[0],
                  b_gate[0], b_up[0], b_down[0])
    y_p, y_s = _combine(ys, dest, x1, gates, p_all, w_ple_gate[0].astype(BF16), w_ple_proj[0].astype(BF16),
                        row2(ln2_g[0]), row2(ln2_b[0]), row2(ln3_g[0]), row2(ln3_b[0]), alpha, tm=128,
                        n_prompt=n_p)

    return (y_p.reshape(bp, sp, d),
            y_s.reshape(bs, ss, d),
            conv_tail[:, SUBLANES - (CONV_W - 1):, :][None],
            rec_p[None],
            conv_s.reshape(bs, CONV_W - 1, d_conv)[None],
            rec_s[None])
```

```python
import functools

import numpy as np
import jax
import jax.numpy as jnp
from jax import lax
from jax.experimental import pallas as pl
from jax.experimental.pallas import tpu as pltpu

F32 = jnp.float32
BF16 = jnp.bfloat16
HIGHEST = lax.Precision.HIGHEST

CONV_W = 3
N_HEADS = 8
HEAD_K = 128
HEAD_V = 128
TOP_K = 4
SWIGLU_LIMIT = 7.0
SWIGLU_ALPHA = 1.702
LN_EPS = 1e-5
RMS_EPS = 1e-6

LANES = 128
SUBLANES = 8
VMEM_PHYSICAL_BYTES = 64 * 1024 * 1024
DMA_THREADS = 2

CHUNK = 128
SAMPLE_BLOCK = 16
ROW_BLOCK = 128
MAX_BLOCK = 512
ITEM_ROWS = 1280
FF_TILE = 512
NEG_BIG = -1e30


def _cparams(sem, vmem_mb):
    return pltpu.CompilerParams(dimension_semantics=sem, vmem_limit_bytes=vmem_mb * 1024 * 1024)


def _layer_norm(x, g, b):
    mu = jnp.mean(x, axis=-1, keepdims=True)
    xc = x - mu
    var = jnp.mean(xc * xc, axis=-1, keepdims=True)
    return xc * lax.rsqrt(var + LN_EPS) * g + b


def _sigmoid(x):
    return 1.0 / (1.0 + jnp.exp(-x))


def _split3(x, axis):
    p1 = x.astype(BF16)
    r1 = x - p1.astype(F32)
    p2 = r1.astype(BF16)
    p3 = (r1 - p2.astype(F32)).astype(BF16)
    return jnp.concatenate([p1, p2, p3], axis=axis)


def _ln_in_kernel(x_ref, g_ref, b_ref, o_ref):
    o_ref[...] = _layer_norm(x_ref[...], g_ref[...], b_ref[...]).astype(BF16)


def _ln_in(x, g, b, tm):
    n, d = x.shape
    return pl.pallas_call(
        _ln_in_kernel,
        grid=(n // tm,),
        in_specs=[pl.BlockSpec((tm, d), lambda i: (i, 0)),
                  pl.BlockSpec((1, d), lambda i: (0, 0)),
                  pl.BlockSpec((1, d), lambda i: (0, 0))],
        out_specs=pl.BlockSpec((tm, d), lambda i: (i, 0)),
        out_shape=jax.ShapeDtypeStruct((n, d), BF16),
        compiler_params=_cparams(("parallel",), 40),
        name="ln_in",
    )(x, g, b)


def _matmul_kernel(x_ref, w_ref, o_ref):
    o_ref[...] = jnp.dot(x_ref[...], w_ref[...], preferred_element_type=F32)


def _in_proj(xn, w, tm, tn):
    n, d = xn.shape
    d_in = w.shape[1]
    return pl.pallas_call(
        _matmul_kernel,
        grid=(d_in // tn, n // tm),
        in_specs=[pl.BlockSpec((tm, d), lambda j, i: (i, 0)),
                  pl.BlockSpec((d, tn), lambda j, i: (0, j))],
        out_specs=pl.BlockSpec((tm, tn), lambda j, i: (i, j)),
        out_shape=jax.ShapeDtypeStruct((n, d_in), F32),
        compiler_params=_cparams(("parallel", "parallel"), 48),
        name="in_proj",
    )(xn, w)


def _in_proj_sample_kernel(x_ref, g_ref, b_ref, w_ref, o_ref):
    xn = _layer_norm(x_ref[...], g_ref[...], b_ref[...])
    o_ref[...] = jnp.dot(xn, w_ref[...], precision=HIGHEST, preferred_element_type=F32)


def _in_proj_sample(x, g, b, w, tn):
    n, d = x.shape
    d_in = w.shape[1]
    return pl.pallas_call(
        _in_proj_sample_kernel,
        grid=(d_in // tn,),
        in_specs=[pl.BlockSpec((n, d), lambda j: (0, 0)),
                  pl.BlockSpec((1, d), lambda j: (0, 0)),
                  pl.BlockSpec((1, d), lambda j: (0, 0)),
                  pl.BlockSpec((d, tn), lambda j: (0, j))],
        out_specs=pl.BlockSpec((n, tn), lambda j: (0, j)),
        out_shape=jax.ShapeDtypeStruct((n, d_in), F32),
        compiler_params=_cparams(("parallel",), 40),
        name="in_proj_sample",
    )(x, g, b, w)


def _forget_gates(fz, lb):
    e = jnp.exp(-jnp.abs(fz))
    r = 1.0 / (1.0 + e)
    er = e * r
    pos = fz >= 0
    sig_p = jnp.where(pos, r, er)
    sig_n = jnp.where(pos, er, r)
    oml = 1.0 - lb
    return lb + oml * sig_p, oml * sig_n


def _chunk_matrices(c):
    t = np.arange(c)[:, None]
    j = np.arange(c)[None, :]
    mats = [(j <= t), (j > t)]
    blk = c
    while blk >= 2:
        half = blk // 2
        mid = (t // blk) * blk + half
        second = (t % blk) >= half
        m_q = (j >= mid) & (j <= t)
        m_k = (j > t) & (j < mid)
        mats.append(np.where(second, m_q, m_k))
        blk = half
    return np.concatenate(mats, axis=0).astype(np.float32)


def _mix_prompt_kernel(proj_ref, convw_ref, lb_ref, rmsg_ref, cmat_ref,
                       mix_ref, convst_ref, recst_ref, s_ref, carry_ref, *, d_conv):
    c = CHUNK
    tb = pl.program_id(1)
    n_tb = pl.num_programs(1)

    @pl.when(tb == 0)
    def _():
        s_ref[...] = jnp.zeros_like(s_ref)
        carry_ref[...] = jnp.zeros_like(carry_ref)

    u = proj_ref[:, 0:d_conv] * proj_ref[:, 2 * d_conv:3 * d_conv]
    row = lax.broadcasted_iota(jnp.int32, u.shape, 0)
    prev1 = carry_ref[SUBLANES - 1:SUBLANES, :]
    prev2 = carry_ref[SUBLANES - 2:SUBLANES - 1, :]
    u1 = jnp.where(row == 0, prev1, pltpu.roll(u, 1, 0))
    u2 = jnp.where(row == 0, prev2, jnp.where(row == 1, prev1, pltpu.roll(u, 2, 0)))
    y = convw_ref[0:1, :] * u2 + convw_ref[1:2, :] * u1 + convw_ref[2:3, :] * u
    mix_ref[:, 0:d_conv] = (proj_ref[:, d_conv:2 * d_conv] * y).astype(BF16)
    carry_ref[...] = u[c - SUBLANES:c, :]
    convst_ref[0] = u[c - SUBLANES:c, :]

    o0 = 3 * d_conv
    d_rec = N_HEADS * HEAD_K
    q = proj_ref[:, o0:o0 + d_rec]
    fz = proj_ref[:, o0 + d_rec:o0 + 2 * d_rec]
    v = proj_ref[:, o0 + 2 * d_rec:o0 + 3 * d_rec]
    g = proj_ref[:, o0 + 3 * d_rec:o0 + 4 * d_rec]
    f, kk = _forget_gates(fz, lb_ref[...])
    contract0 = (((0,), (0,)), ((), ()))
    logf3 = _split3(jnp.log(f), axis=0)
    ex = jnp.dot(cmat_ref[...], logf3, preferred_element_type=F32)
    b_cum = ex[0:c]
    d_end = ex[c:2 * c]
    n_lev = cmat_ref.shape[0] // c - 2
    b_cols = lax.dot_general(logf3, jnp.ones((3 * c, HEAD_V), BF16), contract0,
                             preferred_element_type=F32)

    trow = lax.broadcasted_iota(jnp.int32, (c, c), 0)
    tcol = lax.broadcasted_iota(jnp.int32, (c, c), 1)
    prow = lax.broadcasted_iota(jnp.int32, (c, HEAD_K), 0)
    contract1 = (((1,), (1,)), ((), ()))

    for h in range(N_HEADS):
        sl = slice(h * HEAD_K, (h + 1) * HEAD_K)
        qh, kh, vh = q[:, sl], kk[:, sl], v[:, sl]
        vb = vh.astype(BF16)
        s_old = s_ref[h]
        o = jnp.dot((qh * jnp.exp(b_cum[:, sl])).astype(BF16), s_old.astype(BF16),
                    preferred_element_type=F32)
        sc = jnp.zeros((c, c), F32)
        for lev in range(n_lev):
            blk = c >> lev
            sh = blk.bit_length() - 1
            dl = jnp.exp(ex[(2 + lev) * c:(3 + lev) * c, sl])
            second = (prow & (blk - 1)) >= (blk // 2)
            qt = jnp.where(second, qh * dl, 0.0).astype(BF16)
            kt = jnp.where(second, 0.0, kh * dl).astype(BF16)
            s_l = lax.dot_general(qt, kt, contract1, preferred_element_type=F32)
            sc = sc + jnp.where((trow >> sh) == (tcol >> sh), s_l, 0.0)
        o = o + jnp.dot(sc.astype(BF16), vb, preferred_element_type=F32)
        o = o + jnp.sum(qh * kh, axis=1, keepdims=True) * vh
        khat = (kh * jnp.exp(d_end[:, sl])).astype(BF16)
        upd = lax.dot_general(khat, vb, contract0, preferred_element_type=F32)
        s_ref[h] = jnp.exp(b_cols[sl, :]) * s_old + upd
        on = o * lax.rsqrt(jnp.mean(o * o, axis=1, keepdims=True) + RMS_EPS) * rmsg_ref[:, sl]
        gh = g[:, sl]
        mix_ref[:, d_conv + h * HEAD_V:d_conv + (h + 1) * HEAD_V] = (
            on * (gh * _sigmoid(gh))).astype(BF16)

    @pl.when(tb == n_tb - 1)
    def _():
        recst_ref[0] = s_ref[...]


def _mix_prompt(proj, conv_w, lb, rms_g, n_tok, bsz, seq, d_conv):
    d_in = proj.shape[1]
    d_mix = d_conv + N_HEADS * HEAD_V
    n_tb = seq // CHUNK
    cmat = jnp.asarray(np.tile(_chunk_matrices(CHUNK), (1, 3)), dtype=BF16)
    kern = functools.partial(_mix_prompt_kernel, d_conv=d_conv)
    return pl.pallas_call(
        kern,
        grid=(bsz, n_tb),
        in_specs=[pl.BlockSpec((CHUNK, d_in), lambda b, t: (b * n_tb + t, 0)),
                  pl.BlockSpec((CONV_W, d_conv), lambda b, t: (0, 0)),
                  pl.BlockSpec((1, N_HEADS * HEAD_K), lambda b, t: (0, 0)),
                  pl.BlockSpec((1, N_HEADS * HEAD_V), lambda b, t: (0, 0)),
                  pl.BlockSpec(cmat.shape, lambda b, t: (0, 0))],
        out_specs=[pl.BlockSpec((CHUNK, d_mix), lambda b, t: (b * n_tb + t, 0)),
                   pl.BlockSpec((1, SUBLANES, d_conv), lambda b, t: (b, 0, 0)),
                   pl.BlockSpec((1, N_HEADS, HEAD_K, HEAD_V), lambda b, t: (b, 0, 0, 0))],
        out_shape=[jax.ShapeDtypeStruct((n_tok, d_mix), BF16),
                   jax.ShapeDtypeStruct((bsz, SUBLANES, d_conv), F32),
                   jax.ShapeDtypeStruct((bsz, N_HEADS, HEAD_K, HEAD_V), F32)],
        scratch_shapes=[pltpu.VMEM((N_HEADS, HEAD_K, HEAD_V), F32),
                        pltpu.VMEM((SUBLANES, d_conv), F32)],
        compiler_params=_cparams(("parallel", "arbitrary"), 40),
        name="mix_prompt",
    )(proj, conv_w, lb, rms_g, cmat)


def _mix_sample_kernel(proj_ref, cst_ref, rst_ref, convw_ref, lb_ref, rmsg_ref, sel_ref,
                       mix_ref, cnew_ref, rnew_ref, *, d_conv):
    nb = SAMPLE_BLOCK
    u = proj_ref[:, 0:d_conv] * proj_ref[:, 2 * d_conv:3 * d_conv]
    buf0 = cst_ref[:, 0:d_conv]
    buf1 = cst_ref[:, d_conv:2 * d_conv]
    y = convw_ref[0:1, :] * buf0 + convw_ref[1:2, :] * buf1 + convw_ref[2:3, :] * u
    mix_ref[:, 0:d_conv] = proj_ref[:, d_conv:2 * d_conv] * y
    cnew_ref[:, 0:d_conv] = buf1
    cnew_ref[:, d_conv:2 * d_conv] = u

    o0 = 3 * d_conv
    d_rec = N_HEADS * HEAD_K
    q = proj_ref[:, o0:o0 + d_rec]
    fz = proj_ref[:, o0 + d_rec:o0 + 2 * d_rec]
    v = proj_ref[:, o0 + 2 * d_rec:o0 + 3 * d_rec]
    g = proj_ref[:, o0 + 3 * d_rec:o0 + 4 * d_rec]
    f, kk = _forget_gates(fz, lb_ref[...])
    contract0 = (((0,), (0,)), ((), ()))
    sel = sel_ref[...]
    row = lax.broadcasted_iota(jnp.int32, (nb, HEAD_V), 0)

    def columns(a):
        return lax.dot_general(_split3(a, axis=0), sel, contract0, preferred_element_type=F32)

    for h in range(N_HEADS):
        sl = slice(h * HEAD_K, (h + 1) * HEAD_K)
        f_c, k_c, q_c = columns(f[:, sl]), columns(kk[:, sl]), columns(q[:, sl])
        o = jnp.zeros((nb, HEAD_V), F32)
        for n in range(nb):
            nl = slice(n * HEAD_V, (n + 1) * HEAD_V)
            s_new = f_c[:, nl] * rst_ref[n, h] + k_c[:, nl] * v[n:n + 1, sl]
            rnew_ref[n, h] = s_new
            o_row = jnp.sum(q_c[:, nl] * s_new, axis=0, keepdims=True)
            o = jnp.where(row == n, o_row, o)
        on = o * lax.rsqrt(jnp.mean(o * o, axis=1, keepdims=True) + RMS_EPS) * rmsg_ref[:, sl]
        gh = g[:, sl]
        mix_ref[:, d_conv + h * HEAD_V:d_conv + (h + 1) * HEAD_V] = on * (gh * _sigmoid(gh))


def _mix_sample(proj, conv_state, rec_state, conv_w, lb, rms_g, d_conv):
    n_seq = conv_state.shape[0]
    d_in = proj.shape[1]
    d_mix = d_conv + N_HEADS * HEAD_V
    nb = SAMPLE_BLOCK
    sel = jnp.asarray(np.tile(np.kron(np.eye(nb), np.ones((1, HEAD_V))), (3, 1)), dtype=BF16)
    kern = functools.partial(_mix_sample_kernel, d_conv=d_conv)
    return pl.pallas_call(
        kern,
        grid=(n_seq // nb,),
        in_specs=[pl.BlockSpec((nb, d_in), lambda i: (i, 0)),
                  pl.BlockSpec((nb, 2 * d_conv), lambda i: (i, 0)),
                  pl.BlockSpec((nb, N_HEADS, HEAD_K, HEAD_V), lambda i: (i, 0, 0, 0)),
                  pl.BlockSpec((CONV_W, d_conv), lambda i: (0, 0)),
                  pl.BlockSpec((1, N_HEADS * HEAD_K), lambda i: (0, 0)),
                  pl.BlockSpec((1, N_HEADS * HEAD_V), lambda i: (0, 0)),
                  pl.BlockSpec(sel.shape, lambda i: (0, 0))],
        out_specs=[pl.BlockSpec((nb, d_mix), lambda i: (i, 0)),
                   pl.BlockSpec((nb, 2 * d_conv), lambda i: (i, 0)),
                   pl.BlockSpec((nb, N_HEADS, HEAD_K, HEAD_V), lambda i: (i, 0, 0, 0))],
        out_shape=[jax.ShapeDtypeStruct((n_seq, d_mix), F32),
                   jax.ShapeDtypeStruct((n_seq, 2 * d_conv), F32),
                   jax.ShapeDtypeStruct(rec_state.shape, F32)],
        compiler_params=_cparams(("parallel",), 52),
        name="mix_sample",
    )(proj, conv_state, rec_state, conv_w, lb, rms_g, sel)


def _post_mix_kernel(mixp_ref, mixs_ref, xp_ref, xs_ref, woutb_ref, woutf_ref, ling_ref, linb_ref,
                     l1g_ref, l1b_ref, wr_ref, wr2_ref, br_ref,
                     x1_ref, x1p_ref, route_ref, gate_ref, cnt_ref, run_ref, h_ref, lg_ref,
                     *, alpha, n_pt):
    i = pl.program_id(0)

    @pl.when(i == 0)
    def _():
        run_ref[...] = jnp.zeros_like(run_ref)

    @pl.when(i < n_pt)
    def _():
        h_ref[...] = jnp.dot(mixp_ref[...], woutb_ref[...], preferred_element_type=F32)

    @pl.when(i >= n_pt)
    def _():
        h_ref[...] = jnp.dot(mixs_ref[...], woutf_ref[...], precision=HIGHEST,
                             preferred_element_type=F32)

    x = jnp.where(i < n_pt, xp_ref[...], xs_ref[...])
    xn = _layer_norm(x, ling_ref[...], linb_ref[...])
    x1 = _layer_norm(alpha * xn + h_ref[...], l1g_ref[...], l1b_ref[...])
    x1_ref[...] = x1
    half = x1.shape[1] // 2
    bits = pltpu.bitcast(x1.astype(BF16).astype(F32), jnp.uint32)
    x1p_ref[...] = (bits[:, half:] & jnp.uint32(0xFFFF0000)) | (bits[:, :half] >> 16)

    tm = x1.shape[0]

    @pl.when(i < n_pt)
    def _():
        xh = x1.astype(BF16)
        xl = (x1 - xh.astype(F32)).astype(BF16)
        pr = jnp.dot(jnp.concatenate([xh, xl], axis=0), wr2_ref[...], preferred_element_type=F32)
        lg_ref[...] = (pr[0:tm, 0:LANES] + pr[0:tm, LANES:2 * LANES]
                       + pr[tm:2 * tm, 0:LANES] + pr[tm:2 * tm, LANES:2 * LANES])

    @pl.when(i >= n_pt)
    def _():
        lg_ref[...] = jnp.dot(x1, wr_ref[...], precision=HIGHEST, preferred_element_type=F32)

    logits = lg_ref[...] + br_ref[...]
    lane = lax.broadcasted_iota(jnp.int32, (tm, LANES), 1)
    lane_f = lane.astype(F32)
    work = logits
    vals, idxs = [], []
    for _ in range(TOP_K):
        m = jnp.max(work, axis=1, keepdims=True)
        ix = jnp.min(jnp.where(work == m, lane_f, float(LANES)), axis=1, keepdims=True)
        vals.append(m)
        idxs.append(ix)
        work = jnp.where(lane_f == ix, NEG_BIG, work)
    ex = [jnp.exp(vv - vals[0]) for vv in vals]
    den = ex[0] + ex[1] + ex[2] + ex[3]
    onehots = [(lane_f == ix).astype(F32) for ix in idxs]
    oh = onehots[0] + onehots[1] + onehots[2] + onehots[3]
    tr = lax.broadcasted_iota(jnp.int32, (tm, tm), 0)
    tc = lax.broadcasted_iota(jnp.int32, (tm, tm), 1)
    before = jnp.dot((tc < tr).astype(BF16), oh.astype(BF16), preferred_element_type=F32)
    pos = before + run_ref[...]
    route = jnp.zeros((tm, LANES), F32)
    gates = jnp.zeros((tm, LANES), F32)
    for k in range(TOP_K):
        rank = jnp.sum(onehots[k] * pos, axis=1, keepdims=True)
        route = jnp.where(lane == k, idxs[k], route)
        route = jnp.where(lane == TOP_K + k, rank, route)
        gates = jnp.where(lane == k, ex[k] / den, gates)
    route_ref[...] = route.astype(jnp.int32)
    gate_ref[...] = gates
    run_ref[...] = run_ref[...] + jnp.sum(oh, axis=0, keepdims=True)
    cnt_ref[...] = run_ref[...].astype(jnp.int32)


def _post_mix(mix_p, mix_s, x_p, x_s, w_out_b, w_out_f, lin_g, lin_b, l1_g, l1_b, w_r, w_r2, b_r, alpha, tm):
    d = x_p.shape[1]
    n = x_p.shape[0] + x_s.shape[0]
    d_mix = mix_p.shape[1]
    n_pt = mix_p.shape[0] // tm
    assert mix_p.shape[0] % tm == 0 and mix_s.shape[0] % tm == 0
    row = lambda i: (i, 0)
    fixed = lambda i: (0, 0)
    prompt_row = lambda i: (jnp.minimum(i, n_pt - 1), 0)
    sample_row = lambda i: (jnp.maximum(i - n_pt, 0), 0)
    once = pl.Buffered(1)
    kern = functools.partial(_post_mix_kernel, alpha=alpha, n_pt=n_pt)
    return pl.pallas_call(
        kern,
        grid=(n // tm,),
        in_specs=[pl.BlockSpec((tm, d_mix), prompt_row),
                  pl.BlockSpec((tm, d_mix), sample_row),
                  pl.BlockSpec((tm, d), prompt_row),
                  pl.BlockSpec((tm, d), sample_row),
                  pl.BlockSpec(w_out_b.shape, fixed, pipeline_mode=once),
                  pl.BlockSpec(w_out_f.shape, fixed, pipeline_mode=once),
                  pl.BlockSpec((1, d), fixed), pl.BlockSpec((1, d), fixed),
                  pl.BlockSpec((1, d), fixed), pl.BlockSpec((1, d), fixed),
                  pl.BlockSpec((d, LANES), fixed), pl.BlockSpec((d, 2 * LANES), fixed),
                  pl.BlockSpec((1, LANES), fixed)],
        out_specs=[pl.BlockSpec((tm, d), row),
                   pl.BlockSpec((tm, d // 2), row),
                   pl.BlockSpec((tm, LANES), row),
                   pl.BlockSpec((tm, LANES), row),
                   pl.BlockSpec((1, LANES), fixed)],
        out_shape=[jax.ShapeDtypeStruct((n, d), F32),
                   jax.ShapeDtypeStruct((n, d // 2), jnp.uint32),
                   jax.ShapeDtypeStruct((n, LANES), jnp.int32),
                   jax.ShapeDtypeStruct((n, LANES), F32),
                   jax.ShapeDtypeStruct((1, LANES), jnp.int32)],
        scratch_shapes=[pltpu.VMEM((1, LANES), F32), pltpu.VMEM((tm, d), F32),
                        pltpu.VMEM((tm, LANES), F32)],
        compiler_params=_cparams(("arbitrary",), 48),
        name="post_mix",
    )(mix_p, mix_s, x_p, x_s, w_out_b, w_out_f, lin_g, lin_b, l1_g, l1_b, w_r, w_r2, b_r)


def _dispatch_kernel(dest_ref, padlo_ref, padhi_ref, x_ref, xs_hbm, zero_ref, sem, *, tm, n_experts):
    i = pl.program_id(0)

    def row_copy(src, dst_row):
        return pltpu.make_async_copy(src, xs_hbm.at[pl.ds(dst_row, 1), :], sem)

    @pl.when(i == 0)
    def _():
        zero_ref[...] = jnp.zeros_like(zero_ref)

        def per_expert(e, carry):
            def start(r, c):
                row_copy(zero_ref.at[pl.ds(0, 1), :], r).start()
                return c

            def wait(r, c):
                row_copy(zero_ref.at[pl.ds(0, 1), :], r).wait()
                return c

            lax.fori_loop(padlo_ref[e], padhi_ref[e], start, 0)
            lax.fori_loop(padlo_ref[e], padhi_ref[e], wait, 0)
            return carry

        lax.fori_loop(0, n_experts, per_expert, 0)

        tail0 = padhi_ref[n_experts - 1]
        n_tail = (xs_hbm.shape[0] - tail0) // ROW_BLOCK

        def tail_copy(c):
            r0 = pl.multiple_of(tail0 + c * ROW_BLOCK, ROW_BLOCK)
            return pltpu.make_async_copy(zero_ref, xs_hbm.at[pl.ds(r0, ROW_BLOCK), :], sem)

        def tail_start(c, carry):
            tail_copy(c).start()
            return carry

        def tail_wait(c, carry):
            tail_copy(c).wait()
            return carry

        lax.fori_loop(0, n_tail, tail_start, 0)
        lax.fori_loop(0, n_tail, tail_wait, 0)

    def start(t, c):
        src = x_ref.at[pl.ds(t, 1), :]
        for k in range(TOP_K):
            row_copy(src, dest_ref[t * TOP_K + k]).start(priority=k % DMA_THREADS)
        return c

    def wait(t, c):
        src = x_ref.at[pl.ds(t, 1), :]
        for k in range(TOP_K):
            row_copy(src, dest_ref[t * TOP_K + k]).wait()
        return c

    lax.fori_loop(0, tm, start, 0, unroll=4)
    lax.fori_loop(0, tm, wait, 0, unroll=4)


def _dispatch(x1p, dest_flat, pad_lo, pad_hi, n_slots, tm):
    n, dh = x1p.shape
    n_experts = pad_lo.shape[0]
    kern = functools.partial(_dispatch_kernel, tm=tm, n_experts=n_experts)
    return pl.pallas_call(
        kern,
        grid=(n // tm,),
        in_specs=[pl.BlockSpec((tm * TOP_K,), lambda i: (i,), memory_space=pltpu.SMEM),
                  pl.BlockSpec(memory_space=pltpu.SMEM),
                  pl.BlockSpec(memory_space=pltpu.SMEM),
                  pl.BlockSpec((tm, dh), lambda i: (i, 0))],
        out_specs=pl.BlockSpec(memory_space=pl.ANY),
        out_shape=jax.ShapeDtypeStruct((n_slots, dh), jnp.uint32),
        scratch_shapes=[pltpu.VMEM((ROW_BLOCK, dh), jnp.uint32), pltpu.SemaphoreType.DMA(())],
        compiler_params=_cparams(("arbitrary",), 32),
        name="dispatch",
    )(dest_flat, pad_lo, pad_hi, x1p)


def _expert_kernel(ie_ref, is_ref, ir_ref, xs_hbm, wg_ref, wu_ref, wd_ref, bg_ref, bu_ref, bd_ref,
                   ys_hbm, xbuf, ybuf, wgu_bf, wd_bf, sem_in, sem_out):
    i = pl.program_id(0)
    j = pl.program_id(1)
    n_j = pl.num_programs(1)
    rows = ir_ref[i]
    start = is_ref[i]
    tf = wg_ref.shape[1]
    b_row = ie_ref[i] * n_j + j
    bg = bg_ref[pl.ds(b_row, 1), :]
    bu = bu_ref[pl.ds(b_row, 1), :]
    bd = bd_ref[pl.ds(ie_ref[i], 1), :]

    def in_copy(r0, size):
        g0 = pl.multiple_of(start + r0, ROW_BLOCK)
        return pltpu.make_async_copy(xs_hbm.at[pl.ds(g0, size), :], xbuf.at[pl.ds(r0, size), :], sem_in)

    def out_copy(r0, size):
        g0 = pl.multiple_of(start + r0, ROW_BLOCK)
        return pltpu.make_async_copy(ybuf.at[pl.ds(r0, size), :], ys_hbm.at[pl.ds(g0, size), :], sem_out)

    def for_blocks(fn):
        n_big = rows // MAX_BLOCK

        def body(c, carry):
            fn(pl.multiple_of(c * MAX_BLOCK, MAX_BLOCK), MAX_BLOCK)
            return carry
        lax.fori_loop(0, n_big, body, 0)
        base = n_big * MAX_BLOCK
        size = MAX_BLOCK // 2
        while size >= ROW_BLOCK:
            has = (rows & size) != 0

            @pl.when(has)
            def _(base=base, size=size):
                fn(pl.multiple_of(base, ROW_BLOCK), size)
            base = base + jnp.where(has, size, 0)
            size //= 2

    @pl.when(rows > 0)
    def _():
        @pl.when(j == 0)
        def _():
            for_blocks(lambda r0, size: in_copy(r0, size).start())

        wgu_bf[:, 0:tf] = wg_ref[...].astype(BF16)
        wgu_bf[:, tf:2 * tf] = wu_ref[...].astype(BF16)
        wd_bf[...] = wd_ref[...].astype(BF16)

        @pl.when(j == 0)
        def _():
            for_blocks(lambda r0, size: in_copy(r0, size).wait())

        def block(r0, size, first, last):
            xu = xbuf[pl.ds(r0, size), :]
            lo = pltpu.bitcast(xu << 16, F32).astype(BF16)
            hi = pltpu.bitcast(xu & jnp.uint32(0xFFFF0000), F32).astype(BF16)
            x = jnp.concatenate([lo, hi], axis=1)
            gu = jnp.dot(x, wgu_bf[...], preferred_element_type=F32)
            gg = jnp.minimum(gu[:, 0:tf] + bg, SWIGLU_LIMIT)
            uu = jnp.clip(gu[:, tf:2 * tf] + bu, -SWIGLU_LIMIT, SWIGLU_LIMIT)
            hid = gg * _sigmoid(SWIGLU_ALPHA * gg) * (uu + 1.0)
            y = jnp.dot(hid.astype(BF16), wd_bf[...], preferred_element_type=F32)
            if not first:
                y = y + ybuf[pl.ds(r0, size), :]
            if last:
                y = y + bd
            ybuf[pl.ds(r0, size), :] = y
            if last:
                out_copy(r0, size).start()

        @pl.when(j == 0)
        def _():
            for_blocks(lambda r0, size: block(r0, size, True, False))

        @pl.when(jnp.logical_and(j > 0, j < n_j - 1))
        def _():
            for_blocks(lambda r0, size: block(r0, size, False, False))

        @pl.when(j == n_j - 1)
        def _():
            for_blocks(lambda r0, size: block(r0, size, False, True))
            for_blocks(lambda r0, size: out_copy(r0, size).wait())

    @pl.when(jnp.logical_and(i == pl.num_programs(0) - 1, j == n_j - 1))
    def _():
        tail0 = is_ref[pl.num_programs(0)]
        n_tail = (ys_hbm.shape[0] - tail0) // ROW_BLOCK
        ybuf[0:ROW_BLOCK, :] = jnp.zeros((ROW_BLOCK, ybuf.shape[1]), F32)

        def tail_copy(c):
            g0 = pl.multiple_of(tail0 + c * ROW_BLOCK, ROW_BLOCK)
            return pltpu.make_async_copy(ybuf.at[pl.ds(0, ROW_BLOCK), :],
                                         ys_hbm.at[pl.ds(g0, ROW_BLOCK), :], sem_out)

        def tail_start(c, carry):
            tail_copy(c).start()
            return carry

        def tail_wait(c, carry):
            tail_copy(c).wait()
            return carry

        lax.fori_loop(0, n_tail, tail_start, 0)
        lax.fori_loop(0, n_tail, tail_wait, 0)


def _experts(xs, item_e, item_start, item_rows, w_gate, w_up, w_down, b_gate, b_up, b_down):
    n_slots, dh = xs.shape
    n_exp, d, d_ff = w_gate.shape
    n_items = item_e.shape[0]
    n_j = d_ff // FF_TILE
    assert n_j >= 2 and d == 2 * dh

    def jj(i, j, ir):
        return jnp.where(ir[i] > 0, j, n_j - 1)

    grid_spec = pltpu.PrefetchScalarGridSpec(
        num_scalar_prefetch=3,
        grid=(n_items, n_j),
        in_specs=[pl.BlockSpec(memory_space=pl.ANY),
                  pl.BlockSpec((None, d, FF_TILE), lambda i, j, ie, is_, ir: (ie[i], 0, jj(i, j, ir))),
                  pl.BlockSpec((None, d, FF_TILE), lambda i, j, ie, is_, ir: (ie[i], 0, jj(i, j, ir))),
                  pl.BlockSpec((None, FF_TILE, d), lambda i, j, ie, is_, ir: (ie[i], jj(i, j, ir), 0)),
                  pl.BlockSpec((n_exp * n_j, FF_TILE), lambda i, j, ie, is_, ir: (0, 0)),
                  pl.BlockSpec((n_exp * n_j, FF_TILE), lambda i, j, ie, is_, ir: (0, 0)),
                  pl.BlockSpec((n_exp, d), lambda i, j, ie, is_, ir: (0, 0))],
        out_specs=pl.BlockSpec(memory_space=pl.ANY),
        scratch_shapes=[pltpu.VMEM((ITEM_ROWS, dh), jnp.uint32),
                        pltpu.VMEM((ITEM_ROWS, d), F32),
                        pltpu.VMEM((d, 2 * FF_TILE), BF16),
                        pltpu.VMEM((FF_TILE, d), BF16),
                        pltpu.SemaphoreType.DMA(()),
                        pltpu.SemaphoreType.DMA(())],
    )
    return pl.pallas_call(
        _expert_kernel,
        grid_spec=grid_spec,
        out_shape=jax.ShapeDtypeStruct((n_slots, d), F32),
        compiler_params=_cparams(("arbitrary", "arbitrary"), 58),
        name="experts",
    )(item_e, item_start, item_rows, xs, w_gate, w_up, w_down,
      b_gate.reshape(n_exp * n_j, FF_TILE), b_up.reshape(n_exp * n_j, FF_TILE), b_down)


def _combine_kernel(dcur_ref, dnext_ref, ys_hbm, x1_ref, gate_ref, p_ref, wpg_ref, wpp_ref,
                    l2g_ref, l2b_ref, l3g_ref, l3b_ref, op_ref, os_ref, gbuf_a, gbuf_b, sems,
                    *, alpha, tm, n_pt):
    i = pl.program_id(0)
    n_i = pl.num_programs(0)

    def row_copy(dref, t, k, buf, sem):
        return pltpu.make_async_copy(ys_hbm.at[pl.ds(dref[t * TOP_K + k], 1), :],
                                     buf.at[k, pl.ds(t, 1), :], sem)

    def gather_loop(dref, buf, sem, wait):
        def body(t, c):
            for k in range(TOP_K):
                cp = row_copy(dref, t, k, buf, sem)
                if wait:
                    cp.wait()
                else:
                    cp.start(priority=k % DMA_THREADS)
            return c
        lax.fori_loop(0, tm, body, 0, unroll=4)

    def step(cur, cur_sem, nxt, nxt_sem):
        @pl.when(i == 0)
        def _():
            gather_loop(dcur_ref, cur, cur_sem, False)

        gather_loop(dcur_ref, cur, cur_sem, True)

        for t in range(tm):
            for k in range(TOP_K):
                row_copy(dnext_ref, t, k, nxt, nxt_sem).start(priority=k % DMA_THREADS)

        x1 = x1_ref[...]
        lane = lax.broadcasted_iota(jnp.int32, gate_ref.shape, 1)
        gates = gate_ref[...]
        ff = jnp.zeros_like(x1)
        for k in range(TOP_K):
            gk = jnp.sum(jnp.where(lane == k, gates, 0.0), axis=1, keepdims=True)
            ff = ff + gk * cur[k]
        x2 = _layer_norm(alpha * x1 + ff, l2g_ref[...], l2b_ref[...])
        eg = _sigmoid(jnp.dot(x2.astype(BF16), wpg_ref[...], preferred_element_type=F32))
        ep = jnp.dot(p_ref[...].astype(BF16), wpp_ref[...], preferred_element_type=F32)
        out = _layer_norm(alpha * x2 + eg * ep, l3g_ref[...], l3b_ref[...])

        @pl.when(i < n_pt)
        def _():
            op_ref[...] = out

        @pl.when(i >= n_pt)
        def _():
            os_ref[...] = out

        @pl.when(i == n_i - 1)
        def _():
            gather_loop(dnext_ref, nxt, nxt_sem, True)

    @pl.when(lax.rem(i, 2) == 0)
    def _():
        step(gbuf_a, sems.at[0], gbuf_b, sems.at[1])

    @pl.when(lax.rem(i, 2) == 1)
    def _():
        step(gbuf_b, sems.at[1], gbuf_a, sems.at[0])


def _combine(ys, dest_flat, x1, gates, p, w_pg, w_pp, l2_g, l2_b, l3_g, l3_b, alpha, tm, n_prompt):
    n, d = x1.shape
    n_i = n // tm
    n_pt = n_prompt // tm
    assert n_prompt % tm == 0 and n % tm == 0
    row = lambda i: (i, 0)
    fixed = lambda i: (0, 0)
    kern = functools.partial(_combine_kernel, alpha=alpha, tm=tm, n_pt=n_pt)
    return pl.pallas_call(
        kern,
        grid=(n_i,),
        in_specs=[pl.BlockSpec((tm * TOP_K,), lambda i: (i,), memory_space=pltpu.SMEM),
                  pl.BlockSpec((tm * TOP_K,), lambda i: (jnp.minimum(i + 1, n_i - 1),),
                               memory_space=pltpu.SMEM),
                  pl.BlockSpec(memory_space=pl.ANY),
                  pl.BlockSpec((tm, d), row),
                  pl.BlockSpec((tm, LANES), row),
                  pl.BlockSpec((tm, p.shape[1]), row),
                  pl.BlockSpec(w_pg.shape, fixed),
                  pl.BlockSpec(w_pp.shape, fixed),
                  pl.BlockSpec((1, d), fixed), pl.BlockSpec((1, d), fixed),
                  pl.BlockSpec((1, d), fixed), pl.BlockSpec((1, d), fixed)],
        out_specs=[pl.BlockSpec((tm, d), lambda i: (jnp.minimum(i, n_pt - 1), 0)),
                   pl.BlockSpec((tm, d), lambda i: (jnp.maximum(i - n_pt, 0), 0))],
        out_shape=[jax.ShapeDtypeStruct((n_prompt, d), F32),
                   jax.ShapeDtypeStruct((n - n_prompt, d), F32)],
        scratch_shapes=[pltpu.VMEM((TOP_K, tm, d), F32), pltpu.VMEM((TOP_K, tm, d), F32),
                        pltpu.SemaphoreType.DMA((2,))],
        compiler_params=_cparams(("arbitrary",), 48),
        name="combine",
    )(dest_flat, dest_flat, ys, x1, gates, p, w_pg, w_pp, l2_g, l2_b, l3_g, l3_b)


def _routing_tables(expert_idx, rank, counts, n_items):
    n_exp = counts.shape[0]
    padded = (counts + ROW_BLOCK - 1) // ROW_BLOCK * ROW_BLOCK
    pad_end = jnp.cumsum(padded)
    pad_start = pad_end - padded
    dest = (pad_start[expert_idx] + rank).reshape(-1).astype(jnp.int32)
    items_per = (padded + ITEM_ROWS - 1) // ITEM_ROWS
    item_end = jnp.cumsum(items_per)
    ids = jnp.arange(n_items, dtype=jnp.int32)
    total = item_end[-1]
    last_valid = jnp.maximum(total - 1, 0)
    eff = jnp.minimum(ids, last_valid)
    e_of = jnp.minimum(jnp.searchsorted(item_end, eff, side='right'), n_exp - 1).astype(jnp.int32)
    sub = eff - (item_end[e_of] - items_per[e_of])
    start = pad_start[e_of] + sub * ITEM_ROWS
    rows = jnp.clip(padded[e_of] - sub * ITEM_ROWS, 0, ITEM_ROWS)
    rows = jnp.where(ids < total, rows, 0)
    start = jnp.concatenate([start, pad_end[-1:]])
    return (dest, (pad_start + counts).astype(jnp.int32), pad_end.astype(jnp.int32),
            e_of, start.astype(jnp.int32), rows.astype(jnp.int32))


def kernel(x_prompt, x_sample, state_conv, state_rec, p_prompt, p_sample, ln_in_g, ln_in_b, w_in, conv_w,
           lb_theta, rms_g, w_out, ln1_g, ln1_b, w_router, b_router, w_gate, b_gate, w_up, b_up, w_down,
           b_down, ln2_g, ln2_b, w_ple_gate, w_ple_proj, ln3_g, ln3_b):
    bp, sp, d = x_prompt.shape
    bs, ss, _ = x_sample.shape
    depth = w_in.shape[0]
    assert depth == 1 and ss == 1 and sp % CHUNK == 0
    d_conv = state_conv.shape[-1]
    n_exp = w_router.shape[-1]
    n_p = bp * sp
    n = n_p + bs
    alpha = (2 * depth) ** 0.25
    row2 = lambda a: a.reshape(1, -1)

    lb = jnp.cumsum(jax.nn.softmax(lb_theta.astype(F32), axis=0), axis=0)[0]
    x_p = x_prompt.reshape(n_p, d)
    x_s = x_sample.reshape(bs, d)
    p_all = jnp.concatenate([p_prompt[0].reshape(n_p, -1), p_sample[0].reshape(bs, -1)], axis=0)

    xn_p = _ln_in(x_p, row2(ln_in_g), row2(ln_in_b), tm=512)
    proj_p = _in_proj(xn_p, w_in[0].astype(BF16), tm=1024, tn=1024)
    proj_s = _in_proj_sample(x_s, row2(ln_in_g), row2(ln_in_b), w_in[0], tn=1024)

    mix_p, conv_tail, rec_p = _mix_prompt(proj_p, conv_w[0], row2(lb), row2(rms_g[0]), n_p, bp, sp, d_conv)
    mix_s, conv_s, rec_s = _mix_sample(proj_s, state_conv[0].reshape(bs, -1), state_rec[0], conv_w[0],
                                       row2(lb), row2(rms_g[0]), d_conv)

    w_r = jnp.zeros((d, LANES), F32).at[:, :n_exp].set(w_router[0])
    b_r = jnp.full((1, LANES), NEG_BIG, F32).at[0, :n_exp].set(b_router[0])
    w_r_hi = w_r.astype(BF16)
    w_r2 = jnp.concatenate([w_r_hi, (w_r - w_r_hi.astype(F32)).astype(BF16)], axis=1)
    x1, x1p, route, gates, counts = _post_mix(
        mix_p, mix_s, x_p, x_s, w_out[0].astype(BF16), w_out[0], row2(ln_in_g), row2(ln_in_b),
        row2(ln1_g[0]), row2(ln1_b[0]), w_r, w_r2, b_r, alpha, tm=128)

    n_slots = n * TOP_K + n_exp * ROW_BLOCK
    n_items = n_exp + n_slots // ITEM_ROWS
    dest, pad_lo, pad_hi, item_e, item_start, item_rows = _routing_tables(
        route[:, 0:TOP_K], route[:, TOP_K:2 * TOP_K], counts[0, :n_exp], n_items)

    xs = _dispatch(x1p, dest, pad_lo, pad_hi, n_slots, tm=128)
    ys = _experts(xs, item_e, item_start, item_rows, w_gate[0], w_up[0], w_down[0],
                  b_gate[0], b_up[0], b_down[0])
    y_p, y_s = _combine(ys, dest, x1, gates, p_all, w_ple_gate[0].astype(BF16), w_ple_proj[0].astype(BF16),
                        row2(ln2_g[0]), row2(ln2_b[0]), row2(ln3_g[0]), row2(ln3_b[0]), alpha, tm=128,
                        n_prompt=n_p)

    return (y_p.reshape(bp, sp, d),
            y_s.reshape(bs, ss, d),
            conv_tail[:, SUBLANES - (CONV_W - 1):, :][None],
            rec_p[None],
            conv_s.reshape(bs, CONV_W - 1, d_conv)[None],
            rec_s[None])
```

```python
import functools

import numpy as np
import jax
import jax.numpy as jnp
from jax import lax
from jax.experimental import pallas as pl
from jax.experimental.pallas import tpu as pltpu

F32 = jnp.float32
BF16 = jnp.bfloat16
HIGHEST = lax.Precision.HIGHEST

CONV_W = 3
N_HEADS = 8
HEAD_K = 128
HEAD_V = 128
TOP_K = 4
SWIGLU_LIMIT = 7.0
SWIGLU_ALPHA = 1.702
LN_EPS = 1e-5
RMS_EPS = 1e-6

LANES = 128
SUBLANES = 8
VMEM_PHYSICAL_BYTES = 64 * 1024 * 1024
DMA_THREADS = 2

CHUNK = 128
SAMPLE_BLOCK = 16
ROW_BLOCK = 128
MAX_BLOCK = 512
ITEM_ROWS = 1280
FF_TILE = 512
NEG_BIG = -1e30


def _cparams(sem, vmem_mb):
    return pltpu.CompilerParams(dimension_semantics=sem, vmem_limit_bytes=vmem_mb * 1024 * 1024)


def _layer_norm(x, g, b):
    mu = jnp.mean(x, axis=-1, keepdims=True)
    xc = x - mu
    var = jnp.mean(xc * xc, axis=-1, keepdims=True)
    return xc * lax.rsqrt(var + LN_EPS) * g + b


def _sigmoid(x):
    return 1.0 / (1.0 + jnp.exp(-x))


def _split3(x, axis):
    p1 = x.astype(BF16)
    r1 = x - p1.astype(F32)
    p2 = r1.astype(BF16)
    p3 = (r1 - p2.astype(F32)).astype(BF16)
    return jnp.concatenate([p1, p2, p3], axis=axis)


def _ln_in_kernel(x_ref, g_ref, b_ref, o_ref):
    o_ref[...] = _layer_norm(x_ref[...], g_ref[...], b_ref[...]).astype(BF16)


def _ln_in(x, g, b, tm):
    n, d = x.shape
    return pl.pallas_call(
        _ln_in_kernel,
        grid=(n // tm,),
        in_specs=[pl.BlockSpec((tm, d), lambda i: (i, 0)),
                  pl.BlockSpec((1, d), lambda i: (0, 0)),
                  pl.BlockSpec((1, d), lambda i: (0, 0))],
        out_specs=pl.BlockSpec((tm, d), lambda i: (i, 0)),
        out_shape=jax.ShapeDtypeStruct((n, d), BF16),
        compiler_params=_cparams(("parallel",), 40),
        name="ln_in",
    )(x, g, b)


def _matmul_kernel(x_ref, w_ref, o_ref, wb_ref):
    @pl.when(pl.program_id(1) == 0)
    def _():
        wb_ref[...] = w_ref[...].astype(BF16)

    o_ref[...] = jnp.dot(x_ref[...], wb_ref[...], preferred_element_type=F32)


def _in_proj(xn, w, tm, tn):
    n, d = xn.shape
    d_in = w.shape[1]
    return pl.pallas_call(
        _matmul_kernel,
        grid=(d_in // tn, n // tm),
        in_specs=[pl.BlockSpec((tm, d), lambda j, i: (i, 0)),
                  pl.BlockSpec((d, tn), lambda j, i: (0, j))],
        out_specs=pl.BlockSpec((tm, tn), lambda j, i: (i, j)),
        out_shape=jax.ShapeDtypeStruct((n, d_in), F32),
        scratch_shapes=[pltpu.VMEM((d, tn), BF16)],
        compiler_params=_cparams(("arbitrary", "arbitrary"), 48),
        name="in_proj",
    )(xn, w)


def _in_proj_sample_kernel(x_ref, g_ref, b_ref, w_ref, o_ref):
    xn = _layer_norm(x_ref[...], g_ref[...], b_ref[...])
    o_ref[...] = jnp.dot(xn, w_ref[...], precision=HIGHEST, preferred_element_type=F32)


def _in_proj_sample(x, g, b, w, tn):
    n, d = x.shape
    d_in = w.shape[1]
    return pl.pallas_call(
        _in_proj_sample_kernel,
        grid=(d_in // tn,),
        in_specs=[pl.BlockSpec((n, d), lambda j: (0, 0)),
                  pl.BlockSpec((1, d), lambda j: (0, 0)),
                  pl.BlockSpec((1, d), lambda j: (0, 0)),
                  pl.BlockSpec((d, tn), lambda j: (0, j))],
        out_specs=pl.BlockSpec((n, tn), lambda j: (0, j)),
        out_shape=jax.ShapeDtypeStruct((n, d_in), F32),
        compiler_params=_cparams(("parallel",), 40),
        name="in_proj_sample",
    )(x, g, b, w)


def _forget_gates(fz, lb):
    e = jnp.exp(-jnp.abs(fz))
    r = 1.0 / (1.0 + e)
    er = e * r
    pos = fz >= 0
    sig_p = jnp.where(pos, r, er)
    sig_n = jnp.where(pos, er, r)
    oml = 1.0 - lb
    return lb + oml * sig_p, oml * sig_n


def _chunk_matrices(c):
    t = np.arange(c)[:, None]
    j = np.arange(c)[None, :]
    mats = [(j <= t), (j > t)]
    blk = c
    while blk >= 2:
        half = blk // 2
        mid = (t // blk) * blk + half
        second = (t % blk) >= half
        m_q = (j >= mid) & (j <= t)
        m_k = (j > t) & (j < mid)
        mats.append(np.where(second, m_q, m_k))
        blk = half
    return np.concatenate(mats, axis=0).astype(np.float32)


def _mix_prompt_kernel(proj_ref, convw_ref, lb_ref, rmsg_ref, cmat_ref,
                       mix_ref, convst_ref, recst_ref, s_ref, carry_ref, *, d_conv):
    c = CHUNK
    tb = pl.program_id(1)
    n_tb = pl.num_programs(1)

    @pl.when(tb == 0)
    def _():
        s_ref[...] = jnp.zeros_like(s_ref)
        carry_ref[...] = jnp.zeros_like(carry_ref)

    u = proj_ref[:, 0:d_conv] * proj_ref[:, 2 * d_conv:3 * d_conv]
    row = lax.broadcasted_iota(jnp.int32, u.shape, 0)
    prev1 = carry_ref[SUBLANES - 1:SUBLANES, :]
    prev2 = carry_ref[SUBLANES - 2:SUBLANES - 1, :]
    u1 = jnp.where(row == 0, prev1, pltpu.roll(u, 1, 0))
    u2 = jnp.where(row == 0, prev2, jnp.where(row == 1, prev1, pltpu.roll(u, 2, 0)))
    y = convw_ref[0:1, :] * u2 + convw_ref[1:2, :] * u1 + convw_ref[2:3, :] * u
    mix_ref[:, 0:d_conv] = (proj_ref[:, d_conv:2 * d_conv] * y).astype(BF16)
    carry_ref[...] = u[c - SUBLANES:c, :]
    convst_ref[0] = u[c - SUBLANES:c, :]

    o0 = 3 * d_conv
    d_rec = N_HEADS * HEAD_K
    q = proj_ref[:, o0:o0 + d_rec]
    fz = proj_ref[:, o0 + d_rec:o0 + 2 * d_rec]
    v = proj_ref[:, o0 + 2 * d_rec:o0 + 3 * d_rec]
    g = proj_ref[:, o0 + 3 * d_rec:o0 + 4 * d_rec]
    f, kk = _forget_gates(fz, lb_ref[...])
    contract0 = (((0,), (0,)), ((), ()))
    logf3 = _split3(jnp.log(f), axis=0)
    ex = jnp.dot(cmat_ref[...], logf3, preferred_element_type=F32)
    b_cum = ex[0:c]
    d_end = ex[c:2 * c]
    n_lev = cmat_ref.shape[0] // c - 2
    b_cols = lax.dot_general(logf3, jnp.ones((3 * c, HEAD_V), BF16), contract0,
                             preferred_element_type=F32)

    trow = lax.broadcasted_iota(jnp.int32, (c, c), 0)
    tcol = lax.broadcasted_iota(jnp.int32, (c, c), 1)
    prow = lax.broadcasted_iota(jnp.int32, (c, HEAD_K), 0)
    contract1 = (((1,), (1,)), ((), ()))

    for h in range(N_HEADS):
        sl = slice(h * HEAD_K, (h + 1) * HEAD_K)
        qh, kh, vh = q[:, sl], kk[:, sl], v[:, sl]
        vb = vh.astype(BF16)
        s_old = s_ref[h]
        o = jnp.dot((qh * jnp.exp(b_cum[:, sl])).astype(BF16), s_old.astype(BF16),
                    preferred_element_type=F32)
        sc = jnp.zeros((c, c), F32)
        for lev in range(n_lev):
            blk = c >> lev
            sh = blk.bit_length() - 1
            dl = jnp.exp(ex[(2 + lev) * c:(3 + lev) * c, sl])
            second = (prow & (blk - 1)) >= (blk // 2)
            qt = jnp.where(second, qh * dl, 0.0).astype(BF16)
            kt = jnp.where(second, 0.0, kh * dl).astype(BF16)
            s_l = lax.dot_general(qt, kt, contract1, preferred_element_type=F32)
            sc = sc + jnp.where((trow >> sh) == (tcol >> sh), s_l, 0.0)
        o = o + jnp.dot(sc.astype(BF16), vb, preferred_element_type=F32)
        o = o + jnp.sum(qh * kh, axis=1, keepdims=True) * vh
        khat = (kh * jnp.exp(d_end[:, sl])).astype(BF16)
        upd = lax.dot_general(khat, vb, contract0, preferred_element_type=F32)
        s_ref[h] = jnp.exp(b_cols[sl, :]) * s_old + upd
        on = o * lax.rsqrt(jnp.mean(o * o, axis=1, keepdims=True) + RMS_EPS) * rmsg_ref[:, sl]
        gh = g[:, sl]
        mix_ref[:, d_conv + h * HEAD_V:d_conv + (h + 1) * HEAD_V] = (
            on * (gh * _sigmoid(gh))).astype(BF16)

    @pl.when(tb == n_tb - 1)
    def _():
        recst_ref[0] = s_ref[...]


def _mix_prompt(proj, conv_w, lb, rms_g, n_tok, bsz, seq, d_conv):
    d_in = proj.shape[1]
    d_mix = d_conv + N_HEADS * HEAD_V
    n_tb = seq // CHUNK
    cmat = jnp.asarray(np.tile(_chunk_matrices(CHUNK), (1, 3)), dtype=BF16)
    kern = functools.partial(_mix_prompt_kernel, d_conv=d_conv)
    return pl.pallas_call(
        kern,
        grid=(bsz, n_tb),
        in_specs=[pl.BlockSpec((CHUNK, d_in), lambda b, t: (b * n_tb + t, 0)),
                  pl.BlockSpec((CONV_W, d_conv), lambda b, t: (0, 0)),
                  pl.BlockSpec((1, N_HEADS * HEAD_K), lambda b, t: (0, 0)),
                  pl.BlockSpec((1, N_HEADS * HEAD_V), lambda b, t: (0, 0)),
                  pl.BlockSpec(cmat.shape, lambda b, t: (0, 0))],
        out_specs=[pl.BlockSpec((CHUNK, d_mix), lambda b, t: (b * n_tb + t, 0)),
                   pl.BlockSpec((1, SUBLANES, d_conv), lambda b, t: (b, 0, 0)),
                   pl.BlockSpec((1, N_HEADS, HEAD_K, HEAD_V), lambda b, t: (b, 0, 0, 0))],
        out_shape=[jax.ShapeDtypeStruct((n_tok, d_mix), BF16),
                   jax.ShapeDtypeStruct((bsz, SUBLANES, d_conv), F32),
                   jax.ShapeDtypeStruct((bsz, N_HEADS, HEAD_K, HEAD_V), F32)],
        scratch_shapes=[pltpu.VMEM((N_HEADS, HEAD_K, HEAD_V), F32),
                        pltpu.VMEM((SUBLANES, d_conv), F32)],
        compiler_params=_cparams(("parallel", "arbitrary"), 40),
        name="mix_prompt",
    )(proj, conv_w, lb, rms_g, cmat)


def _mix_sample_kernel(proj_ref, cst_ref, rst_ref, convw_ref, lb_ref, rmsg_ref, sel_ref,
                       mix_ref, cnew_ref, rnew_ref, *, d_conv):
    nb = SAMPLE_BLOCK
    u = proj_ref[:, 0:d_conv] * proj_ref[:, 2 * d_conv:3 * d_conv]
    buf0 = cst_ref[:, 0:d_conv]
    buf1 = cst_ref[:, d_conv:2 * d_conv]
    y = convw_ref[0:1, :] * buf0 + convw_ref[1:2, :] * buf1 + convw_ref[2:3, :] * u
    mix_ref[:, 0:d_conv] = proj_ref[:, d_conv:2 * d_conv] * y
    cnew_ref[:, 0:d_conv] = buf1
    cnew_ref[:, d_conv:2 * d_conv] = u

    o0 = 3 * d_conv
    d_rec = N_HEADS * HEAD_K
    q = proj_ref[:, o0:o0 + d_rec]
    fz = proj_ref[:, o0 + d_rec:o0 + 2 * d_rec]
    v = proj_ref[:, o0 + 2 * d_rec:o0 + 3 * d_rec]
    g = proj_ref[:, o0 + 3 * d_rec:o0 + 4 * d_rec]
    f, kk = _forget_gates(fz, lb_ref[...])
    contract0 = (((0,), (0,)), ((), ()))
    sel = sel_ref[...]
    row = lax.broadcasted_iota(jnp.int32, (nb, HEAD_V), 0)

    def columns(a):
        return lax.dot_general(_split3(a, axis=0), sel, contract0, preferred_element_type=F32)

    for h in range(N_HEADS):
        sl = slice(h * HEAD_K, (h + 1) * HEAD_K)
        f_c, k_c, q_c = columns(f[:, sl]), columns(kk[:, sl]), columns(q[:, sl])
        o = jnp.zeros((nb, HEAD_V), F32)
        for n in range(nb):
            nl = slice(n * HEAD_V, (n + 1) * HEAD_V)
            s_new = f_c[:, nl] * rst_ref[n, h] + k_c[:, nl] * v[n:n + 1, sl]
            rnew_ref[n, h] = s_new
            o_row = jnp.sum(q_c[:, nl] * s_new, axis=0, keepdims=True)
            o = jnp.where(row == n, o_row, o)
        on = o * lax.rsqrt(jnp.mean(o * o, axis=1, keepdims=True) + RMS_EPS) * rmsg_ref[:, sl]
        gh = g[:, sl]
        mix_ref[:, d_conv + h * HEAD_V:d_conv + (h + 1) * HEAD_V] = on * (gh * _sigmoid(gh))


def _mix_sample(proj, conv_state, rec_state, conv_w, lb, rms_g, d_conv):
    n_seq = conv_state.shape[0]
    d_in = proj.shape[1]
    d_mix = d_conv + N_HEADS * HEAD_V
    nb = SAMPLE_BLOCK
    sel = jnp.asarray(np.tile(np.kron(np.eye(nb), np.ones((1, HEAD_V))), (3, 1)), dtype=BF16)
    kern = functools.partial(_mix_sample_kernel, d_conv=d_conv)
    return pl.pallas_call(
        kern,
        grid=(n_seq // nb,),
        in_specs=[pl.BlockSpec((nb, d_in), lambda i: (i, 0)),
                  pl.BlockSpec((nb, 2 * d_conv), lambda i: (i, 0)),
                  pl.BlockSpec((nb, N_HEADS, HEAD_K, HEAD_V), lambda i: (i, 0, 0, 0)),
                  pl.BlockSpec((CONV_W, d_conv), lambda i: (0, 0)),
                  pl.BlockSpec((1, N_HEADS * HEAD_K), lambda i: (0, 0)),
                  pl.BlockSpec((1, N_HEADS * HEAD_V), lambda i: (0, 0)),
                  pl.BlockSpec(sel.shape, lambda i: (0, 0))],
        out_specs=[pl.BlockSpec((nb, d_mix), lambda i: (i, 0)),
                   pl.BlockSpec((nb, 2 * d_conv), lambda i: (i, 0)),
                   pl.BlockSpec((nb, N_HEADS, HEAD_K, HEAD_V), lambda i: (i, 0, 0, 0))],
        out_shape=[jax.ShapeDtypeStruct((n_seq, d_mix), F32),
                   jax.ShapeDtypeStruct((n_seq, 2 * d_conv), F32),
                   jax.ShapeDtypeStruct(rec_state.shape, F32)],
        compiler_params=_cparams(("parallel",), 52),
        name="mix_sample",
    )(proj, conv_state, rec_state, conv_w, lb, rms_g, sel)


def _post_mix_kernel(mixp_ref, mixs_ref, xp_ref, xs_ref, woutf_ref, ling_ref, linb_ref,
                     l1g_ref, l1b_ref, wr_ref, wr2_ref, br_ref,
                     x1_ref, x1p_ref, route_ref, gate_ref, cnt_ref, run_ref, h_ref, lg_ref, woutb_ref,
                     *, alpha, n_pt):
    i = pl.program_id(0)

    @pl.when(i == 0)
    def _():
        run_ref[...] = jnp.zeros_like(run_ref)
        woutb_ref[...] = woutf_ref[...].astype(BF16)

    @pl.when(i < n_pt)
    def _():
        h_ref[...] = jnp.dot(mixp_ref[...], woutb_ref[...], preferred_element_type=F32)

    @pl.when(i >= n_pt)
    def _():
        h_ref[...] = jnp.dot(mixs_ref[...], woutf_ref[...], precision=HIGHEST,
                             preferred_element_type=F32)

    x = jnp.where(i < n_pt, xp_ref[...], xs_ref[...])
    xn = _layer_norm(x, ling_ref[...], linb_ref[...])
    x1 = _layer_norm(alpha * xn + h_ref[...], l1g_ref[...], l1b_ref[...])
    x1_ref[...] = x1
    half = x1.shape[1] // 2
    bits = pltpu.bitcast(x1.astype(BF16).astype(F32), jnp.uint32)
    x1p_ref[...] = (bits[:, half:] & jnp.uint32(0xFFFF0000)) | (bits[:, :half] >> 16)

    tm = x1.shape[0]

    @pl.when(i < n_pt)
    def _():
        xh = x1.astype(BF16)
        xl = (x1 - xh.astype(F32)).astype(BF16)
        pr = jnp.dot(jnp.concatenate([xh, xl], axis=0), wr2_ref[...], preferred_element_type=F32)
        lg_ref[...] = (pr[0:tm, 0:LANES] + pr[0:tm, LANES:2 * LANES]
                       + pr[tm:2 * tm, 0:LANES] + pr[tm:2 * tm, LANES:2 * LANES])

    @pl.when(i >= n_pt)
    def _():
        lg_ref[...] = jnp.dot(x1, wr_ref[...], precision=HIGHEST, preferred_element_type=F32)

    logits = lg_ref[...] + br_ref[...]
    lane = lax.broadcasted_iota(jnp.int32, (tm, LANES), 1)
    lane_f = lane.astype(F32)
    work = logits
    vals, idxs = [], []
    for _ in range(TOP_K):
        m = jnp.max(work, axis=1, keepdims=True)
        ix = jnp.min(jnp.where(work == m, lane_f, float(LANES)), axis=1, keepdims=True)
        vals.append(m)
        idxs.append(ix)
        work = jnp.where(lane_f == ix, NEG_BIG, work)
    ex = [jnp.exp(vv - vals[0]) for vv in vals]
    den = ex[0] + ex[1] + ex[2] + ex[3]
    onehots = [(lane_f == ix).astype(F32) for ix in idxs]
    oh = onehots[0] + onehots[1] + onehots[2] + onehots[3]
    tr = lax.broadcasted_iota(jnp.int32, (tm, tm), 0)
    tc = lax.broadcasted_iota(jnp.int32, (tm, tm), 1)
    before = jnp.dot((tc < tr).astype(BF16), oh.astype(BF16), preferred_element_type=F32)
    pos = before + run_ref[...]
    route = jnp.zeros((tm, LANES), F32)
    gates = jnp.zeros((tm, LANES), F32)
    for k in range(TOP_K):
        rank = jnp.sum(onehots[k] * pos, axis=1, keepdims=True)
        route = jnp.where(lane == k, idxs[k], route)
        route = jnp.where(lane == TOP_K + k, rank, route)
        gates = jnp.where(lane == k, ex[k] / den, gates)
    route_ref[...] = route.T[0:2 * TOP_K, :].astype(jnp.int32)
    gate_ref[...] = gates
    run_ref[...] = run_ref[...] + jnp.sum(oh, axis=0, keepdims=True)
    cnt_ref[...] = run_ref[...].astype(jnp.int32)


def _post_mix(mix_p, mix_s, x_p, x_s, w_out_f, lin_g, lin_b, l1_g, l1_b, w_r, w_r2, b_r, alpha, tm):
    d = x_p.shape[1]
    n = x_p.shape[0] + x_s.shape[0]
    d_mix = mix_p.shape[1]
    n_pt = mix_p.shape[0] // tm
    assert mix_p.shape[0] % tm == 0 and mix_s.shape[0] % tm == 0
    row = lambda i: (i, 0)
    fixed = lambda i: (0, 0)
    prompt_row = lambda i: (jnp.minimum(i, n_pt - 1), 0)
    sample_row = lambda i: (jnp.maximum(i - n_pt, 0), 0)
    once = pl.Buffered(1)
    kern = functools.partial(_post_mix_kernel, alpha=alpha, n_pt=n_pt)
    return pl.pallas_call(
        kern,
        grid=(n // tm,),
        in_specs=[pl.BlockSpec((tm, d_mix), prompt_row),
                  pl.BlockSpec((tm, d_mix), sample_row),
                  pl.BlockSpec((tm, d), prompt_row),
                  pl.BlockSpec((tm, d), sample_row),
                  pl.BlockSpec(w_out_f.shape, fixed, pipeline_mode=once),
                  pl.BlockSpec((1, d), fixed), pl.BlockSpec((1, d), fixed),
                  pl.BlockSpec((1, d), fixed), pl.BlockSpec((1, d), fixed),
                  pl.BlockSpec((d, LANES), fixed), pl.BlockSpec((d, 2 * LANES), fixed),
                  pl.BlockSpec((1, LANES), fixed)],
        out_specs=[pl.BlockSpec((tm, d), row),
                   pl.BlockSpec((tm, d // 2), row),
                   pl.BlockSpec((2 * TOP_K, tm), lambda i: (0, i)),
                   pl.BlockSpec((tm, LANES), row),
                   pl.BlockSpec((1, LANES), fixed)],
        out_shape=[jax.ShapeDtypeStruct((n, d), F32),
                   jax.ShapeDtypeStruct((n, d // 2), jnp.uint32),
                   jax.ShapeDtypeStruct((2 * TOP_K, n), jnp.int32),
                   jax.ShapeDtypeStruct((n, LANES), F32),
                   jax.ShapeDtypeStruct((1, LANES), jnp.int32)],
        scratch_shapes=[pltpu.VMEM((1, LANES), F32), pltpu.VMEM((tm, d), F32),
                        pltpu.VMEM((tm, LANES), F32), pltpu.VMEM(w_out_f.shape, BF16)],
        compiler_params=_cparams(("arbitrary",), 48),
        name="post_mix",
    )(mix_p, mix_s, x_p, x_s, w_out_f, lin_g, lin_b, l1_g, l1_b, w_r, w_r2, b_r)


def _dispatch_kernel(dest_ref, padlo_ref, padhi_ref, x_ref, xs_hbm, zero_ref, sem, *, tm, n_experts):
    i = pl.program_id(0)

    def row_copy(src, dst_row):
        return pltpu.make_async_copy(src, xs_hbm.at[pl.ds(dst_row, 1), :], sem)

    @pl.when(i == 0)
    def _():
        zero_ref[...] = jnp.zeros_like(zero_ref)

        def per_expert(e, carry):
            def start(r, c):
                row_copy(zero_ref.at[pl.ds(0, 1), :], r).start()
                return c

            def wait(r, c):
                row_copy(zero_ref.at[pl.ds(0, 1), :], r).wait()
                return c

            lax.fori_loop(padlo_ref[e], padhi_ref[e], start, 0)
            lax.fori_loop(padlo_ref[e], padhi_ref[e], wait, 0)
            return carry

        lax.fori_loop(0, n_experts, per_expert, 0)

        tail0 = padhi_ref[n_experts - 1]
        n_tail = (xs_hbm.shape[0] - tail0) // ROW_BLOCK

        def tail_copy(c):
            r0 = pl.multiple_of(tail0 + c * ROW_BLOCK, ROW_BLOCK)
            return pltpu.make_async_copy(zero_ref, xs_hbm.at[pl.ds(r0, ROW_BLOCK), :], sem)

        def tail_start(c, carry):
            tail_copy(c).start()
            return carry

        def tail_wait(c, carry):
            tail_copy(c).wait()
            return carry

        lax.fori_loop(0, n_tail, tail_start, 0)
        lax.fori_loop(0, n_tail, tail_wait, 0)

    def start(t, c):
        src = x_ref.at[pl.ds(t, 1), :]
        for k in range(TOP_K):
            row_copy(src, dest_ref[k, t]).start(priority=k % DMA_THREADS)
        return c

    def wait(t, c):
        src = x_ref.at[pl.ds(t, 1), :]
        for k in range(TOP_K):
            row_copy(src, dest_ref[k, t]).wait()
        return c

    lax.fori_loop(0, tm, start, 0, unroll=4)
    lax.fori_loop(0, tm, wait, 0, unroll=4)


def _dispatch(x1p, dest_flat, pad_lo, pad_hi, n_slots, tm):
    n, dh = x1p.shape
    n_experts = pad_lo.shape[0]
    kern = functools.partial(_dispatch_kernel, tm=tm, n_experts=n_experts)
    return pl.pallas_call(
        kern,
        grid=(n // tm,),
        in_specs=[pl.BlockSpec((TOP_K, tm), lambda i: (0, i), memory_space=pltpu.SMEM),
                  pl.BlockSpec(memory_space=pltpu.SMEM),
                  pl.BlockSpec(memory_space=pltpu.SMEM),
                  pl.BlockSpec((tm, dh), lambda i: (i, 0))],
        out_specs=pl.BlockSpec(memory_space=pl.ANY),
        out_shape=jax.ShapeDtypeStruct((n_slots, dh), jnp.uint32),
        scratch_shapes=[pltpu.VMEM((ROW_BLOCK, dh), jnp.uint32), pltpu.SemaphoreType.DMA(())],
        compiler_params=_cparams(("arbitrary",), 32),
        name="dispatch",
    )(dest_flat, pad_lo, pad_hi, x1p)


def _expert_kernel(ie_ref, is_ref, ir_ref, xs_hbm, wg_ref, wu_ref, wd_ref, bg_ref, bu_ref, bd_ref,
                   ys_hbm, xbuf, ybuf, wgu_bf, wd_bf, sem_in, sem_out):
    i = pl.program_id(0)
    j = pl.program_id(1)
    n_j = pl.num_programs(1)
    rows = ir_ref[i]
    start = is_ref[i]
    tf = wg_ref.shape[1]
    b_row = ie_ref[i] * n_j + j
    bg = bg_ref[pl.ds(b_row, 1), :]
    bu = bu_ref[pl.ds(b_row, 1), :]
    bd = bd_ref[pl.ds(ie_ref[i], 1), :]

    def in_copy(r0, size):
        g0 = pl.multiple_of(start + r0, ROW_BLOCK)
        return pltpu.make_async_copy(xs_hbm.at[pl.ds(g0, size), :], xbuf.at[pl.ds(r0, size), :], sem_in)

    def out_copy(r0, size):
        g0 = pl.multiple_of(start + r0, ROW_BLOCK)
        return pltpu.make_async_copy(ybuf.at[pl.ds(r0, size), :], ys_hbm.at[pl.ds(g0, size), :], sem_out)

    def for_blocks(fn):
        n_big = rows // MAX_BLOCK

        def body(c, carry):
            fn(pl.multiple_of(c * MAX_BLOCK, MAX_BLOCK), MAX_BLOCK)
            return carry
        lax.fori_loop(0, n_big, body, 0)
        base = n_big * MAX_BLOCK
        size = MAX_BLOCK // 2
        while size >= ROW_BLOCK:
            has = (rows & size) != 0

            @pl.when(has)
            def _(base=base, size=size):
                fn(pl.multiple_of(base, ROW_BLOCK), size)
            base = base + jnp.where(has, size, 0)
            size //= 2

    @pl.when(rows > 0)
    def _():
        @pl.when(j == 0)
        def _():
            for_blocks(lambda r0, size: in_copy(r0, size).start())

        wgu_bf[:, 0:tf] = wg_ref[...].astype(BF16)
        wgu_bf[:, tf:2 * tf] = wu_ref[...].astype(BF16)
        wd_bf[...] = wd_ref[...].astype(BF16)

        @pl.when(j == 0)
        def _():
            for_blocks(lambda r0, size: in_copy(r0, size).wait())

        def block(r0, size, first, last):
            xu = xbuf[pl.ds(r0, size), :]
            lo = pltpu.bitcast(xu << 16, F32).astype(BF16)
            hi = pltpu.bitcast(xu & jnp.uint32(0xFFFF0000), F32).astype(BF16)
            x = jnp.concatenate([lo, hi], axis=1)
            gu = jnp.dot(x, wgu_bf[...], preferred_element_type=F32)
            gg = jnp.minimum(gu[:, 0:tf] + bg, SWIGLU_LIMIT)
            uu = jnp.clip(gu[:, tf:2 * tf] + bu, -SWIGLU_LIMIT, SWIGLU_LIMIT)
            hid = gg * _sigmoid(SWIGLU_ALPHA * gg) * (uu + 1.0)
            y = jnp.dot(hid.astype(BF16), wd_bf[...], preferred_element_type=F32)
            if not first:
                y = y + ybuf[pl.ds(r0, size), :]
            if last:
                y = y + bd
            ybuf[pl.ds(r0, size), :] = y
            if last:
                out_copy(r0, size).start()

        @pl.when(j == 0)
        def _():
            for_blocks(lambda r0, size: block(r0, size, True, False))

        @pl.when(jnp.logical_and(j > 0, j < n_j - 1))
        def _():
            for_blocks(lambda r0, size: block(r0, size, False, False))

        @pl.when(j == n_j - 1)
        def _():
            for_blocks(lambda r0, size: block(r0, size, False, True))
            for_blocks(lambda r0, size: out_copy(r0, size).wait())

    @pl.when(jnp.logical_and(i == pl.num_programs(0) - 1, j == n_j - 1))
    def _():
        tail0 = is_ref[pl.num_programs(0)]
        n_tail = (ys_hbm.shape[0] - tail0) // ROW_BLOCK
        ybuf[0:ROW_BLOCK, :] = jnp.zeros((ROW_BLOCK, ybuf.shape[1]), F32)

        def tail_copy(c):
            g0 = pl.multiple_of(tail0 + c * ROW_BLOCK, ROW_BLOCK)
            return pltpu.make_async_copy(ybuf.at[pl.ds(0, ROW_BLOCK), :],
                                         ys_hbm.at[pl.ds(g0, ROW_BLOCK), :], sem_out)

        def tail_start(c, carry):
            tail_copy(c).start()
            return carry

        def tail_wait(c, carry):
            tail_copy(c).wait()
            return carry

        lax.fori_loop(0, n_tail, tail_start, 0)
        lax.fori_loop(0, n_tail, tail_wait, 0)


def _experts(xs, item_e, item_start, item_rows, w_gate, w_up, w_down, b_gate, b_up, b_down):
    n_slots, dh = xs.shape
    n_exp, d, d_ff = w_gate.shape
    n_items = item_e.shape[0]
    n_j = d_ff // FF_TILE
    assert n_j >= 2 and d == 2 * dh

    def jj(i, j, ir):
        return jnp.where(ir[i] > 0, j, n_j - 1)

    grid_spec = pltpu.PrefetchScalarGridSpec(
        num_scalar_prefetch=3,
        grid=(n_items, n_j),
        in_specs=[pl.BlockSpec(memory_space=pl.ANY),
                  pl.BlockSpec((None, d, FF_TILE), lambda i, j, ie, is_, ir: (ie[i], 0, jj(i, j, ir))),
                  pl.BlockSpec((None, d, FF_TILE), lambda i, j, ie, is_, ir: (ie[i], 0, jj(i, j, ir))),
                  pl.BlockSpec((None, FF_TILE, d), lambda i, j, ie, is_, ir: (ie[i], jj(i, j, ir), 0)),
                  pl.BlockSpec((n_exp * n_j, FF_TILE), lambda i, j, ie, is_, ir: (0, 0)),
                  pl.BlockSpec((n_exp * n_j, FF_TILE), lambda i, j, ie, is_, ir: (0, 0)),
                  pl.BlockSpec((n_exp, d), lambda i, j, ie, is_, ir: (0, 0))],
        out_specs=pl.BlockSpec(memory_space=pl.ANY),
        scratch_shapes=[pltpu.VMEM((ITEM_ROWS, dh), jnp.uint32),
                        pltpu.VMEM((ITEM_ROWS, d), F32),
                        pltpu.VMEM((d, 2 * FF_TILE), BF16),
                        pltpu.VMEM((FF_TILE, d), BF16),
                        pltpu.SemaphoreType.DMA(()),
                        pltpu.SemaphoreType.DMA(())],
    )
    return pl.pallas_call(
        _expert_kernel,
        grid_spec=grid_spec,
        out_shape=jax.ShapeDtypeStruct((n_slots, d), F32),
        compiler_params=_cparams(("arbitrary", "arbitrary"), 58),
        name="experts",
    )(item_e, item_start, item_rows, xs, w_gate, w_up, w_down,
      b_gate.reshape(n_exp * n_j, FF_TILE), b_up.reshape(n_exp * n_j, FF_TILE), b_down)


def _combine_kernel(dcur_ref, dnext_ref, ys_hbm, x1_ref, gate_ref, p_ref, wpg_ref, wpp_ref,
                    l2g_ref, l2b_ref, l3g_ref, l3b_ref, op_ref, os_ref, gbuf_a, gbuf_b, wpgb_ref, sems,
                    *, alpha, tm, n_pt):
    i = pl.program_id(0)
    n_i = pl.num_programs(0)

    @pl.when(i == 0)
    def _():
        wpgb_ref[...] = wpg_ref[...].astype(BF16)

    def row_copy(dref, t, k, buf, sem):
        return pltpu.make_async_copy(ys_hbm.at[pl.ds(dref[k, t], 1), :],
                                     buf.at[k, pl.ds(t, 1), :], sem)

    def gather_loop(dref, buf, sem, wait):
        def body(t, c):
            for k in range(TOP_K):
                cp = row_copy(dref, t, k, buf, sem)
                if wait:
                    cp.wait()
                else:
                    cp.start(priority=k % DMA_THREADS)
            return c
        lax.fori_loop(0, tm, body, 0, unroll=4)

    def step(cur, cur_sem, nxt, nxt_sem):
        @pl.when(i == 0)
        def _():
            gather_loop(dcur_ref, cur, cur_sem, False)

        gather_loop(dcur_ref, cur, cur_sem, True)

        for t in range(tm):
            for k in range(TOP_K):
                row_copy(dnext_ref, t, k, nxt, nxt_sem).start(priority=k % DMA_THREADS)

        x1 = x1_ref[...]
        lane = lax.broadcasted_iota(jnp.int32, gate_ref.shape, 1)
        gates = gate_ref[...]
        ff = jnp.zeros_like(x1)
        for k in range(TOP_K):
            gk = jnp.sum(jnp.where(lane == k, gates, 0.0), axis=1, keepdims=True)
            ff = ff + gk * cur[k]
        x2 = _layer_norm(alpha * x1 + ff, l2g_ref[...], l2b_ref[...])
        eg = _sigmoid(jnp.dot(x2.astype(BF16), wpgb_ref[...], preferred_element_type=F32))
        ep = jnp.dot(p_ref[...].astype(BF16), wpp_ref[...], preferred_element_type=F32)
        out = _layer_norm(alpha * x2 + eg * ep, l3g_ref[...], l3b_ref[...])

        @pl.when(i < n_pt)
        def _():
            op_ref[...] = out

        @pl.when(i >= n_pt)
        def _():
            os_ref[...] = out

        @pl.when(i == n_i - 1)
        def _():
            gather_loop(dnext_ref, nxt, nxt_sem, True)

    @pl.when(lax.rem(i, 2) == 0)
    def _():
        step(gbuf_a, sems.at[0], gbuf_b, sems.at[1])

    @pl.when(lax.rem(i, 2) == 1)
    def _():
        step(gbuf_b, sems.at[1], gbuf_a, sems.at[0])


def _combine(ys, dest_flat, x1, gates, p, w_pg, w_pp, l2_g, l2_b, l3_g, l3_b, alpha, tm, n_prompt):
    n, d = x1.shape
    n_i = n // tm
    n_pt = n_prompt // tm
    assert n_prompt % tm == 0 and n % tm == 0
    row = lambda i: (i, 0)
    fixed = lambda i: (0, 0)
    kern = functools.partial(_combine_kernel, alpha=alpha, tm=tm, n_pt=n_pt)
    return pl.pallas_call(
        kern,
        grid=(n_i,),
        in_specs=[pl.BlockSpec((TOP_K, tm), lambda i: (0, i), memory_space=pltpu.SMEM),
                  pl.BlockSpec((TOP_K, tm), lambda i: (0, jnp.minimum(i + 1, n_i - 1)),
                               memory_space=pltpu.SMEM),
                  pl.BlockSpec(memory_space=pl.ANY),
                  pl.BlockSpec((tm, d), row),
                  pl.BlockSpec((tm, LANES), row),
                  pl.BlockSpec((tm, p.shape[1]), row),
                  pl.BlockSpec(w_pg.shape, fixed, pipeline_mode=pl.Buffered(1)),
                  pl.BlockSpec(w_pp.shape, fixed),
                  pl.BlockSpec((1, d), fixed), pl.BlockSpec((1, d), fixed),
                  pl.BlockSpec((1, d), fixed), pl.BlockSpec((1, d), fixed)],
        out_specs=[pl.BlockSpec((tm, d), lambda i: (jnp.minimum(i, n_pt - 1), 0)),
                   pl.BlockSpec((tm, d), lambda i: (jnp.maximum(i - n_pt, 0), 0))],
        out_shape=[jax.ShapeDtypeStruct((n_prompt, d), F32),
                   jax.ShapeDtypeStruct((n - n_prompt, d), F32)],
        scratch_shapes=[pltpu.VMEM((TOP_K, tm, d), F32), pltpu.VMEM((TOP_K, tm, d), F32),
                        pltpu.VMEM(w_pg.shape, BF16), pltpu.SemaphoreType.DMA((2,))],
        compiler_params=_cparams(("arbitrary",), 48),
        name="combine",
    )(dest_flat, dest_flat, ys, x1, gates, p, w_pg, w_pp, l2_g, l2_b, l3_g, l3_b)


def _routing_tables(expert_idx, rank, counts, n_items):
    n_exp = counts.shape[0]
    padded = (counts + ROW_BLOCK - 1) // ROW_BLOCK * ROW_BLOCK
    pad_end = jnp.cumsum(padded)
    pad_start = pad_end - padded
    dest = (pad_start[expert_idx] + rank).astype(jnp.int32)
    items_per = (padded + ITEM_ROWS - 1) // ITEM_ROWS
    item_end = jnp.cumsum(items_per)
    ids = jnp.arange(n_items, dtype=jnp.int32)
    total = item_end[-1]
    last_valid = jnp.maximum(total - 1, 0)
    eff = jnp.minimum(ids, last_valid)
    e_of = jnp.minimum(jnp.searchsorted(item_end, eff, side='right'), n_exp - 1).astype(jnp.int32)
    sub = eff - (item_end[e_of] - items_per[e_of])
    start = pad_start[e_of] + sub * ITEM_ROWS
    rows = jnp.clip(padded[e_of] - sub * ITEM_ROWS, 0, ITEM_ROWS)
    rows = jnp.where(ids < total, rows, 0)
    start = jnp.concatenate([start, pad_end[-1:]])
    return (dest, (pad_start + counts).astype(jnp.int32), pad_end.astype(jnp.int32),
            e_of, start.astype(jnp.int32), rows.astype(jnp.int32))


def kernel(x_prompt, x_sample, state_conv, state_rec, p_prompt, p_sample, ln_in_g, ln_in_b, w_in, conv_w,
           lb_theta, rms_g, w_out, ln1_g, ln1_b, w_router, b_router, w_gate, b_gate, w_up, b_up, w_down,
           b_down, ln2_g, ln2_b, w_ple_gate, w_ple_proj, ln3_g, ln3_b):
    bp, sp, d = x_prompt.shape
    bs, ss, _ = x_sample.shape
    depth = w_in.shape[0]
    assert depth == 1 and ss == 1 and sp % CHUNK == 0
    d_conv = state_conv.shape[-1]
    n_exp = w_router.shape[-1]
    n_p = bp * sp
    n = n_p + bs
    alpha = (2 * depth) ** 0.25
    row2 = lambda a: a.reshape(1, -1)

    lb = jnp.cumsum(jax.nn.softmax(lb_theta.astype(F32), axis=0), axis=0)[0]
    x_p = x_prompt.reshape(n_p, d)
    x_s = x_sample.reshape(bs, d)
    p_all = jnp.concatenate([p_prompt[0].reshape(n_p, -1), p_sample[0].reshape(bs, -1)], axis=0)

    xn_p = _ln_in(x_p, row2(ln_in_g), row2(ln_in_b), tm=512)
    proj_p = _in_proj(xn_p, w_in[0], tm=1024, tn=1024)
    proj_s = _in_proj_sample(x_s, row2(ln_in_g), row2(ln_in_b), w_in[0], tn=1024)

    mix_p, conv_tail, rec_p = _mix_prompt(proj_p, conv_w[0], row2(lb), row2(rms_g[0]), n_p, bp, sp, d_conv)
    mix_s, conv_s, rec_s = _mix_sample(proj_s, state_conv[0].reshape(bs, -1), state_rec[0], conv_w[0],
                                       row2(lb), row2(rms_g[0]), d_conv)

    w_r = jnp.zeros((d, LANES), F32).at[:, :n_exp].set(w_router[0])
    b_r = jnp.full((1, LANES), NEG_BIG, F32).at[0, :n_exp].set(b_router[0])
    w_r_hi = w_r.astype(BF16)
    w_r2 = jnp.concatenate([w_r_hi, (w_r - w_r_hi.astype(F32)).astype(BF16)], axis=1)
    x1, x1p, route, gates, counts = _post_mix(
        mix_p, mix_s, x_p, x_s, w_out[0], row2(ln_in_g), row2(ln_in_b),
        row2(ln1_g[0]), row2(ln1_b[0]), w_r, w_r2, b_r, alpha, tm=128)

    n_slots = n * TOP_K + n_exp * ROW_BLOCK
    n_items = n_exp + n_slots // ITEM_ROWS
    dest, pad_lo, pad_hi, item_e, item_start, item_rows = _routing_tables(
        route[0:TOP_K], route[TOP_K:2 * TOP_K], counts[0, :n_exp], n_items)

    xs = _dispatch(x1p, dest, pad_lo, pad_hi, n_slots, tm=128)
    ys = _experts(xs, item_e, item_start, item_rows, w_gate[0], w_up[0], w_down[0],
                  b_gate[0], b_up[0], b_down[0])
    y_p, y_s = _combine(ys, dest, x1, gates, p_all, w_ple_gate[0], w_ple_proj[0].astype(BF16),
                        row2(ln2_g[0]), row2(ln2_b[0]), row2(ln3_g[0]), row2(ln3_b[0]), alpha, tm=128,
                        n_prompt=n_p)

    return (y_p.reshape(bp, sp, d),
            y_s.reshape(bs, ss, d),
            conv_tail[:, SUBLANES - (CONV_W - 1):, :][None],
            rec_p[None],
            conv_s.reshape(bs, CONV_W - 1, d_conv)[None],
            rec_s[None])
```

```python
import functools

import numpy as np
import jax
import jax.numpy as jnp
from jax import lax
from jax.experimental import pallas as pl
from jax.experimental.pallas import tpu as pltpu

F32 = jnp.float32
BF16 = jnp.bfloat16
HIGHEST = lax.Precision.HIGHEST

CONV_W = 3
N_HEADS = 8
HEAD_K = 128
HEAD_V = 128
TOP_K = 4
SWIGLU_LIMIT = 7.0
SWIGLU_ALPHA = 1.702
LN_EPS = 1e-5
RMS_EPS = 1e-6

LANES = 128
SUBLANES = 8
VMEM_PHYSICAL_BYTES = 64 * 1024 * 1024
DMA_THREADS = 2

CHUNK = 128
SAMPLE_BLOCK = 16
ROW_BLOCK = 128
MAX_BLOCK = 512
ITEM_ROWS = 1280
FF_TILE = 512
NEG_BIG = -1e30


def _cparams(sem, vmem_mb):
    return pltpu.CompilerParams(dimension_semantics=sem, vmem_limit_bytes=vmem_mb * 1024 * 1024)


def _layer_norm(x, g, b):
    mu = jnp.mean(x, axis=-1, keepdims=True)
    xc = x - mu
    var = jnp.mean(xc * xc, axis=-1, keepdims=True)
    return xc * lax.rsqrt(var + LN_EPS) * g + b


def _sigmoid(x):
    return 1.0 / (1.0 + jnp.exp(-x))


def _split3(x, axis):
    p1 = x.astype(BF16)
    r1 = x - p1.astype(F32)
    p2 = r1.astype(BF16)
    p3 = (r1 - p2.astype(F32)).astype(BF16)
    return jnp.concatenate([p1, p2, p3], axis=axis)


def _ln_in_kernel(x_ref, g_ref, b_ref, o_ref):
    o_ref[...] = _layer_norm(x_ref[...], g_ref[...], b_ref[...]).astype(BF16)


def _ln_in(x, g, b, tm):
    n, d = x.shape
    return pl.pallas_call(
        _ln_in_kernel,
        grid=(n // tm,),
        in_specs=[pl.BlockSpec((tm, d), lambda i: (i, 0)),
                  pl.BlockSpec((1, d), lambda i: (0, 0)),
                  pl.BlockSpec((1, d), lambda i: (0, 0))],
        out_specs=pl.BlockSpec((tm, d), lambda i: (i, 0)),
        out_shape=jax.ShapeDtypeStruct((n, d), BF16),
        compiler_params=_cparams(("parallel",), 40),
        name="ln_in",
    )(x, g, b)


def _matmul_kernel(x_ref, w_ref, o_ref, wb_ref):
    @pl.when(pl.program_id(1) == 0)
    def _():
        wb_ref[...] = w_ref[...].astype(BF16)

    o_ref[...] = jnp.dot(x_ref[...], wb_ref[...], preferred_element_type=F32)


def _in_proj(xn, w, tm, tn):
    n, d = xn.shape
    d_in = w.shape[1]
    return pl.pallas_call(
        _matmul_kernel,
        grid=(d_in // tn, n // tm),
        in_specs=[pl.BlockSpec((tm, d), lambda j, i: (i, 0)),
                  pl.BlockSpec((d, tn), lambda j, i: (0, j))],
        out_specs=pl.BlockSpec((tm, tn), lambda j, i: (i, j)),
        out_shape=jax.ShapeDtypeStruct((n, d_in), F32),
        scratch_shapes=[pltpu.VMEM((d, tn), BF16)],
        compiler_params=_cparams(("arbitrary", "arbitrary"), 48),
        name="in_proj",
    )(xn, w)


def _in_proj_sample_kernel(x_ref, g_ref, b_ref, w_ref, o_ref):
    xn = _layer_norm(x_ref[...], g_ref[...], b_ref[...])
    o_ref[...] = jnp.dot(xn, w_ref[...], precision=HIGHEST, preferred_element_type=F32)


def _in_proj_sample(x, g, b, w, tn):
    n, d = x.shape
    d_in = w.shape[1]
    return pl.pallas_call(
        _in_proj_sample_kernel,
        grid=(d_in // tn,),
        in_specs=[pl.BlockSpec((n, d), lambda j: (0, 0)),
                  pl.BlockSpec((1, d), lambda j: (0, 0)),
                  pl.BlockSpec((1, d), lambda j: (0, 0)),
                  pl.BlockSpec((d, tn), lambda j: (0, j))],
        out_specs=pl.BlockSpec((n, tn), lambda j: (0, j)),
        out_shape=jax.ShapeDtypeStruct((n, d_in), F32),
        compiler_params=_cparams(("parallel",), 40),
        name="in_proj_sample",
    )(x, g, b, w)


def _forget_gates(fz, lb):
    e = jnp.exp(-jnp.abs(fz))
    r = 1.0 / (1.0 + e)
    er = e * r
    pos = fz >= 0
    sig_p = jnp.where(pos, r, er)
    sig_n = jnp.where(pos, er, r)
    oml = 1.0 - lb
    return lb + oml * sig_p, oml * sig_n


def _chunk_matrices(c):
    t = np.arange(c)[:, None]
    j = np.arange(c)[None, :]
    mats = [(j <= t), (j > t)]
    blk = c
    while blk >= 2:
        half = blk // 2
        mid = (t // blk) * blk + half
        second = (t % blk) >= half
        m_q = (j >= mid) & (j <= t)
        m_k = (j > t) & (j < mid)
        mats.append(np.where(second, m_q, m_k))
        blk = half
    return np.concatenate(mats, axis=0).astype(np.float32)


def _mix_prompt_kernel(proj_ref, convw_ref, lb_ref, rmsg_ref, cmat_ref,
                       mix_ref, convst_ref, recst_ref, s_ref, carry_ref, *, d_conv):
    c = CHUNK
    tb = pl.program_id(1)
    n_tb = pl.num_programs(1)

    @pl.when(tb == 0)
    def _():
        s_ref[...] = jnp.zeros_like(s_ref)
        carry_ref[...] = jnp.zeros_like(carry_ref)

    u = proj_ref[:, 0:d_conv] * proj_ref[:, 2 * d_conv:3 * d_conv]
    row = lax.broadcasted_iota(jnp.int32, u.shape, 0)
    prev1 = carry_ref[SUBLANES - 1:SUBLANES, :]
    prev2 = carry_ref[SUBLANES - 2:SUBLANES - 1, :]
    u1 = jnp.where(row == 0, prev1, pltpu.roll(u, 1, 0))
    u2 = jnp.where(row == 0, prev2, jnp.where(row == 1, prev1, pltpu.roll(u, 2, 0)))
    y = convw_ref[0:1, :] * u2 + convw_ref[1:2, :] * u1 + convw_ref[2:3, :] * u
    mix_ref[:, 0:d_conv] = (proj_ref[:, d_conv:2 * d_conv] * y).astype(BF16)
    carry_ref[...] = u[c - SUBLANES:c, :]
    convst_ref[0] = u[c - SUBLANES:c, :]

    o0 = 3 * d_conv
    d_rec = N_HEADS * HEAD_K
    q = proj_ref[:, o0:o0 + d_rec]
    fz = proj_ref[:, o0 + d_rec:o0 + 2 * d_rec]
    v = proj_ref[:, o0 + 2 * d_rec:o0 + 3 * d_rec]
    g = proj_ref[:, o0 + 3 * d_rec:o0 + 4 * d_rec]
    f, kk = _forget_gates(fz, lb_ref[...])
    contract0 = (((0,), (0,)), ((), ()))
    logf3 = _split3(jnp.log(f), axis=0)
    ex = jnp.dot(cmat_ref[...], logf3, preferred_element_type=F32)
    b_cum = ex[0:c]
    d_end = ex[c:2 * c]
    n_lev = cmat_ref.shape[0] // c - 2
    b_cols = lax.dot_general(logf3, jnp.ones((3 * c, HEAD_V), BF16), contract0,
                             preferred_element_type=F32)

    trow = lax.broadcasted_iota(jnp.int32, (c, c), 0)
    tcol = lax.broadcasted_iota(jnp.int32, (c, c), 1)
    prow = lax.broadcasted_iota(jnp.int32, (c, HEAD_K), 0)
    contract1 = (((1,), (1,)), ((), ()))

    for h in range(N_HEADS):
        sl = slice(h * HEAD_K, (h + 1) * HEAD_K)
        qh, kh, vh = q[:, sl], kk[:, sl], v[:, sl]
        vb = vh.astype(BF16)
        s_old = s_ref[h]
        o = jnp.dot((qh * jnp.exp(b_cum[:, sl])).astype(BF16), s_old.astype(BF16),
                    preferred_element_type=F32)
        sc = jnp.zeros((c, c), F32)
        for lev in range(n_lev):
            blk = c >> lev
            sh = blk.bit_length() - 1
            dl = jnp.exp(ex[(2 + lev) * c:(3 + lev) * c, sl])
            second = (prow & (blk - 1)) >= (blk // 2)
            qt = jnp.where(second, qh * dl, 0.0).astype(BF16)
            kt = jnp.where(second, 0.0, kh * dl).astype(BF16)
            s_l = lax.dot_general(qt, kt, contract1, preferred_element_type=F32)
            sc = sc + jnp.where((trow >> sh) == (tcol >> sh), s_l, 0.0)
        o = o + jnp.dot(sc.astype(BF16), vb, preferred_element_type=F32)
        o = o + jnp.sum(qh * kh, axis=1, keepdims=True) * vh
        khat = (kh * jnp.exp(d_end[:, sl])).astype(BF16)
        upd = lax.dot_general(khat, vb, contract0, preferred_element_type=F32)
        s_ref[h] = jnp.exp(b_cols[sl, :]) * s_old + upd
        on = o * lax.rsqrt(jnp.mean(o * o, axis=1, keepdims=True) + RMS_EPS) * rmsg_ref[:, sl]
        gh = g[:, sl]
        mix_ref[:, d_conv + h * HEAD_V:d_conv + (h + 1) * HEAD_V] = (
            on * (gh * _sigmoid(gh))).astype(BF16)

    @pl.when(tb == n_tb - 1)
    def _():
        recst_ref[0] = s_ref[...]


def _mix_prompt(proj, conv_w, lb, rms_g, n_tok, bsz, seq, d_conv):
    d_in = proj.shape[1]
    d_mix = d_conv + N_HEADS * HEAD_V
    n_tb = seq // CHUNK
    cmat = jnp.asarray(np.tile(_chunk_matrices(CHUNK), (1, 3)), dtype=BF16)
    kern = functools.partial(_mix_prompt_kernel, d_conv=d_conv)
    return pl.pallas_call(
        kern,
        grid=(bsz, n_tb),
        in_specs=[pl.BlockSpec((CHUNK, d_in), lambda b, t: (b * n_tb + t, 0)),
                  pl.BlockSpec((CONV_W, d_conv), lambda b, t: (0, 0)),
                  pl.BlockSpec((1, N_HEADS * HEAD_K), lambda b, t: (0, 0)),
                  pl.BlockSpec((1, N_HEADS * HEAD_V), lambda b, t: (0, 0)),
                  pl.BlockSpec(cmat.shape, lambda b, t: (0, 0))],
        out_specs=[pl.BlockSpec((CHUNK, d_mix), lambda b, t: (b * n_tb + t, 0)),
                   pl.BlockSpec((1, SUBLANES, d_conv), lambda b, t: (b, 0, 0)),
                   pl.BlockSpec((1, N_HEADS, HEAD_K, HEAD_V), lambda b, t: (b, 0, 0, 0))],
        out_shape=[jax.ShapeDtypeStruct((n_tok, d_mix), BF16),
                   jax.ShapeDtypeStruct((bsz, SUBLANES, d_conv), F32),
                   jax.ShapeDtypeStruct((bsz, N_HEADS, HEAD_K, HEAD_V), F32)],
        scratch_shapes=[pltpu.VMEM((N_HEADS, HEAD_K, HEAD_V), F32),
                        pltpu.VMEM((SUBLANES, d_conv), F32)],
        compiler_params=_cparams(("parallel", "arbitrary"), 40),
        name="mix_prompt",
    )(proj, conv_w, lb, rms_g, cmat)


def _mix_sample_kernel(proj_ref, cst_ref, rst_ref, convw_ref, lb_ref, rmsg_ref, sel_ref,
                       mix_ref, cnew_ref, rnew_ref, *, d_conv):
    nb = SAMPLE_BLOCK
    u = proj_ref[:, 0:d_conv] * proj_ref[:, 2 * d_conv:3 * d_conv]
    buf0 = cst_ref[:, 0:d_conv]
    buf1 = cst_ref[:, d_conv:2 * d_conv]
    y = convw_ref[0:1, :] * buf0 + convw_ref[1:2, :] * buf1 + convw_ref[2:3, :] * u
    mix_ref[:, 0:d_conv] = proj_ref[:, d_conv:2 * d_conv] * y
    cnew_ref[:, 0:d_conv] = buf1
    cnew_ref[:, d_conv:2 * d_conv] = u

    o0 = 3 * d_conv
    d_rec = N_HEADS * HEAD_K
    q = proj_ref[:, o0:o0 + d_rec]
    fz = proj_ref[:, o0 + d_rec:o0 + 2 * d_rec]
    v = proj_ref[:, o0 + 2 * d_rec:o0 + 3 * d_rec]
    g = proj_ref[:, o0 + 3 * d_rec:o0 + 4 * d_rec]
    f, kk = _forget_gates(fz, lb_ref[...])
    contract0 = (((0,), (0,)), ((), ()))
    sel = sel_ref[...]
    row = lax.broadcasted_iota(jnp.int32, (nb, HEAD_V), 0)

    def columns(a):
        return lax.dot_general(_split3(a, axis=0), sel, contract0, preferred_element_type=F32)

    for h in range(N_HEADS):
        sl = slice(h * HEAD_K, (h + 1) * HEAD_K)
        f_c, k_c, q_c = columns(f[:, sl]), columns(kk[:, sl]), columns(q[:, sl])
        o = jnp.zeros((nb, HEAD_V), F32)
        for n in range(nb):
            nl = slice(n * HEAD_V, (n + 1) * HEAD_V)
            s_new = f_c[:, nl] * rst_ref[n, h] + k_c[:, nl] * v[n:n + 1, sl]
            rnew_ref[n, h] = s_new
            o_row = jnp.sum(q_c[:, nl] * s_new, axis=0, keepdims=True)
            o = jnp.where(row == n, o_row, o)
        on = o * lax.rsqrt(jnp.mean(o * o, axis=1, keepdims=True) + RMS_EPS) * rmsg_ref[:, sl]
        gh = g[:, sl]
        mix_ref[:, d_conv + h * HEAD_V:d_conv + (h + 1) * HEAD_V] = on * (gh * _sigmoid(gh))


def _mix_sample(proj, conv_state, rec_state, conv_w, lb, rms_g, d_conv):
    n_seq = conv_state.shape[0]
    d_in = proj.shape[1]
    d_mix = d_conv + N_HEADS * HEAD_V
    nb = SAMPLE_BLOCK
    sel = jnp.asarray(np.tile(np.kron(np.eye(nb), np.ones((1, HEAD_V))), (3, 1)), dtype=BF16)
    kern = functools.partial(_mix_sample_kernel, d_conv=d_conv)
    return pl.pallas_call(
        kern,
        grid=(n_seq // nb,),
        in_specs=[pl.BlockSpec((nb, d_in), lambda i: (i, 0)),
                  pl.BlockSpec((nb, 2 * d_conv), lambda i: (i, 0)),
                  pl.BlockSpec((nb, N_HEADS, HEAD_K, HEAD_V), lambda i: (i, 0, 0, 0)),
                  pl.BlockSpec((CONV_W, d_conv), lambda i: (0, 0)),
                  pl.BlockSpec((1, N_HEADS * HEAD_K), lambda i: (0, 0)),
                  pl.BlockSpec((1, N_HEADS * HEAD_V), lambda i: (0, 0)),
                  pl.BlockSpec(sel.shape, lambda i: (0, 0))],
        out_specs=[pl.BlockSpec((nb, d_mix), lambda i: (i, 0)),
                   pl.BlockSpec((nb, 2 * d_conv), lambda i: (i, 0)),
                   pl.BlockSpec((nb, N_HEADS, HEAD_K, HEAD_V), lambda i: (i, 0, 0, 0))],
        out_shape=[jax.ShapeDtypeStruct((n_seq, d_mix), F32),
                   jax.ShapeDtypeStruct((n_seq, 2 * d_conv), F32),
                   jax.ShapeDtypeStruct(rec_state.shape, F32)],
        compiler_params=_cparams(("parallel",), 52),
        name="mix_sample",
    )(proj, conv_state, rec_state, conv_w, lb, rms_g, sel)


def _post_mix_kernel(mixp_ref, mixs_ref, xp_ref, xs_ref, woutf_ref, ling_ref, linb_ref,
                     l1g_ref, l1b_ref, wr_ref, wr2_ref, br_ref,
                     x1_ref, x1p_ref, route_ref, gate_ref, cnt_ref, run_ref, h_ref, lg_ref, woutb_ref,
                     *, alpha, n_pt):
    i = pl.program_id(0)

    @pl.when(i == 0)
    def _():
        run_ref[...] = jnp.zeros_like(run_ref)
        woutb_ref[...] = woutf_ref[...].astype(BF16)

    @pl.when(i < n_pt)
    def _():
        h_ref[...] = jnp.dot(mixp_ref[...], woutb_ref[...], preferred_element_type=F32)

    @pl.when(i >= n_pt)
    def _():
        h_ref[...] = jnp.dot(mixs_ref[...], woutf_ref[...], precision=HIGHEST,
                             preferred_element_type=F32)

    x = jnp.where(i < n_pt, xp_ref[...], xs_ref[...])
    xn = _layer_norm(x, ling_ref[...], linb_ref[...])
    x1 = _layer_norm(alpha * xn + h_ref[...], l1g_ref[...], l1b_ref[...])
    x1_ref[...] = x1
    half = x1.shape[1] // 2
    bits = pltpu.bitcast(x1.astype(BF16).astype(F32), jnp.uint32)
    x1p_ref[...] = (bits[:, half:] & jnp.uint32(0xFFFF0000)) | (bits[:, :half] >> 16)

    tm = x1.shape[0]

    @pl.when(i < n_pt)
    def _():
        xh = x1.astype(BF16)
        xl = (x1 - xh.astype(F32)).astype(BF16)
        pr = jnp.dot(jnp.concatenate([xh, xl], axis=0), wr2_ref[...], preferred_element_type=F32)
        lg_ref[...] = (pr[0:tm, 0:LANES] + pr[0:tm, LANES:2 * LANES]
                       + pr[tm:2 * tm, 0:LANES] + pr[tm:2 * tm, LANES:2 * LANES])

    @pl.when(i >= n_pt)
    def _():
        lg_ref[...] = jnp.dot(x1, wr_ref[...], precision=HIGHEST, preferred_element_type=F32)

    logits = lg_ref[...] + br_ref[...]
    lane = lax.broadcasted_iota(jnp.int32, (tm, LANES), 1)
    lane_f = lane.astype(F32)
    work = logits
    vals, idxs = [], []
    for _ in range(TOP_K):
        m = jnp.max(work, axis=1, keepdims=True)
        ix = jnp.min(jnp.where(work == m, lane_f, float(LANES)), axis=1, keepdims=True)
        vals.append(m)
        idxs.append(ix)
        work = jnp.where(lane_f == ix, NEG_BIG, work)
    ex = [jnp.exp(vv - vals[0]) for vv in vals]
    den = ex[0] + ex[1] + ex[2] + ex[3]
    onehots = [(lane_f == ix).astype(F32) for ix in idxs]
    oh = onehots[0] + onehots[1] + onehots[2] + onehots[3]
    tr = lax.broadcasted_iota(jnp.int32, (tm, tm), 0)
    tc = lax.broadcasted_iota(jnp.int32, (tm, tm), 1)
    before = jnp.dot((tc < tr).astype(BF16), oh.astype(BF16), preferred_element_type=F32)
    pos = before + run_ref[...]
    route = jnp.zeros((tm, LANES), F32)
    gates = jnp.zeros((tm, LANES), F32)
    for k in range(TOP_K):
        rank = jnp.sum(onehots[k] * pos, axis=1, keepdims=True)
        route = jnp.where(lane == k, idxs[k], route)
        route = jnp.where(lane == TOP_K + k, rank, route)
        gates = jnp.where(lane == k, ex[k] / den, gates)
    route_ref[...] = route.T[0:2 * TOP_K, :].astype(jnp.int32)
    gate_ref[...] = gates
    run_ref[...] = run_ref[...] + jnp.sum(oh, axis=0, keepdims=True)
    cnt_ref[...] = run_ref[...].astype(jnp.int32)


def _post_mix(mix_p, mix_s, x_p, x_s, w_out_f, lin_g, lin_b, l1_g, l1_b, w_r, w_r2, b_r, alpha, tm):
    d = x_p.shape[1]
    n = x_p.shape[0] + x_s.shape[0]
    d_mix = mix_p.shape[1]
    n_pt = mix_p.shape[0] // tm
    assert mix_p.shape[0] % tm == 0 and mix_s.shape[0] % tm == 0
    row = lambda i: (i, 0)
    fixed = lambda i: (0, 0)
    prompt_row = lambda i: (jnp.minimum(i, n_pt - 1), 0)
    sample_row = lambda i: (jnp.maximum(i - n_pt, 0), 0)
    once = pl.Buffered(1)
    kern = functools.partial(_post_mix_kernel, alpha=alpha, n_pt=n_pt)
    return pl.pallas_call(
        kern,
        grid=(n // tm,),
        in_specs=[pl.BlockSpec((tm, d_mix), prompt_row),
                  pl.BlockSpec((tm, d_mix), sample_row),
                  pl.BlockSpec((tm, d), prompt_row),
                  pl.BlockSpec((tm, d), sample_row),
                  pl.BlockSpec(w_out_f.shape, fixed, pipeline_mode=once),
                  pl.BlockSpec((1, d), fixed), pl.BlockSpec((1, d), fixed),
                  pl.BlockSpec((1, d), fixed), pl.BlockSpec((1, d), fixed),
                  pl.BlockSpec((d, LANES), fixed), pl.BlockSpec((d, 2 * LANES), fixed),
                  pl.BlockSpec((1, LANES), fixed)],
        out_specs=[pl.BlockSpec((tm, d), row),
                   pl.BlockSpec((tm, d // 2), row),
                   pl.BlockSpec((2 * TOP_K, tm), lambda i: (0, i)),
                   pl.BlockSpec((tm, LANES), row),
                   pl.BlockSpec((1, LANES), fixed)],
        out_shape=[jax.ShapeDtypeStruct((n, d), F32),
                   jax.ShapeDtypeStruct((n, d // 2), jnp.uint32),
                   jax.ShapeDtypeStruct((2 * TOP_K, n), jnp.int32),
                   jax.ShapeDtypeStruct((n, LANES), F32),
                   jax.ShapeDtypeStruct((1, LANES), jnp.int32)],
        scratch_shapes=[pltpu.VMEM((1, LANES), F32), pltpu.VMEM((tm, d), F32),
                        pltpu.VMEM((tm, LANES), F32), pltpu.VMEM(w_out_f.shape, BF16)],
        compiler_params=_cparams(("arbitrary",), 48),
        name="post_mix",
    )(mix_p, mix_s, x_p, x_s, w_out_f, lin_g, lin_b, l1_g, l1_b, w_r, w_r2, b_r)


def _dispatch_kernel(*refs, tm, n_experts):
    dest_refs = refs[:TOP_K]
    padlo_ref, padhi_ref, x_ref, xs_hbm, zero_ref, sem = refs[TOP_K:]
    i = pl.program_id(0)

    def row_copy(src, dst_row):
        return pltpu.make_async_copy(src, xs_hbm.at[pl.ds(dst_row, 1), :], sem)

    @pl.when(i == 0)
    def _():
        zero_ref[...] = jnp.zeros_like(zero_ref)

        def per_expert(e, carry):
            def start(r, c):
                row_copy(zero_ref.at[pl.ds(0, 1), :], r).start()
                return c

            def wait(r, c):
                row_copy(zero_ref.at[pl.ds(0, 1), :], r).wait()
                return c

            lax.fori_loop(padlo_ref[e], padhi_ref[e], start, 0)
            lax.fori_loop(padlo_ref[e], padhi_ref[e], wait, 0)
            return carry

        lax.fori_loop(0, n_experts, per_expert, 0)

        tail0 = padhi_ref[n_experts - 1]
        n_tail = (xs_hbm.shape[0] - tail0) // ROW_BLOCK

        def tail_copy(c):
            r0 = pl.multiple_of(tail0 + c * ROW_BLOCK, ROW_BLOCK)
            return pltpu.make_async_copy(zero_ref, xs_hbm.at[pl.ds(r0, ROW_BLOCK), :], sem)

        def tail_start(c, carry):
            tail_copy(c).start()
            return carry

        def tail_wait(c, carry):
            tail_copy(c).wait()
            return carry

        lax.fori_loop(0, n_tail, tail_start, 0)
        lax.fori_loop(0, n_tail, tail_wait, 0)

    def start(t, c):
        src = x_ref.at[pl.ds(t, 1), :]
        for k in range(TOP_K):
            row_copy(src, dest_refs[k][t]).start(priority=k % DMA_THREADS)
        return c

    def wait(t, c):
        src = x_ref.at[pl.ds(t, 1), :]
        for k in range(TOP_K):
            row_copy(src, dest_refs[k][t]).wait()
        return c

    lax.fori_loop(0, tm, start, 0, unroll=4)
    lax.fori_loop(0, tm, wait, 0, unroll=4)


def _dispatch(x1p, dest, pad_lo, pad_hi, n_slots, tm):
    n, dh = x1p.shape
    n_experts = pad_lo.shape[0]
    kern = functools.partial(_dispatch_kernel, tm=tm, n_experts=n_experts)
    return pl.pallas_call(
        kern,
        grid=(n // tm,),
        in_specs=[pl.BlockSpec((tm,), lambda i: (i,), memory_space=pltpu.SMEM)] * TOP_K + [
                  pl.BlockSpec(memory_space=pltpu.SMEM),
                  pl.BlockSpec(memory_space=pltpu.SMEM),
                  pl.BlockSpec((tm, dh), lambda i: (i, 0))],
        out_specs=pl.BlockSpec(memory_space=pl.ANY),
        out_shape=jax.ShapeDtypeStruct((n_slots, dh), jnp.uint32),
        scratch_shapes=[pltpu.VMEM((ROW_BLOCK, dh), jnp.uint32), pltpu.SemaphoreType.DMA(())],
        compiler_params=_cparams(("arbitrary",), 32),
        name="dispatch",
    )(*dest, pad_lo, pad_hi, x1p)


def _expert_kernel(ie_ref, is_ref, ir_ref, xs_hbm, wg_ref, wu_ref, wd_ref, bg_ref, bu_ref, bd_ref,
                   ys_hbm, xbuf, ybuf, wgu_bf, wd_bf, sem_in, sem_out):
    i = pl.program_id(0)
    j = pl.program_id(1)
    n_j = pl.num_programs(1)
    rows = ir_ref[i]
    start = is_ref[i]
    tf = wg_ref.shape[1]
    b_row = ie_ref[i] * n_j + j
    bg = bg_ref[pl.ds(b_row, 1), :]
    bu = bu_ref[pl.ds(b_row, 1), :]
    bd = bd_ref[pl.ds(ie_ref[i], 1), :]

    def in_copy(r0, size):
        g0 = pl.multiple_of(start + r0, ROW_BLOCK)
        return pltpu.make_async_copy(xs_hbm.at[pl.ds(g0, size), :], xbuf.at[pl.ds(r0, size), :], sem_in)

    def out_copy(r0, size):
        g0 = pl.multiple_of(start + r0, ROW_BLOCK)
        return pltpu.make_async_copy(ybuf.at[pl.ds(r0, size), :], ys_hbm.at[pl.ds(g0, size), :], sem_out)

    def for_blocks(fn):
        n_big = rows // MAX_BLOCK

        def body(c, carry):
            fn(pl.multiple_of(c * MAX_BLOCK, MAX_BLOCK), MAX_BLOCK)
            return carry
        lax.fori_loop(0, n_big, body, 0)
        base = n_big * MAX_BLOCK
        size = MAX_BLOCK // 2
        while size >= ROW_BLOCK:
            has = (rows & size) != 0

            @pl.when(has)
            def _(base=base, size=size):
                fn(pl.multiple_of(base, ROW_BLOCK), size)
            base = base + jnp.where(has, size, 0)
            size //= 2

    @pl.when(rows > 0)
    def _():
        @pl.when(j == 0)
        def _():
            for_blocks(lambda r0, size: in_copy(r0, size).start())

        wgu_bf[:, 0:tf] = wg_ref[...].astype(BF16)
        wgu_bf[:, tf:2 * tf] = wu_ref[...].astype(BF16)
        wd_bf[...] = wd_ref[...].astype(BF16)

        @pl.when(j == 0)
        def _():
            for_blocks(lambda r0, size: in_copy(r0, size).wait())

        def block(r0, size, first, last):
            xu = xbuf[pl.ds(r0, size), :]
            lo = pltpu.bitcast(xu << 16, F32).astype(BF16)
            hi = pltpu.bitcast(xu & jnp.uint32(0xFFFF0000), F32).astype(BF16)
            x = jnp.concatenate([lo, hi], axis=1)
            gu = jnp.dot(x, wgu_bf[...], preferred_element_type=F32)
            gg = jnp.minimum(gu[:, 0:tf] + bg, SWIGLU_LIMIT)
            uu = jnp.clip(gu[:, tf:2 * tf] + bu, -SWIGLU_LIMIT, SWIGLU_LIMIT)
            hid = gg * _sigmoid(SWIGLU_ALPHA * gg) * (uu + 1.0)
            y = jnp.dot(hid.astype(BF16), wd_bf[...], preferred_element_type=F32)
            if not first:
                y = y + ybuf[pl.ds(r0, size), :]
            if last:
                y = y + bd
            ybuf[pl.ds(r0, size), :] = y
            if last:
                out_copy(r0, size).start()

        @pl.when(j == 0)
        def _():
            for_blocks(lambda r0, size: block(r0, size, True, False))

        @pl.when(jnp.logical_and(j > 0, j < n_j - 1))
        def _():
            for_blocks(lambda r0, size: block(r0, size, False, False))

        @pl.when(j == n_j - 1)
        def _():
            for_blocks(lambda r0, size: block(r0, size, False, True))
            for_blocks(lambda r0, size: out_copy(r0, size).wait())

    @pl.when(jnp.logical_and(i == pl.num_programs(0) - 1, j == n_j - 1))
    def _():
        tail0 = is_ref[pl.num_programs(0)]
        n_tail = (ys_hbm.shape[0] - tail0) // ROW_BLOCK
        ybuf[0:ROW_BLOCK, :] = jnp.zeros((ROW_BLOCK, ybuf.shape[1]), F32)

        def tail_copy(c):
            g0 = pl.multiple_of(tail0 + c * ROW_BLOCK, ROW_BLOCK)
            return pltpu.make_async_copy(ybuf.at[pl.ds(0, ROW_BLOCK), :],
                                         ys_hbm.at[pl.ds(g0, ROW_BLOCK), :], sem_out)

        def tail_start(c, carry):
            tail_copy(c).start()
            return carry

        def tail_wait(c, carry):
            tail_copy(c).wait()
            return carry

        lax.fori_loop(0, n_tail, tail_start, 0)
        lax.fori_loop(0, n_tail, tail_wait, 0)


def _experts(xs, item_e, item_start, item_rows, w_gate, w_up, w_down, b_gate, b_up, b_down):
    n_slots, dh = xs.shape
    n_exp, d, d_ff = w_gate.shape
    n_items = item_e.shape[0]
    n_j = d_ff // FF_TILE
    assert n_j >= 2 and d == 2 * dh

    def jj(i, j, ir):
        return jnp.where(ir[i] > 0, j, n_j - 1)

    grid_spec = pltpu.PrefetchScalarGridSpec(
        num_scalar_prefetch=3,
        grid=(n_items, n_j),
        in_specs=[pl.BlockSpec(memory_space=pl.ANY),
                  pl.BlockSpec((None, d, FF_TILE), lambda i, j, ie, is_, ir: (ie[i], 0, jj(i, j, ir))),
                  pl.BlockSpec((None, d, FF_TILE), lambda i, j, ie, is_, ir: (ie[i], 0, jj(i, j, ir))),
                  pl.BlockSpec((None, FF_TILE, d), lambda i, j, ie, is_, ir: (ie[i], jj(i, j, ir), 0)),
                  pl.BlockSpec((n_exp * n_j, FF_TILE), lambda i, j, ie, is_, ir: (0, 0)),
                  pl.BlockSpec((n_exp * n_j, FF_TILE), lambda i, j, ie, is_, ir: (0, 0)),
                  pl.BlockSpec((n_exp, d), lambda i, j, ie, is_, ir: (0, 0))],
        out_specs=pl.BlockSpec(memory_space=pl.ANY),
        scratch_shapes=[pltpu.VMEM((ITEM_ROWS, dh), jnp.uint32),
                        pltpu.VMEM((ITEM_ROWS, d), F32),
                        pltpu.VMEM((d, 2 * FF_TILE), BF16),
                        pltpu.VMEM((FF_TILE, d), BF16),
                        pltpu.SemaphoreType.DMA(()),
                        pltpu.SemaphoreType.DMA(())],
    )
    return pl.pallas_call(
        _expert_kernel,
        grid_spec=grid_spec,
        out_shape=jax.ShapeDtypeStruct((n_slots, d), F32),
        compiler_params=_cparams(("arbitrary", "arbitrary"), 58),
        name="experts",
    )(item_e, item_start, item_rows, xs, w_gate, w_up, w_down,
      b_gate.reshape(n_exp * n_j, FF_TILE), b_up.reshape(n_exp * n_j, FF_TILE), b_down)


def _combine_kernel(*refs, alpha, tm, n_pt):
    dcur_ref = refs[:TOP_K]
    dnext_ref = refs[TOP_K:2 * TOP_K]
    (ys_hbm, x1_ref, gate_ref, p_ref, wpg_ref, wpp_ref, l2g_ref, l2b_ref, l3g_ref, l3b_ref,
     op_ref, os_ref, gbuf_a, gbuf_b, wpgb_ref, sems) = refs[2 * TOP_K:]
    i = pl.program_id(0)
    n_i = pl.num_programs(0)

    @pl.when(i == 0)
    def _():
        wpgb_ref[...] = wpg_ref[...].astype(BF16)

    def row_copy(dref, t, k, buf, sem):
        return pltpu.make_async_copy(ys_hbm.at[pl.ds(dref[k][t], 1), :],
                                     buf.at[k, pl.ds(t, 1), :], sem)

    def gather_loop(dref, buf, sem, wait):
        def body(t, c):
            for k in range(TOP_K):
                cp = row_copy(dref, t, k, buf, sem)
                if wait:
                    cp.wait()
                else:
                    cp.start(priority=k % DMA_THREADS)
            return c
        lax.fori_loop(0, tm, body, 0, unroll=4)

    def step(cur, cur_sem, nxt, nxt_sem):
        @pl.when(i == 0)
        def _():
            gather_loop(dcur_ref, cur, cur_sem, False)

        gather_loop(dcur_ref, cur, cur_sem, True)

        for t in range(tm):
            for k in range(TOP_K):
                row_copy(dnext_ref, t, k, nxt, nxt_sem).start(priority=k % DMA_THREADS)

        x1 = x1_ref[...]
        lane = lax.broadcasted_iota(jnp.int32, gate_ref.shape, 1)
        gates = gate_ref[...]
        ff = jnp.zeros_like(x1)
        for k in range(TOP_K):
            gk = jnp.sum(jnp.where(lane == k, gates, 0.0), axis=1, keepdims=True)
            ff = ff + gk * cur[k]
        x2 = _layer_norm(alpha * x1 + ff, l2g_ref[...], l2b_ref[...])
        eg = _sigmoid(jnp.dot(x2.astype(BF16), wpgb_ref[...], preferred_element_type=F32))
        ep = jnp.dot(p_ref[...].astype(BF16), wpp_ref[...], preferred_element_type=F32)
        out = _layer_norm(alpha * x2 + eg * ep, l3g_ref[...], l3b_ref[...])

        @pl.when(i < n_pt)
        def _():
            op_ref[...] = out

        @pl.when(i >= n_pt)
        def _():
            os_ref[...] = out

        @pl.when(i == n_i - 1)
        def _():
            gather_loop(dnext_ref, nxt, nxt_sem, True)

    @pl.when(lax.rem(i, 2) == 0)
    def _():
        step(gbuf_a, sems.at[0], gbuf_b, sems.at[1])

    @pl.when(lax.rem(i, 2) == 1)
    def _():
        step(gbuf_b, sems.at[1], gbuf_a, sems.at[0])


def _combine(ys, dest, x1, gates, p, w_pg, w_pp, l2_g, l2_b, l3_g, l3_b, alpha, tm, n_prompt):
    n, d = x1.shape
    n_i = n // tm
    n_pt = n_prompt // tm
    assert n_prompt % tm == 0 and n % tm == 0
    row = lambda i: (i, 0)
    fixed = lambda i: (0, 0)
    kern = functools.partial(_combine_kernel, alpha=alpha, tm=tm, n_pt=n_pt)
    return pl.pallas_call(
        kern,
        grid=(n_i,),
        in_specs=[pl.BlockSpec((tm,), lambda i: (i,), memory_space=pltpu.SMEM)] * TOP_K + [
                  pl.BlockSpec((tm,), lambda i: (jnp.minimum(i + 1, n_i - 1),),
                               memory_space=pltpu.SMEM)] * TOP_K + [
                  pl.BlockSpec(memory_space=pl.ANY),
                  pl.BlockSpec((tm, d), row),
                  pl.BlockSpec((tm, LANES), row),
                  pl.BlockSpec((tm, p.shape[1]), row),
                  pl.BlockSpec(w_pg.shape, fixed, pipeline_mode=pl.Buffered(1)),
                  pl.BlockSpec(w_pp.shape, fixed),
                  pl.BlockSpec((1, d), fixed), pl.BlockSpec((1, d), fixed),
                  pl.BlockSpec((1, d), fixed), pl.BlockSpec((1, d), fixed)],
        out_specs=[pl.BlockSpec((tm, d), lambda i: (jnp.minimum(i, n_pt - 1), 0)),
                   pl.BlockSpec((tm, d), lambda i: (jnp.maximum(i - n_pt, 0), 0))],
        out_shape=[jax.ShapeDtypeStruct((n_prompt, d), F32),
                   jax.ShapeDtypeStruct((n - n_prompt, d), F32)],
        scratch_shapes=[pltpu.VMEM((TOP_K, tm, d), F32), pltpu.VMEM((TOP_K, tm, d), F32),
                        pltpu.VMEM(w_pg.shape, BF16), pltpu.SemaphoreType.DMA((2,))],
        compiler_params=_cparams(("arbitrary",), 48),
        name="combine",
    )(*dest, *dest, ys, x1, gates, p, w_pg, w_pp, l2_g, l2_b, l3_g, l3_b)


def _routing_tables(expert_idx, rank, counts, n_items):
    n_exp = counts.shape[0]
    padded = (counts + ROW_BLOCK - 1) // ROW_BLOCK * ROW_BLOCK
    pad_end = jnp.cumsum(padded)
    pad_start = pad_end - padded
    experts = jnp.arange(n_exp, dtype=expert_idx.dtype)[:, None, None]
    group0 = jnp.sum(jnp.where(expert_idx[None] == experts, pad_start[:, None, None], 0), axis=0)
    dest = (group0 + rank).astype(jnp.int32)
    dest = tuple(dest[k] for k in range(TOP_K))
    items_per = (padded + ITEM_ROWS - 1) // ITEM_ROWS
    item_end = jnp.cumsum(items_per)
    ids = jnp.arange(n_items, dtype=jnp.int32)
    total = item_end[-1]
    last_valid = jnp.maximum(total - 1, 0)
    eff = jnp.minimum(ids, last_valid)
    e_of = jnp.minimum(jnp.searchsorted(item_end, eff, side='right'), n_exp - 1).astype(jnp.int32)
    sub = eff - (item_end[e_of] - items_per[e_of])
    start = pad_start[e_of] + sub * ITEM_ROWS
    rows = jnp.clip(padded[e_of] - sub * ITEM_ROWS, 0, ITEM_ROWS)
    rows = jnp.where(ids < total, rows, 0)
    start = jnp.concatenate([start, pad_end[-1:]])
    return (dest, (pad_start + counts).astype(jnp.int32), pad_end.astype(jnp.int32),
            e_of, start.astype(jnp.int32), rows.astype(jnp.int32))


def kernel(x_prompt, x_sample, state_conv, state_rec, p_prompt, p_sample, ln_in_g, ln_in_b, w_in, conv_w,
           lb_theta, rms_g, w_out, ln1_g, ln1_b, w_router, b_router, w_gate, b_gate, w_up, b_up, w_down,
           b_down, ln2_g, ln2_b, w_ple_gate, w_ple_proj, ln3_g, ln3_b):
    bp, sp, d = x_prompt.shape
    bs, ss, _ = x_sample.shape
    depth = w_in.shape[0]
    assert depth == 1 and ss == 1 and sp % CHUNK == 0
    d_conv = state_conv.shape[-1]
    n_exp = w_router.shape[-1]
    n_p = bp * sp
    n = n_p + bs
    alpha = (2 * depth) ** 0.25
    row2 = lambda a: a.reshape(1, -1)

    lb = jnp.cumsum(jax.nn.softmax(lb_theta.astype(F32), axis=0), axis=0)[0]
    x_p = x_prompt.reshape(n_p, d)
    x_s = x_sample.reshape(bs, d)
    p_all = jnp.concatenate([p_prompt[0].reshape(n_p, -1), p_sample[0].reshape(bs, -1)], axis=0)

    xn_p = _ln_in(x_p, row2(ln_in_g), row2(ln_in_b), tm=512)
    proj_p = _in_proj(xn_p, w_in[0], tm=1024, tn=1024)
    proj_s = _in_proj_sample(x_s, row2(ln_in_g), row2(ln_in_b), w_in[0], tn=1024)

    mix_p, conv_tail, rec_p = _mix_prompt(proj_p, conv_w[0], row2(lb), row2(rms_g[0]), n_p, bp, sp, d_conv)
    mix_s, conv_s, rec_s = _mix_sample(proj_s, state_conv[0].reshape(bs, -1), state_rec[0], conv_w[0],
                                       row2(lb), row2(rms_g[0]), d_conv)

    w_r = jnp.zeros((d, LANES), F32).at[:, :n_exp].set(w_router[0])
    b_r = jnp.full((1, LANES), NEG_BIG, F32).at[0, :n_exp].set(b_router[0])
    w_r_hi = w_r.astype(BF16)
    w_r2 = jnp.concatenate([w_r_hi, (w_r - w_r_hi.astype(F32)).astype(BF16)], axis=1)
    x1, x1p, route, gates, counts = _post_mix(
        mix_p, mix_s, x_p, x_s, w_out[0], row2(ln_in_g), row2(ln_in_b),
        row2(ln1_g[0]), row2(ln1_b[0]), w_r, w_r2, b_r, alpha, tm=128)

    n_slots = n * TOP_K + n_exp * ROW_BLOCK
    n_items = n_exp + n_slots // ITEM_ROWS
    dest, pad_lo, pad_hi, item_e, item_start, item_rows = _routing_tables(
        route[0:TOP_K], route[TOP_K:2 * TOP_K], counts[0, :n_exp], n_items)

    xs = _dispatch(x1p, dest, pad_lo, pad_hi, n_slots, tm=128)
    ys = _experts(xs, item_e, item_start, item_rows, w_gate[0], w_up[0], w_down[0],
                  b_gate[0], b_up[0], b_down[0])
    y_p, y_s = _combine(ys, dest, x1, gates, p_all, w_ple_gate[0], w_ple_proj[0].astype(BF16),
                        row2(ln2_g[0]), row2(ln2_b[0]), row2(ln3_g[0]), row2(ln3_b[0]), alpha, tm=128,
                        n_prompt=n_p)

    return (y_p.reshape(bp, sp, d),
            y_s.reshape(bs, ss, d),
            conv_tail[:, SUBLANES - (CONV_W - 1):, :][None],
            rec_p[None],
            conv_s.reshape(bs, CONV_W - 1, d_conv)[None],
            rec_s[None])
```

```python
import functools

import numpy as np
import jax
import jax.numpy as jnp
from jax import lax
from jax.experimental import pallas as pl
from jax.experimental.pallas import tpu as pltpu

F32 = jnp.float32
BF16 = jnp.bfloat16
HIGHEST = lax.Precision.HIGHEST

CONV_W = 3
N_HEADS = 8
HEAD_K = 128
HEAD_V = 128
TOP_K = 4
SWIGLU_LIMIT = 7.0
SWIGLU_ALPHA = 1.702
LN_EPS = 1e-5
RMS_EPS = 1e-6

LANES = 128
SUBLANES = 8
VMEM_PHYSICAL_BYTES = 64 * 1024 * 1024
DMA_THREADS = 2

CHUNK = 128
SAMPLE_BLOCK = 16
ROW_BLOCK = 128
MAX_BLOCK = 512
ITEM_ROWS = 1280
FF_TILE = 512
NEG_BIG = -1e30


def _cparams(sem, vmem_mb):
    return pltpu.CompilerParams(dimension_semantics=sem, vmem_limit_bytes=vmem_mb * 1024 * 1024)


def _layer_norm(x, g, b):
    mu = jnp.mean(x, axis=-1, keepdims=True)
    xc = x - mu
    var = jnp.mean(xc * xc, axis=-1, keepdims=True)
    return xc * lax.rsqrt(var + LN_EPS) * g + b


def _sigmoid(x):
    return 1.0 / (1.0 + jnp.exp(-x))


def _pack_bf16_pairs(x):
    half = x.shape[1] // 2
    bits = pltpu.bitcast(x.astype(BF16).astype(F32), jnp.uint32)
    return (bits[:, half:] & jnp.uint32(0xFFFF0000)) | (bits[:, :half] >> 16)


def _unpack_bf16_pairs(u):
    return (pltpu.bitcast(u << 16, F32), pltpu.bitcast(u & jnp.uint32(0xFFFF0000), F32))


def _split3(x, axis):
    p1 = x.astype(BF16)
    r1 = x - p1.astype(F32)
    p2 = r1.astype(BF16)
    p3 = (r1 - p2.astype(F32)).astype(BF16)
    return jnp.concatenate([p1, p2, p3], axis=axis)


def _ln_in_kernel(x_ref, g_ref, b_ref, o_ref):
    o_ref[...] = _layer_norm(x_ref[...], g_ref[...], b_ref[...]).astype(BF16)


def _ln_in(x, g, b, tm):
    n, d = x.shape
    return pl.pallas_call(
        _ln_in_kernel,
        grid=(n // tm,),
        in_specs=[pl.BlockSpec((tm, d), lambda i: (i, 0)),
                  pl.BlockSpec((1, d), lambda i: (0, 0)),
                  pl.BlockSpec((1, d), lambda i: (0, 0))],
        out_specs=pl.BlockSpec((tm, d), lambda i: (i, 0)),
        out_shape=jax.ShapeDtypeStruct((n, d), BF16),
        compiler_params=_cparams(("parallel",), 40),
        name="ln_in",
    )(x, g, b)


def _matmul_kernel(x_ref, w_ref, o_ref, wb_ref):
    @pl.when(pl.program_id(1) == 0)
    def _():
        wb_ref[...] = w_ref[...].astype(BF16)

    o_ref[...] = jnp.dot(x_ref[...], wb_ref[...], preferred_element_type=F32)


def _in_proj(xn, w, tm, tn):
    n, d = xn.shape
    d_in = w.shape[1]
    return pl.pallas_call(
        _matmul_kernel,
        grid=(d_in // tn, n // tm),
        in_specs=[pl.BlockSpec((tm, d), lambda j, i: (i, 0)),
                  pl.BlockSpec((d, tn), lambda j, i: (0, j))],
        out_specs=pl.BlockSpec((tm, tn), lambda j, i: (i, j)),
        out_shape=jax.ShapeDtypeStruct((n, d_in), F32),
        scratch_shapes=[pltpu.VMEM((d, tn), BF16)],
        compiler_params=_cparams(("arbitrary", "arbitrary"), 48),
        name="in_proj",
    )(xn, w)


def _in_proj_sample_kernel(x_ref, g_ref, b_ref, w_ref, o_ref):
    xn = _layer_norm(x_ref[...], g_ref[...], b_ref[...])
    o_ref[...] = jnp.dot(xn, w_ref[...], precision=HIGHEST, preferred_element_type=F32)


def _in_proj_sample(x, g, b, w, tn):
    n, d = x.shape
    d_in = w.shape[1]
    return pl.pallas_call(
        _in_proj_sample_kernel,
        grid=(d_in // tn,),
        in_specs=[pl.BlockSpec((n, d), lambda j: (0, 0)),
                  pl.BlockSpec((1, d), lambda j: (0, 0)),
                  pl.BlockSpec((1, d), lambda j: (0, 0)),
                  pl.BlockSpec((d, tn), lambda j: (0, j))],
        out_specs=pl.BlockSpec((n, tn), lambda j: (0, j)),
        out_shape=jax.ShapeDtypeStruct((n, d_in), F32),
        compiler_params=_cparams(("parallel",), 40),
        name="in_proj_sample",
    )(x, g, b, w)


def _forget_gates(fz, lb):
    e = jnp.exp(-jnp.abs(fz))
    r = 1.0 / (1.0 + e)
    er = e * r
    pos = fz >= 0
    sig_p = jnp.where(pos, r, er)
    sig_n = jnp.where(pos, er, r)
    oml = 1.0 - lb
    return lb + oml * sig_p, oml * sig_n


def _chunk_matrices(c):
    t = np.arange(c)[:, None]
    j = np.arange(c)[None, :]
    mats = [(j <= t), (j > t)]
    blk = c
    while blk >= 2:
        half = blk // 2
        mid = (t // blk) * blk + half
        second = (t % blk) >= half
        m_q = (j >= mid) & (j <= t)
        m_k = (j > t) & (j < mid)
        mats.append(np.where(second, m_q, m_k))
        blk = half
    return np.concatenate(mats, axis=0).astype(np.float32)


def _mix_prompt_kernel(proj_ref, convw_ref, lb_ref, rmsg_ref, cmat_ref,
                       mix_ref, convst_ref, recst_ref, s_ref, carry_ref, *, d_conv):
    c = CHUNK
    tb = pl.program_id(1)
    n_tb = pl.num_programs(1)

    @pl.when(tb == 0)
    def _():
        s_ref[...] = jnp.zeros_like(s_ref)
        carry_ref[...] = jnp.zeros_like(carry_ref)

    u = proj_ref[:, 0:d_conv] * proj_ref[:, 2 * d_conv:3 * d_conv]
    row = lax.broadcasted_iota(jnp.int32, u.shape, 0)
    prev1 = carry_ref[SUBLANES - 1:SUBLANES, :]
    prev2 = carry_ref[SUBLANES - 2:SUBLANES - 1, :]
    u1 = jnp.where(row == 0, prev1, pltpu.roll(u, 1, 0))
    u2 = jnp.where(row == 0, prev2, jnp.where(row == 1, prev1, pltpu.roll(u, 2, 0)))
    y = convw_ref[0:1, :] * u2 + convw_ref[1:2, :] * u1 + convw_ref[2:3, :] * u
    mix_ref[:, 0:d_conv] = (proj_ref[:, d_conv:2 * d_conv] * y).astype(BF16)
    carry_ref[...] = u[c - SUBLANES:c, :]
    convst_ref[0] = u[c - SUBLANES:c, :]

    o0 = 3 * d_conv
    d_rec = N_HEADS * HEAD_K
    q = proj_ref[:, o0:o0 + d_rec]
    fz = proj_ref[:, o0 + d_rec:o0 + 2 * d_rec]
    v = proj_ref[:, o0 + 2 * d_rec:o0 + 3 * d_rec]
    g = proj_ref[:, o0 + 3 * d_rec:o0 + 4 * d_rec]
    f, kk = _forget_gates(fz, lb_ref[...])
    contract0 = (((0,), (0,)), ((), ()))
    logf3 = _split3(jnp.log(f), axis=0)
    ex = jnp.dot(cmat_ref[...], logf3, preferred_element_type=F32)
    b_cum = ex[0:c]
    d_end = ex[c:2 * c]
    n_lev = cmat_ref.shape[0] // c - 2
    b_cols = lax.dot_general(logf3, jnp.ones((3 * c, HEAD_V), BF16), contract0,
                             preferred_element_type=F32)

    trow = lax.broadcasted_iota(jnp.int32, (c, c), 0)
    tcol = lax.broadcasted_iota(jnp.int32, (c, c), 1)
    prow = lax.broadcasted_iota(jnp.int32, (c, HEAD_K), 0)
    contract1 = (((1,), (1,)), ((), ()))

    for h in range(N_HEADS):
        sl = slice(h * HEAD_K, (h + 1) * HEAD_K)
        qh, kh, vh = q[:, sl], kk[:, sl], v[:, sl]
        vb = vh.astype(BF16)
        s_old = s_ref[h]
        o = jnp.dot((qh * jnp.exp(b_cum[:, sl])).astype(BF16), s_old.astype(BF16),
                    preferred_element_type=F32)
        sc = jnp.zeros((c, c), F32)
        for lev in range(n_lev):
            blk = c >> lev
            sh = blk.bit_length() - 1
            dl = jnp.exp(ex[(2 + lev) * c:(3 + lev) * c, sl])
            second = (prow & (blk - 1)) >= (blk // 2)
            qt = jnp.where(second, qh * dl, 0.0).astype(BF16)
            kt = jnp.where(second, 0.0, kh * dl).astype(BF16)
            s_l = lax.dot_general(qt, kt, contract1, preferred_element_type=F32)
            sc = sc + jnp.where((trow >> sh) == (tcol >> sh), s_l, 0.0)
        o = o + jnp.dot(sc.astype(BF16), vb, preferred_element_type=F32)
        o = o + jnp.sum(qh * kh, axis=1, keepdims=True) * vh
        khat = (kh * jnp.exp(d_end[:, sl])).astype(BF16)
        upd = lax.dot_general(khat, vb, contract0, preferred_element_type=F32)
        s_ref[h] = jnp.exp(b_cols[sl, :]) * s_old + upd
        on = o * lax.rsqrt(jnp.mean(o * o, axis=1, keepdims=True) + RMS_EPS) * rmsg_ref[:, sl]
        gh = g[:, sl]
        mix_ref[:, d_conv + h * HEAD_V:d_conv + (h + 1) * HEAD_V] = (
            on * (gh * _sigmoid(gh))).astype(BF16)

    @pl.when(tb == n_tb - 1)
    def _():
        recst_ref[0] = s_ref[...]


def _mix_prompt(proj, conv_w, lb, rms_g, n_tok, bsz, seq, d_conv):
    d_in = proj.shape[1]
    d_mix = d_conv + N_HEADS * HEAD_V
    n_tb = seq // CHUNK
    cmat = jnp.asarray(np.tile(_chunk_matrices(CHUNK), (1, 3)), dtype=BF16)
    kern = functools.partial(_mix_prompt_kernel, d_conv=d_conv)
    return pl.pallas_call(
        kern,
        grid=(bsz, n_tb),
        in_specs=[pl.BlockSpec((CHUNK, d_in), lambda b, t: (b * n_tb + t, 0)),
                  pl.BlockSpec((CONV_W, d_conv), lambda b, t: (0, 0)),
                  pl.BlockSpec((1, N_HEADS * HEAD_K), lambda b, t: (0, 0)),
                  pl.BlockSpec((1, N_HEADS * HEAD_V), lambda b, t: (0, 0)),
                  pl.BlockSpec(cmat.shape, lambda b, t: (0, 0))],
        out_specs=[pl.BlockSpec((CHUNK, d_mix), lambda b, t: (b * n_tb + t, 0)),
                   pl.BlockSpec((1, SUBLANES, d_conv), lambda b, t: (b, 0, 0)),
                   pl.BlockSpec((1, N_HEADS, HEAD_K, HEAD_V), lambda b, t: (b, 0, 0, 0))],
        out_shape=[jax.ShapeDtypeStruct((n_tok, d_mix), BF16),
                   jax.ShapeDtypeStruct((bsz, SUBLANES, d_conv), F32),
                   jax.ShapeDtypeStruct((bsz, N_HEADS, HEAD_K, HEAD_V), F32)],
        scratch_shapes=[pltpu.VMEM((N_HEADS, HEAD_K, HEAD_V), F32),
                        pltpu.VMEM((SUBLANES, d_conv), F32)],
        compiler_params=_cparams(("parallel", "arbitrary"), 40),
        name="mix_prompt",
    )(proj, conv_w, lb, rms_g, cmat)


def _mix_sample_kernel(proj_ref, cst_ref, rst_ref, convw_ref, lb_ref, rmsg_ref, sel_ref,
                       mix_ref, cnew_ref, rnew_ref, *, d_conv):
    nb = SAMPLE_BLOCK
    u = proj_ref[:, 0:d_conv] * proj_ref[:, 2 * d_conv:3 * d_conv]
    buf0 = cst_ref[:, 0:d_conv]
    buf1 = cst_ref[:, d_conv:2 * d_conv]
    y = convw_ref[0:1, :] * buf0 + convw_ref[1:2, :] * buf1 + convw_ref[2:3, :] * u
    mix_ref[:, 0:d_conv] = proj_ref[:, d_conv:2 * d_conv] * y
    cnew_ref[:, 0:d_conv] = buf1
    cnew_ref[:, d_conv:2 * d_conv] = u

    o0 = 3 * d_conv
    d_rec = N_HEADS * HEAD_K
    q = proj_ref[:, o0:o0 + d_rec]
    fz = proj_ref[:, o0 + d_rec:o0 + 2 * d_rec]
    v = proj_ref[:, o0 + 2 * d_rec:o0 + 3 * d_rec]
    g = proj_ref[:, o0 + 3 * d_rec:o0 + 4 * d_rec]
    f, kk = _forget_gates(fz, lb_ref[...])
    contract0 = (((0,), (0,)), ((), ()))
    sel = sel_ref[...]
    row = lax.broadcasted_iota(jnp.int32, (nb, HEAD_V), 0)

    def columns(a):
        return lax.dot_general(_split3(a, axis=0), sel, contract0, preferred_element_type=F32)

    for h in range(N_HEADS):
        sl = slice(h * HEAD_K, (h + 1) * HEAD_K)
        f_c, k_c, q_c = columns(f[:, sl]), columns(kk[:, sl]), columns(q[:, sl])
        o = jnp.zeros((nb, HEAD_V), F32)
        for n in range(nb):
            nl = slice(n * HEAD_V, (n + 1) * HEAD_V)
            s_new = f_c[:, nl] * rst_ref[n, h] + k_c[:, nl] * v[n:n + 1, sl]
            rnew_ref[n, h] = s_new
            o_row = jnp.sum(q_c[:, nl] * s_new, axis=0, keepdims=True)
            o = jnp.where(row == n, o_row, o)
        on = o * lax.rsqrt(jnp.mean(o * o, axis=1, keepdims=True) + RMS_EPS) * rmsg_ref[:, sl]
        gh = g[:, sl]
        mix_ref[:, d_conv + h * HEAD_V:d_conv + (h + 1) * HEAD_V] = on * (gh * _sigmoid(gh))


def _mix_sample(proj, conv_state, rec_state, conv_w, lb, rms_g, d_conv):
    n_seq = conv_state.shape[0]
    d_in = proj.shape[1]
    d_mix = d_conv + N_HEADS * HEAD_V
    nb = SAMPLE_BLOCK
    sel = jnp.asarray(np.tile(np.kron(np.eye(nb), np.ones((1, HEAD_V))), (3, 1)), dtype=BF16)
    kern = functools.partial(_mix_sample_kernel, d_conv=d_conv)
    return pl.pallas_call(
        kern,
        grid=(n_seq // nb,),
        in_specs=[pl.BlockSpec((nb, d_in), lambda i: (i, 0)),
                  pl.BlockSpec((nb, 2 * d_conv), lambda i: (i, 0)),
                  pl.BlockSpec((nb, N_HEADS, HEAD_K, HEAD_V), lambda i: (i, 0, 0, 0)),
                  pl.BlockSpec((CONV_W, d_conv), lambda i: (0, 0)),
                  pl.BlockSpec((1, N_HEADS * HEAD_K), lambda i: (0, 0)),
                  pl.BlockSpec((1, N_HEADS * HEAD_V), lambda i: (0, 0)),
                  pl.BlockSpec(sel.shape, lambda i: (0, 0))],
        out_specs=[pl.BlockSpec((nb, d_mix), lambda i: (i, 0)),
                   pl.BlockSpec((nb, 2 * d_conv), lambda i: (i, 0)),
                   pl.BlockSpec((nb, N_HEADS, HEAD_K, HEAD_V), lambda i: (i, 0, 0, 0))],
        out_shape=[jax.ShapeDtypeStruct((n_seq, d_mix), F32),
                   jax.ShapeDtypeStruct((n_seq, 2 * d_conv), F32),
                   jax.ShapeDtypeStruct(rec_state.shape, F32)],
        compiler_params=_cparams(("parallel",), 52),
        name="mix_sample",
    )(proj, conv_state, rec_state, conv_w, lb, rms_g, sel)


def _post_mix_kernel(mixp_ref, mixs_ref, xp_ref, xs_ref, woutf_ref, ling_ref, linb_ref,
                     l1g_ref, l1b_ref, wr_ref, wr2_ref, br_ref,
                     x1_ref, x1p_ref, route_ref, gate_ref, cnt_ref, run_ref, h_ref, lg_ref, woutb_ref,
                     *, alpha, n_pt):
    i = pl.program_id(0)

    @pl.when(i == 0)
    def _():
        run_ref[...] = jnp.zeros_like(run_ref)
        woutb_ref[...] = woutf_ref[...].astype(BF16)

    @pl.when(i < n_pt)
    def _():
        h_ref[...] = jnp.dot(mixp_ref[...], woutb_ref[...], preferred_element_type=F32)

    @pl.when(i >= n_pt)
    def _():
        h_ref[...] = jnp.dot(mixs_ref[...], woutf_ref[...], precision=HIGHEST,
                             preferred_element_type=F32)

    x = jnp.where(i < n_pt, xp_ref[...], xs_ref[...])
    xn = _layer_norm(x, ling_ref[...], linb_ref[...])
    x1 = _layer_norm(alpha * xn + h_ref[...], l1g_ref[...], l1b_ref[...])
    x1_ref[...] = x1
    x1p_ref[...] = _pack_bf16_pairs(x1)

    tm = x1.shape[0]

    @pl.when(i < n_pt)
    def _():
        xh = x1.astype(BF16)
        xl = (x1 - xh.astype(F32)).astype(BF16)
        pr = jnp.dot(jnp.concatenate([xh, xl], axis=0), wr2_ref[...], preferred_element_type=F32)
        lg_ref[...] = (pr[0:tm, 0:LANES] + pr[0:tm, LANES:2 * LANES]
                       + pr[tm:2 * tm, 0:LANES] + pr[tm:2 * tm, LANES:2 * LANES])

    @pl.when(i >= n_pt)
    def _():
        lg_ref[...] = jnp.dot(x1, wr_ref[...], precision=HIGHEST, preferred_element_type=F32)

    logits = lg_ref[...] + br_ref[...]
    lane = lax.broadcasted_iota(jnp.int32, (tm, LANES), 1)
    lane_f = lane.astype(F32)
    work = logits
    vals, idxs = [], []
    for _ in range(TOP_K):
        m = jnp.max(work, axis=1, keepdims=True)
        ix = jnp.min(jnp.where(work == m, lane_f, float(LANES)), axis=1, keepdims=True)
        vals.append(m)
        idxs.append(ix)
        work = jnp.where(lane_f == ix, NEG_BIG, work)
    ex = [jnp.exp(vv - vals[0]) for vv in vals]
    den = ex[0] + ex[1] + ex[2] + ex[3]
    onehots = [(lane_f == ix).astype(F32) for ix in idxs]
    oh = onehots[0] + onehots[1] + onehots[2] + onehots[3]
    tr = lax.broadcasted_iota(jnp.int32, (tm, tm), 0)
    tc = lax.broadcasted_iota(jnp.int32, (tm, tm), 1)
    before = jnp.dot((tc < tr).astype(BF16), oh.astype(BF16), preferred_element_type=F32)
    pos = before + run_ref[...]
    route = jnp.zeros((tm, LANES), F32)
    gates = jnp.zeros((tm, LANES), F32)
    for k in range(TOP_K):
        rank = jnp.sum(onehots[k] * pos, axis=1, keepdims=True)
        route = jnp.where(lane == k, idxs[k], route)
        route = jnp.where(lane == TOP_K + k, rank, route)
        gates = jnp.where(lane == k, ex[k] / den, gates)
    route_ref[...] = route.T[0:2 * TOP_K, :].astype(jnp.int32)
    gate_ref[...] = gates
    run_ref[...] = run_ref[...] + jnp.sum(oh, axis=0, keepdims=True)
    cnt_ref[...] = run_ref[...].astype(jnp.int32)


def _post_mix(mix_p, mix_s, x_p, x_s, w_out_f, lin_g, lin_b, l1_g, l1_b, w_r, w_r2, b_r, alpha, tm):
    d = x_p.shape[1]
    n = x_p.shape[0] + x_s.shape[0]
    d_mix = mix_p.shape[1]
    n_pt = mix_p.shape[0] // tm
    assert mix_p.shape[0] % tm == 0 and mix_s.shape[0] % tm == 0
    row = lambda i: (i, 0)
    fixed = lambda i: (0, 0)
    prompt_row = lambda i: (jnp.minimum(i, n_pt - 1), 0)
    sample_row = lambda i: (jnp.maximum(i - n_pt, 0), 0)
    once = pl.Buffered(1)
    kern = functools.partial(_post_mix_kernel, alpha=alpha, n_pt=n_pt)
    return pl.pallas_call(
        kern,
        grid=(n // tm,),
        in_specs=[pl.BlockSpec((tm, d_mix), prompt_row),
                  pl.BlockSpec((tm, d_mix), sample_row),
                  pl.BlockSpec((tm, d), prompt_row),
                  pl.BlockSpec((tm, d), sample_row),
                  pl.BlockSpec(w_out_f.shape, fixed, pipeline_mode=once),
                  pl.BlockSpec((1, d), fixed), pl.BlockSpec((1, d), fixed),
                  pl.BlockSpec((1, d), fixed), pl.BlockSpec((1, d), fixed),
                  pl.BlockSpec((d, LANES), fixed), pl.BlockSpec((d, 2 * LANES), fixed),
                  pl.BlockSpec((1, LANES), fixed)],
        out_specs=[pl.BlockSpec((tm, d), row),
                   pl.BlockSpec((tm, d // 2), row),
                   pl.BlockSpec((2 * TOP_K, tm), lambda i: (0, i)),
                   pl.BlockSpec((tm, LANES), row),
                   pl.BlockSpec((1, LANES), fixed)],
        out_shape=[jax.ShapeDtypeStruct((n, d), F32),
                   jax.ShapeDtypeStruct((n, d // 2), jnp.uint32),
                   jax.ShapeDtypeStruct((2 * TOP_K, n), jnp.int32),
                   jax.ShapeDtypeStruct((n, LANES), F32),
                   jax.ShapeDtypeStruct((1, LANES), jnp.int32)],
        scratch_shapes=[pltpu.VMEM((1, LANES), F32), pltpu.VMEM((tm, d), F32),
                        pltpu.VMEM((tm, LANES), F32), pltpu.VMEM(w_out_f.shape, BF16)],
        compiler_params=_cparams(("arbitrary",), 48),
        name="post_mix",
    )(mix_p, mix_s, x_p, x_s, w_out_f, lin_g, lin_b, l1_g, l1_b, w_r, w_r2, b_r)


def _dispatch_kernel(*refs, tm, n_experts):
    dest_refs = refs[:TOP_K]
    padlo_ref, padhi_ref, x_ref, xs_hbm, zero_ref, sem = refs[TOP_K:]
    i = pl.program_id(0)

    def row_copy(src, dst_row):
        return pltpu.make_async_copy(src, xs_hbm.at[pl.ds(dst_row, 1), :], sem)

    @pl.when(i == 0)
    def _():
        zero_ref[...] = jnp.zeros_like(zero_ref)

        def per_expert(e, carry):
            def start(r, c):
                row_copy(zero_ref.at[pl.ds(0, 1), :], r).start()
                return c

            def wait(r, c):
                row_copy(zero_ref.at[pl.ds(0, 1), :], r).wait()
                return c

            lax.fori_loop(padlo_ref[e], padhi_ref[e], start, 0)
            lax.fori_loop(padlo_ref[e], padhi_ref[e], wait, 0)
            return carry

        lax.fori_loop(0, n_experts, per_expert, 0)

        tail0 = padhi_ref[n_experts - 1]
        n_tail = (xs_hbm.shape[0] - tail0) // ROW_BLOCK

        def tail_copy(c):
            r0 = pl.multiple_of(tail0 + c * ROW_BLOCK, ROW_BLOCK)
            return pltpu.make_async_copy(zero_ref, xs_hbm.at[pl.ds(r0, ROW_BLOCK), :], sem)

        def tail_start(c, carry):
            tail_copy(c).start()
            return carry

        def tail_wait(c, carry):
            tail_copy(c).wait()
            return carry

        lax.fori_loop(0, n_tail, tail_start, 0)
        lax.fori_loop(0, n_tail, tail_wait, 0)

    def start(t, c):
        src = x_ref.at[pl.ds(t, 1), :]
        for k in range(TOP_K):
            row_copy(src, dest_refs[k][t]).start(priority=k % DMA_THREADS)
        return c

    def wait(t, c):
        src = x_ref.at[pl.ds(t, 1), :]
        for k in range(TOP_K):
            row_copy(src, dest_refs[k][t]).wait()
        return c

    lax.fori_loop(0, tm, start, 0, unroll=4)
    lax.fori_loop(0, tm, wait, 0, unroll=4)


def _dispatch(x1p, dest, pad_lo, pad_hi, n_slots, tm):
    n, dh = x1p.shape
    n_experts = pad_lo.shape[0]
    kern = functools.partial(_dispatch_kernel, tm=tm, n_experts=n_experts)
    return pl.pallas_call(
        kern,
        grid=(n // tm,),
        in_specs=[pl.BlockSpec((tm,), lambda i: (i,), memory_space=pltpu.SMEM)] * TOP_K + [
                  pl.BlockSpec(memory_space=pltpu.SMEM),
                  pl.BlockSpec(memory_space=pltpu.SMEM),
                  pl.BlockSpec((tm, dh), lambda i: (i, 0))],
        out_specs=pl.BlockSpec(memory_space=pl.ANY),
        out_shape=jax.ShapeDtypeStruct((n_slots, dh), jnp.uint32),
        scratch_shapes=[pltpu.VMEM((ROW_BLOCK, dh), jnp.uint32), pltpu.SemaphoreType.DMA(())],
        compiler_params=_cparams(("arbitrary",), 32),
        name="dispatch",
    )(*dest, pad_lo, pad_hi, x1p)


def _expert_kernel(ie_ref, is_ref, ir_ref, xs_hbm, wg_ref, wu_ref, wd_ref, bg_ref, bu_ref, bd_ref,
                   ys_hbm, xbuf, ybuf, wgu_bf, wd_bf, sem_in, sem_out):
    i = pl.program_id(0)
    j = pl.program_id(1)
    n_j = pl.num_programs(1)
    rows = ir_ref[i]
    start = is_ref[i]
    tf = wg_ref.shape[1]
    b_row = ie_ref[i] * n_j + j
    bg = bg_ref[pl.ds(b_row, 1), :]
    bu = bu_ref[pl.ds(b_row, 1), :]
    bd = bd_ref[pl.ds(ie_ref[i], 1), :]

    def in_copy(r0, size):
        g0 = pl.multiple_of(start + r0, ROW_BLOCK)
        return pltpu.make_async_copy(xs_hbm.at[pl.ds(g0, size), :], xbuf.at[pl.ds(r0, size), :], sem_in)

    def out_copy(r0, size):
        g0 = pl.multiple_of(start + r0, ROW_BLOCK)
        return pltpu.make_async_copy(xbuf.at[pl.ds(r0, size), :], ys_hbm.at[pl.ds(g0, size), :], sem_out)

    def for_blocks(fn):
        n_big = rows // MAX_BLOCK

        def body(c, carry):
            fn(pl.multiple_of(c * MAX_BLOCK, MAX_BLOCK), MAX_BLOCK)
            return carry
        lax.fori_loop(0, n_big, body, 0)
        base = n_big * MAX_BLOCK
        size = MAX_BLOCK // 2
        while size >= ROW_BLOCK:
            has = (rows & size) != 0

            @pl.when(has)
            def _(base=base, size=size):
                fn(pl.multiple_of(base, ROW_BLOCK), size)
            base = base + jnp.where(has, size, 0)
            size //= 2

    @pl.when(rows > 0)
    def _():
        @pl.when(j == 0)
        def _():
            for_blocks(lambda r0, size: in_copy(r0, size).start())

        wgu_bf[:, 0:tf] = wg_ref[...].astype(BF16)
        wgu_bf[:, tf:2 * tf] = wu_ref[...].astype(BF16)
        wd_bf[...] = wd_ref[...].astype(BF16)

        @pl.when(j == 0)
        def _():
            for_blocks(lambda r0, size: in_copy(r0, size).wait())

        def block(r0, size, first, last):
            lo, hi = _unpack_bf16_pairs(xbuf[pl.ds(r0, size), :])
            x = jnp.concatenate([lo.astype(BF16), hi.astype(BF16)], axis=1)
            gu = jnp.dot(x, wgu_bf[...], preferred_element_type=F32)
            gg = jnp.minimum(gu[:, 0:tf] + bg, SWIGLU_LIMIT)
            uu = jnp.clip(gu[:, tf:2 * tf] + bu, -SWIGLU_LIMIT, SWIGLU_LIMIT)
            hid = gg * _sigmoid(SWIGLU_ALPHA * gg) * (uu + 1.0)
            y = jnp.dot(hid.astype(BF16), wd_bf[...], preferred_element_type=F32)
            if not first:
                y = y + ybuf[pl.ds(r0, size), :]
            if last:
                xbuf[pl.ds(r0, size), :] = _pack_bf16_pairs(y + bd)
                out_copy(r0, size).start()
            else:
                ybuf[pl.ds(r0, size), :] = y

        @pl.when(j == 0)
        def _():
            for_blocks(lambda r0, size: block(r0, size, True, False))

        @pl.when(jnp.logical_and(j > 0, j < n_j - 1))
        def _():
            for_blocks(lambda r0, size: block(r0, size, False, False))

        @pl.when(j == n_j - 1)
        def _():
            for_blocks(lambda r0, size: block(r0, size, False, True))
            for_blocks(lambda r0, size: out_copy(r0, size).wait())

    @pl.when(jnp.logical_and(i == pl.num_programs(0) - 1, j == n_j - 1))
    def _():
        tail0 = is_ref[pl.num_programs(0)]
        n_tail = (ys_hbm.shape[0] - tail0) // ROW_BLOCK
        xbuf[0:ROW_BLOCK, :] = jnp.zeros((ROW_BLOCK, xbuf.shape[1]), jnp.uint32)

        def tail_copy(c):
            g0 = pl.multiple_of(tail0 + c * ROW_BLOCK, ROW_BLOCK)
            return pltpu.make_async_copy(xbuf.at[pl.ds(0, ROW_BLOCK), :],
                                         ys_hbm.at[pl.ds(g0, ROW_BLOCK), :], sem_out)

        def tail_start(c, carry):
            tail_copy(c).start()
            return carry

        def tail_wait(c, carry):
            tail_copy(c).wait()
            return carry

        lax.fori_loop(0, n_tail, tail_start, 0)
        lax.fori_loop(0, n_tail, tail_wait, 0)


def _experts(xs, item_e, item_start, item_rows, w_gate, w_up, w_down, b_gate, b_up, b_down):
    n_slots, dh = xs.shape
    n_exp, d, d_ff = w_gate.shape
    n_items = item_e.shape[0]
    n_j = d_ff // FF_TILE
    assert n_j >= 2 and d == 2 * dh

    def jj(i, j, ir):
        return jnp.where(ir[i] > 0, j, n_j - 1)

    grid_spec = pltpu.PrefetchScalarGridSpec(
        num_scalar_prefetch=3,
        grid=(n_items, n_j),
        in_specs=[pl.BlockSpec(memory_space=pl.ANY),
                  pl.BlockSpec((None, d, FF_TILE), lambda i, j, ie, is_, ir: (ie[i], 0, jj(i, j, ir))),
                  pl.BlockSpec((None, d, FF_TILE), lambda i, j, ie, is_, ir: (ie[i], 0, jj(i, j, ir))),
                  pl.BlockSpec((None, FF_TILE, d), lambda i, j, ie, is_, ir: (ie[i], jj(i, j, ir), 0)),
                  pl.BlockSpec((n_exp * n_j, FF_TILE), lambda i, j, ie, is_, ir: (0, 0)),
                  pl.BlockSpec((n_exp * n_j, FF_TILE), lambda i, j, ie, is_, ir: (0, 0)),
                  pl.BlockSpec((n_exp, d), lambda i, j, ie, is_, ir: (0, 0))],
        out_specs=pl.BlockSpec(memory_space=pl.ANY),
        scratch_shapes=[pltpu.VMEM((ITEM_ROWS, dh), jnp.uint32),
                        pltpu.VMEM((ITEM_ROWS, d), F32),
                        pltpu.VMEM((d, 2 * FF_TILE), BF16),
                        pltpu.VMEM((FF_TILE, d), BF16),
                        pltpu.SemaphoreType.DMA(()),
                        pltpu.SemaphoreType.DMA(())],
    )
    return pl.pallas_call(
        _expert_kernel,
        grid_spec=grid_spec,
        out_shape=jax.ShapeDtypeStruct((n_slots, dh), jnp.uint32),
        compiler_params=_cparams(("arbitrary", "arbitrary"), 58),
        name="experts",
    )(item_e, item_start, item_rows, xs, w_gate, w_up, w_down,
      b_gate.reshape(n_exp * n_j, FF_TILE), b_up.reshape(n_exp * n_j, FF_TILE), b_down)


def _combine_kernel(*refs, alpha, tm, n_pt):
    dcur_ref = refs[:TOP_K]
    dnext_ref = refs[TOP_K:2 * TOP_K]
    (ys_hbm, x1_ref, gate_ref, p_ref, wpg_ref, wpp_ref, l2g_ref, l2b_ref, l3g_ref, l3b_ref,
     op_ref, os_ref, gbuf_a, gbuf_b, wpgb_ref, sems) = refs[2 * TOP_K:]
    i = pl.program_id(0)
    n_i = pl.num_programs(0)

    @pl.when(i == 0)
    def _():
        wpgb_ref[...] = wpg_ref[...].astype(BF16)

    def row_copy(dref, t, k, buf, sem):
        return pltpu.make_async_copy(ys_hbm.at[pl.ds(dref[k][t], 1), :],
                                     buf.at[k, pl.ds(t, 1), :], sem)

    def gather_loop(dref, buf, sem, wait):
        def body(t, c):
            for k in range(TOP_K):
                cp = row_copy(dref, t, k, buf, sem)
                if wait:
                    cp.wait()
                else:
                    cp.start(priority=k % DMA_THREADS)
            return c
        lax.fori_loop(0, tm, body, 0, unroll=4)

    def step(cur, cur_sem, nxt, nxt_sem):
        @pl.when(i == 0)
        def _():
            gather_loop(dcur_ref, cur, cur_sem, False)

        gather_loop(dcur_ref, cur, cur_sem, True)

        for t in range(tm):
            for k in range(TOP_K):
                row_copy(dnext_ref, t, k, nxt, nxt_sem).start(priority=k % DMA_THREADS)

        x1 = x1_ref[...]
        lane = lax.broadcasted_iota(jnp.int32, gate_ref.shape, 1)
        gates = gate_ref[...]
        half = x1.shape[1] // 2
        ff_lo = jnp.zeros((tm, half), F32)
        ff_hi = jnp.zeros((tm, half), F32)
        for k in range(TOP_K):
            gk = jnp.sum(jnp.where(lane == k, gates, 0.0), axis=1, keepdims=True)
            y_lo, y_hi = _unpack_bf16_pairs(cur[k])
            ff_lo = ff_lo + gk * y_lo
            ff_hi = ff_hi + gk * y_hi
        ff = jnp.concatenate([ff_lo, ff_hi], axis=1)
        x2 = _layer_norm(alpha * x1 + ff, l2g_ref[...], l2b_ref[...])
        eg = _sigmoid(jnp.dot(x2.astype(BF16), wpgb_ref[...], preferred_element_type=F32))
        ep = jnp.dot(p_ref[...].astype(BF16), wpp_ref[...], preferred_element_type=F32)
        out = _layer_norm(alpha * x2 + eg * ep, l3g_ref[...], l3b_ref[...])

        @pl.when(i < n_pt)
        def _():
            op_ref[...] = out

        @pl.when(i >= n_pt)
        def _():
            os_ref[...] = out

        @pl.when(i == n_i - 1)
        def _():
            gather_loop(dnext_ref, nxt, nxt_sem, True)

    @pl.when(lax.rem(i, 2) == 0)
    def _():
        step(gbuf_a, sems.at[0], gbuf_b, sems.at[1])

    @pl.when(lax.rem(i, 2) == 1)
    def _():
        step(gbuf_b, sems.at[1], gbuf_a, sems.at[0])


def _combine(ys, dest, x1, gates, p, w_pg, w_pp, l2_g, l2_b, l3_g, l3_b, alpha, tm, n_prompt):
    n, d = x1.shape
    n_i = n // tm
    n_pt = n_prompt // tm
    assert n_prompt % tm == 0 and n % tm == 0
    row = lambda i: (i, 0)
    fixed = lambda i: (0, 0)
    kern = functools.partial(_combine_kernel, alpha=alpha, tm=tm, n_pt=n_pt)
    return pl.pallas_call(
        kern,
        grid=(n_i,),
        in_specs=[pl.BlockSpec((tm,), lambda i: (i,), memory_space=pltpu.SMEM)] * TOP_K + [
                  pl.BlockSpec((tm,), lambda i: (jnp.minimum(i + 1, n_i - 1),),
                               memory_space=pltpu.SMEM)] * TOP_K + [
                  pl.BlockSpec(memory_space=pl.ANY),
                  pl.BlockSpec((tm, d), row),
                  pl.BlockSpec((tm, LANES), row),
                  pl.BlockSpec((tm, p.shape[1]), row),
                  pl.BlockSpec(w_pg.shape, fixed, pipeline_mode=pl.Buffered(1)),
                  pl.BlockSpec(w_pp.shape, fixed),
                  pl.BlockSpec((1, d), fixed), pl.BlockSpec((1, d), fixed),
                  pl.BlockSpec((1, d), fixed), pl.BlockSpec((1, d), fixed)],
        out_specs=[pl.BlockSpec((tm, d), lambda i: (jnp.minimum(i, n_pt - 1), 0)),
                   pl.BlockSpec((tm, d), lambda i: (jnp.maximum(i - n_pt, 0), 0))],
        out_shape=[jax.ShapeDtypeStruct((n_prompt, d), F32),
                   jax.ShapeDtypeStruct((n - n_prompt, d), F32)],
        scratch_shapes=[pltpu.VMEM((TOP_K, tm, d // 2), jnp.uint32),
                        pltpu.VMEM((TOP_K, tm, d // 2), jnp.uint32),
                        pltpu.VMEM(w_pg.shape, BF16), pltpu.SemaphoreType.DMA((2,))],
        compiler_params=_cparams(("arbitrary",), 48),
        name="combine",
    )(*dest, *dest, ys, x1, gates, p, w_pg, w_pp, l2_g, l2_b, l3_g, l3_b)


def _routing_tables(expert_idx, rank, counts, n_items):
    n_exp = counts.shape[0]
    padded = (counts + ROW_BLOCK - 1) // ROW_BLOCK * ROW_BLOCK
    pad_end = jnp.cumsum(padded)
    pad_start = pad_end - padded
    experts = jnp.arange(n_exp, dtype=expert_idx.dtype)[:, None, None]
    group0 = jnp.sum(jnp.where(expert_idx[None] == experts, pad_start[:, None, None], 0), axis=0)
    dest = (group0 + rank).astype(jnp.int32)
    dest = tuple(dest[k] for k in range(TOP_K))
    items_per = (padded + ITEM_ROWS - 1) // ITEM_ROWS
    item_end = jnp.cumsum(items_per)
    ids = jnp.arange(n_items, dtype=jnp.int32)
    total = item_end[-1]
    last_valid = jnp.maximum(total - 1, 0)
    eff = jnp.minimum(ids, last_valid)
    e_of = jnp.minimum(jnp.searchsorted(item_end, eff, side='right'), n_exp - 1).astype(jnp.int32)
    sub = eff - (item_end[e_of] - items_per[e_of])
    start = pad_start[e_of] + sub * ITEM_ROWS
    rows = jnp.clip(padded[e_of] - sub * ITEM_ROWS, 0, ITEM_ROWS)
    rows = jnp.where(ids < total, rows, 0)
    start = jnp.concatenate([start, pad_end[-1:]])
    return (dest, (pad_start + counts).astype(jnp.int32), pad_end.astype(jnp.int32),
            e_of, start.astype(jnp.int32), rows.astype(jnp.int32))


def kernel(x_prompt, x_sample, state_conv, state_rec, p_prompt, p_sample, ln_in_g, ln_in_b, w_in, conv_w,
           lb_theta, rms_g, w_out, ln1_g, ln1_b, w_router, b_router, w_gate, b_gate, w_up, b_up, w_down,
           b_down, ln2_g, ln2_b, w_ple_gate, w_ple_proj, ln3_g, ln3_b):
    bp, sp, d = x_prompt.shape
    bs, ss, _ = x_sample.shape
    depth = w_in.shape[0]
    assert depth == 1 and ss == 1 and sp % CHUNK == 0
    d_conv = state_conv.shape[-1]
    n_exp = w_router.shape[-1]
    n_p = bp * sp
    n = n_p + bs
    alpha = (2 * depth) ** 0.25
    row2 = lambda a: a.reshape(1, -1)

    lb = jnp.cumsum(jax.nn.softmax(lb_theta.astype(F32), axis=0), axis=0)[0]
    x_p = x_prompt.reshape(n_p, d)
    x_s = x_sample.reshape(bs, d)
    p_all = jnp.concatenate([p_prompt[0].reshape(n_p, -1), p_sample[0].reshape(bs, -1)], axis=0)

    xn_p = _ln_in(x_p, row2(ln_in_g), row2(ln_in_b), tm=512)
    proj_p = _in_proj(xn_p, w_in[0], tm=1024, tn=1024)
    proj_s = _in_proj_sample(x_s, row2(ln_in_g), row2(ln_in_b), w_in[0], tn=1024)

    mix_p, conv_tail, rec_p = _mix_prompt(proj_p, conv_w[0], row2(lb), row2(rms_g[0]), n_p, bp, sp, d_conv)
    mix_s, conv_s, rec_s = _mix_sample(proj_s, state_conv[0].reshape(bs, -1), state_rec[0], conv_w[0],
                                       row2(lb), row2(rms_g[0]), d_conv)

    w_r = jnp.zeros((d, LANES), F32).at[:, :n_exp].set(w_router[0])
    b_r = jnp.full((1, LANES), NEG_BIG, F32).at[0, :n_exp].set(b_router[0])
    w_r_hi = w_r.astype(BF16)
    w_r2 = jnp.concatenate([w_r_hi, (w_r - w_r_hi.astype(F32)).astype(BF16)], axis=1)
    x1, x1p, route, gates, counts = _post_mix(
        mix_p, mix_s, x_p, x_s, w_out[0], row2(ln_in_g), row2(ln_in_b),
        row2(ln1_g[0]), row2(ln1_b[0]), w_r, w_r2, b_r, alpha, tm=128)

    n_slots = n * TOP_K + n_exp * ROW_BLOCK
    n_items = n_exp + n_slots // ITEM_ROWS
    dest, pad_lo, pad_hi, item_e, item_start, item_rows = _routing_tables(
        route[0:TOP_K], route[TOP_K:2 * TOP_K], counts[0, :n_exp], n_items)

    xs = _dispatch(x1p, dest, pad_lo, pad_hi, n_slots, tm=128)
    ys = _experts(xs, item_e, item_start, item_rows, w_gate[0], w_up[0], w_down[0],
                  b_gate[0], b_up[0], b_down[0])
    y_p, y_s = _combine(ys, dest, x1, gates, p_all, w_ple_gate[0], w_ple_proj[0].astype(BF16),
                        row2(ln2_g[0]), row2(ln2_b[0]), row2(ln3_g[0]), row2(ln3_b[0]), alpha, tm=128,
                        n_prompt=n_p)

    return (y_p.reshape(bp, sp, d),
            y_s.reshape(bs, ss, d),
            conv_tail[:, SUBLANES - (CONV_W - 1):, :][None],
            rec_p[None],
            conv_s.reshape(bs, CONV_W - 1, d_conv)[None],
            rec_s[None])
```

```python
import functools

import numpy as np
import jax
import jax.numpy as jnp
from jax import lax
from jax.experimental import pallas as pl
from jax.experimental.pallas import tpu as pltpu

F32 = jnp.float32
BF16 = jnp.bfloat16
HIGHEST = lax.Precision.HIGHEST

CONV_W = 3
N_HEADS = 8
HEAD_K = 128
HEAD_V = 128
TOP_K = 4
SWIGLU_LIMIT = 7.0
SWIGLU_ALPHA = 1.702
LN_EPS = 1e-5
RMS_EPS = 1e-6

LANES = 128
SUBLANES = 8
VMEM_PHYSICAL_BYTES = 64 * 1024 * 1024
DMA_THREADS = 2

CHUNK = 128
SEQS_PER_STEP = 2
SAMPLE_BLOCK = 16
ROW_BLOCK = 128
MAX_BLOCK = 512
ITEM_ROWS = 1280
FF_TILE = 512
NEG_BIG = -1e30


def _cparams(sem, vmem_mb):
    return pltpu.CompilerParams(dimension_semantics=sem, vmem_limit_bytes=vmem_mb * 1024 * 1024)


def _layer_norm(x, g, b):
    mu = jnp.mean(x, axis=-1, keepdims=True)
    xc = x - mu
    var = jnp.mean(xc * xc, axis=-1, keepdims=True)
    return xc * lax.rsqrt(var + LN_EPS) * g + b


def _sigmoid(x):
    return 1.0 / (1.0 + jnp.exp(-x))


def _pack_bf16_pairs(x):
    half = x.shape[1] // 2
    bits = pltpu.bitcast(x.astype(BF16).astype(F32), jnp.uint32)
    return (bits[:, half:] & jnp.uint32(0xFFFF0000)) | (bits[:, :half] >> 16)


def _unpack_bf16_pairs(u):
    return (pltpu.bitcast(u << 16, F32), pltpu.bitcast(u & jnp.uint32(0xFFFF0000), F32))


def _split3(x, axis):
    p1 = x.astype(BF16)
    r1 = x - p1.astype(F32)
    p2 = r1.astype(BF16)
    p3 = (r1 - p2.astype(F32)).astype(BF16)
    return jnp.concatenate([p1, p2, p3], axis=axis)


def _ln_in_kernel(x_ref, g_ref, b_ref, o_ref):
    o_ref[...] = _layer_norm(x_ref[...], g_ref[...], b_ref[...]).astype(BF16)


def _ln_in(x, g, b, tm):
    n, d = x.shape
    return pl.pallas_call(
        _ln_in_kernel,
        grid=(n // tm,),
        in_specs=[pl.BlockSpec((tm, d), lambda i: (i, 0)),
                  pl.BlockSpec((1, d), lambda i: (0, 0)),
                  pl.BlockSpec((1, d), lambda i: (0, 0))],
        out_specs=pl.BlockSpec((tm, d), lambda i: (i, 0)),
        out_shape=jax.ShapeDtypeStruct((n, d), BF16),
        compiler_params=_cparams(("parallel",), 40),
        name="ln_in",
    )(x, g, b)


def _matmul_kernel(x_ref, w_ref, o_ref, wb_ref):
    @pl.when(pl.program_id(1) == 0)
    def _():
        wb_ref[...] = w_ref[...].astype(BF16)

    o_ref[...] = jnp.dot(x_ref[...], wb_ref[...], preferred_element_type=F32)


def _in_proj(xn, w, tm, tn):
    n, d = xn.shape
    d_in = w.shape[1]
    return pl.pallas_call(
        _matmul_kernel,
        grid=(d_in // tn, n // tm),
        in_specs=[pl.BlockSpec((tm, d), lambda j, i: (i, 0)),
                  pl.BlockSpec((d, tn), lambda j, i: (0, j))],
        out_specs=pl.BlockSpec((tm, tn), lambda j, i: (i, j)),
        out_shape=jax.ShapeDtypeStruct((n, d_in), F32),
        scratch_shapes=[pltpu.VMEM((d, tn), BF16)],
        compiler_params=_cparams(("arbitrary", "arbitrary"), 48),
        name="in_proj",
    )(xn, w)


def _in_proj_sample_kernel(x_ref, g_ref, b_ref, w_ref, o_ref):
    xn = _layer_norm(x_ref[...], g_ref[...], b_ref[...])
    o_ref[...] = jnp.dot(xn, w_ref[...], precision=HIGHEST, preferred_element_type=F32)


def _in_proj_sample(x, g, b, w, tn):
    n, d = x.shape
    d_in = w.shape[1]
    return pl.pallas_call(
        _in_proj_sample_kernel,
        grid=(d_in // tn,),
        in_specs=[pl.BlockSpec((n, d), lambda j: (0, 0)),
                  pl.BlockSpec((1, d), lambda j: (0, 0)),
                  pl.BlockSpec((1, d), lambda j: (0, 0)),
                  pl.BlockSpec((d, tn), lambda j: (0, j))],
        out_specs=pl.BlockSpec((n, tn), lambda j: (0, j)),
        out_shape=jax.ShapeDtypeStruct((n, d_in), F32),
        compiler_params=_cparams(("parallel",), 40),
        name="in_proj_sample",
    )(x, g, b, w)


def _forget_gates(fz, lb):
    e = jnp.exp(-jnp.abs(fz))
    r = 1.0 / (1.0 + e)
    er = e * r
    pos = fz >= 0
    sig_p = jnp.where(pos, r, er)
    sig_n = jnp.where(pos, er, r)
    oml = 1.0 - lb
    return lb + oml * sig_p, oml * sig_n


def _chunk_matrices(c):
    t = np.arange(c)[:, None]
    j = np.arange(c)[None, :]
    mats = [(j <= t), (j > t)]
    blk = c
    while blk >= 2:
        half = blk // 2
        mid = (t // blk) * blk + half
        second = (t % blk) >= half
        m_q = (j >= mid) & (j <= t)
        m_k = (j > t) & (j < mid)
        mats.append(np.where(second, m_q, m_k))
        blk = half
    return np.concatenate(mats, axis=0).astype(np.float32)


def _mix_prompt_kernel(proj_ref, convw_ref, lb_ref, rmsg_ref, cmat_ref,
                       mix_ref, convst_ref, recst_ref, s_ref, carry_ref, *, d_conv):
    tb = pl.program_id(1)

    @pl.when(tb == 0)
    def _():
        s_ref[...] = jnp.zeros_like(s_ref)
        carry_ref[...] = jnp.zeros_like(carry_ref)

    for s in range(proj_ref.shape[0]):
        _mix_prompt_chunk(proj_ref.at[s], convw_ref, lb_ref, rmsg_ref, cmat_ref, mix_ref.at[s],
                          convst_ref.at[s], s_ref.at[s], carry_ref.at[s], d_conv=d_conv)

    @pl.when(tb == pl.num_programs(1) - 1)
    def _():
        recst_ref[...] = s_ref[...]


def _mix_prompt_chunk(proj_ref, convw_ref, lb_ref, rmsg_ref, cmat_ref, mix_ref, convst_ref, s_ref,
                      carry_ref, *, d_conv):
    c = CHUNK
    u = proj_ref[:, 0:d_conv] * proj_ref[:, 2 * d_conv:3 * d_conv]
    row = lax.broadcasted_iota(jnp.int32, u.shape, 0)
    prev1 = carry_ref[SUBLANES - 1:SUBLANES, :]
    prev2 = carry_ref[SUBLANES - 2:SUBLANES - 1, :]
    u1 = jnp.where(row == 0, prev1, pltpu.roll(u, 1, 0))
    u2 = jnp.where(row == 0, prev2, jnp.where(row == 1, prev1, pltpu.roll(u, 2, 0)))
    y = convw_ref[0:1, :] * u2 + convw_ref[1:2, :] * u1 + convw_ref[2:3, :] * u
    mix_ref[:, 0:d_conv] = (proj_ref[:, d_conv:2 * d_conv] * y).astype(BF16)
    carry_ref[...] = u[c - SUBLANES:c, :]
    convst_ref[...] = u[c - SUBLANES:c, :]

    o0 = 3 * d_conv
    d_rec = N_HEADS * HEAD_K
    q = proj_ref[:, o0:o0 + d_rec]
    fz = proj_ref[:, o0 + d_rec:o0 + 2 * d_rec]
    v = proj_ref[:, o0 + 2 * d_rec:o0 + 3 * d_rec]
    g = proj_ref[:, o0 + 3 * d_rec:o0 + 4 * d_rec]
    f, kk = _forget_gates(fz, lb_ref[...])
    contract0 = (((0,), (0,)), ((), ()))
    logf3 = _split3(jnp.log(f), axis=0)
    ex = jnp.dot(cmat_ref[...], logf3, preferred_element_type=F32)
    b_cum = ex[0:c]
    d_end = ex[c:2 * c]
    n_lev = cmat_ref.shape[0] // c - 2
    b_cols = lax.dot_general(logf3, jnp.ones((3 * c, HEAD_V), BF16), contract0,
                             preferred_element_type=F32)

    trow = lax.broadcasted_iota(jnp.int32, (c, c), 0)
    tcol = lax.broadcasted_iota(jnp.int32, (c, c), 1)
    prow = lax.broadcasted_iota(jnp.int32, (c, HEAD_K), 0)
    contract1 = (((1,), (1,)), ((), ()))

    for h in range(N_HEADS):
        sl = slice(h * HEAD_K, (h + 1) * HEAD_K)
        qh, kh, vh = q[:, sl], kk[:, sl], v[:, sl]
        vb = vh.astype(BF16)
        s_old = s_ref[h]
        o = jnp.dot((qh * jnp.exp(b_cum[:, sl])).astype(BF16), s_old.astype(BF16),
                    preferred_element_type=F32)
        sc = jnp.zeros((c, c), F32)
        for lev in range(n_lev):
            blk = c >> lev
            sh = blk.bit_length() - 1
            dl = jnp.exp(ex[(2 + lev) * c:(3 + lev) * c, sl])
            second = (prow & (blk - 1)) >= (blk // 2)
            qt = jnp.where(second, qh * dl, 0.0).astype(BF16)
            kt = jnp.where(second, 0.0, kh * dl).astype(BF16)
            s_l = lax.dot_general(qt, kt, contract1, preferred_element_type=F32)
            sc = sc + jnp.where((trow >> sh) == (tcol >> sh), s_l, 0.0)
        o = o + jnp.dot(sc.astype(BF16), vb, preferred_element_type=F32)
        o = o + jnp.sum(qh * kh, axis=1, keepdims=True) * vh
        khat = (kh * jnp.exp(d_end[:, sl])).astype(BF16)
        upd = lax.dot_general(khat, vb, contract0, preferred_element_type=F32)
        s_ref[h] = jnp.exp(b_cols[sl, :]) * s_old + upd
        on = o * lax.rsqrt(jnp.mean(o * o, axis=1, keepdims=True) + RMS_EPS) * rmsg_ref[:, sl]
        gh = g[:, sl]
        mix_ref[:, d_conv + h * HEAD_V:d_conv + (h + 1) * HEAD_V] = (
            on * (gh * _sigmoid(gh))).astype(BF16)


def _mix_prompt(proj, conv_w, lb, rms_g, bsz, seq, d_conv):
    d_in = proj.shape[1]
    d_mix = d_conv + N_HEADS * HEAD_V
    n_tb = seq // CHUNK
    ns = SEQS_PER_STEP
    assert bsz % ns == 0
    cmat = jnp.asarray(np.tile(_chunk_matrices(CHUNK), (1, 3)), dtype=BF16)
    kern = functools.partial(_mix_prompt_kernel, d_conv=d_conv)
    mix, conv_tail, rec = pl.pallas_call(
        kern,
        grid=(bsz // ns, n_tb),
        in_specs=[pl.BlockSpec((ns, CHUNK, d_in), lambda b, t: (b, t, 0)),
                  pl.BlockSpec((CONV_W, d_conv), lambda b, t: (0, 0)),
                  pl.BlockSpec((1, N_HEADS * HEAD_K), lambda b, t: (0, 0)),
                  pl.BlockSpec((1, N_HEADS * HEAD_V), lambda b, t: (0, 0)),
                  pl.BlockSpec(cmat.shape, lambda b, t: (0, 0))],
        out_specs=[pl.BlockSpec((ns, CHUNK, d_mix), lambda b, t: (b, t, 0)),
                   pl.BlockSpec((ns, SUBLANES, d_conv), lambda b, t: (b, 0, 0)),
                   pl.BlockSpec((ns, N_HEADS, HEAD_K, HEAD_V), lambda b, t: (b, 0, 0, 0))],
        out_shape=[jax.ShapeDtypeStruct((bsz, seq, d_mix), BF16),
                   jax.ShapeDtypeStruct((bsz, SUBLANES, d_conv), F32),
                   jax.ShapeDtypeStruct((bsz, N_HEADS, HEAD_K, HEAD_V), F32)],
        scratch_shapes=[pltpu.VMEM((ns, N_HEADS, HEAD_K, HEAD_V), F32),
                        pltpu.VMEM((ns, SUBLANES, d_conv), F32)],
        compiler_params=_cparams(("parallel", "arbitrary"), 48),
        name="mix_prompt",
    )(proj.reshape(bsz, seq, d_in), conv_w, lb, rms_g, cmat)
    return mix.reshape(bsz * seq, d_mix), conv_tail, rec


def _mix_sample_kernel(proj_ref, cst_ref, rst_ref, convw_ref, lb_ref, rmsg_ref, sel_ref,
                       mix_ref, cnew_ref, rnew_ref, *, d_conv):
    nb = SAMPLE_BLOCK
    u = proj_ref[:, 0:d_conv] * proj_ref[:, 2 * d_conv:3 * d_conv]
    buf0 = cst_ref[:, 0:d_conv]
    buf1 = cst_ref[:, d_conv:2 * d_conv]
    y = convw_ref[0:1, :] * buf0 + convw_ref[1:2, :] * buf1 + convw_ref[2:3, :] * u
    mix_ref[:, 0:d_conv] = proj_ref[:, d_conv:2 * d_conv] * y
    cnew_ref[:, 0:d_conv] = buf1
    cnew_ref[:, d_conv:2 * d_conv] = u

    o0 = 3 * d_conv
    d_rec = N_HEADS * HEAD_K
    q = proj_ref[:, o0:o0 + d_rec]
    fz = proj_ref[:, o0 + d_rec:o0 + 2 * d_rec]
    v = proj_ref[:, o0 + 2 * d_rec:o0 + 3 * d_rec]
    g = proj_ref[:, o0 + 3 * d_rec:o0 + 4 * d_rec]
    f, kk = _forget_gates(fz, lb_ref[...])
    contract0 = (((0,), (0,)), ((), ()))
    sel = sel_ref[...]
    row = lax.broadcasted_iota(jnp.int32, (nb, HEAD_V), 0)

    def columns(a):
        return lax.dot_general(_split3(a, axis=0), sel, contract0, preferred_element_type=F32)

    for h in range(N_HEADS):
        sl = slice(h * HEAD_K, (h + 1) * HEAD_K)
        f_c, k_c, q_c = columns(f[:, sl]), columns(kk[:, sl]), columns(q[:, sl])
        o = jnp.zeros((nb, HEAD_V), F32)
        for n in range(nb):
            nl = slice(n * HEAD_V, (n + 1) * HEAD_V)
            s_new = f_c[:, nl] * rst_ref[n, h] + k_c[:, nl] * v[n:n + 1, sl]
            rnew_ref[n, h] = s_new
            o_row = jnp.sum(q_c[:, nl] * s_new, axis=0, keepdims=True)
            o = jnp.where(row == n, o_row, o)
        on = o * lax.rsqrt(jnp.mean(o * o, axis=1, keepdims=True) + RMS_EPS) * rmsg_ref[:, sl]
        gh = g[:, sl]
        mix_ref[:, d_conv + h * HEAD_V:d_conv + (h + 1) * HEAD_V] = on * (gh * _sigmoid(gh))


def _mix_sample(proj, conv_state, rec_state, conv_w, lb, rms_g, d_conv):
    n_seq = conv_state.shape[0]
    d_in = proj.shape[1]
    d_mix = d_conv + N_HEADS * HEAD_V
    nb = SAMPLE_BLOCK
    sel = jnp.asarray(np.tile(np.kron(np.eye(nb), np.ones((1, HEAD_V))), (3, 1)), dtype=BF16)
    kern = functools.partial(_mix_sample_kernel, d_conv=d_conv)
    return pl.pallas_call(
        kern,
        grid=(n_seq // nb,),
        in_specs=[pl.BlockSpec((nb, d_in), lambda i: (i, 0)),
                  pl.BlockSpec((nb, 2 * d_conv), lambda i: (i, 0)),
                  pl.BlockSpec((nb, N_HEADS, HEAD_K, HEAD_V), lambda i: (i, 0, 0, 0)),
                  pl.BlockSpec((CONV_W, d_conv), lambda i: (0, 0)),
                  pl.BlockSpec((1, N_HEADS * HEAD_K), lambda i: (0, 0)),
                  pl.BlockSpec((1, N_HEADS * HEAD_V), lambda i: (0, 0)),
                  pl.BlockSpec(sel.shape, lambda i: (0, 0))],
        out_specs=[pl.BlockSpec((nb, d_mix), lambda i: (i, 0)),
                   pl.BlockSpec((nb, 2 * d_conv), lambda i: (i, 0)),
                   pl.BlockSpec((nb, N_HEADS, HEAD_K, HEAD_V), lambda i: (i, 0, 0, 0))],
        out_shape=[jax.ShapeDtypeStruct((n_seq, d_mix), F32),
                   jax.ShapeDtypeStruct((n_seq, 2 * d_conv), F32),
                   jax.ShapeDtypeStruct(rec_state.shape, F32)],
        compiler_params=_cparams(("parallel",), 52),
        name="mix_sample",
    )(proj, conv_state, rec_state, conv_w, lb, rms_g, sel)


def _post_mix_kernel(mixp_ref, mixs_ref, xp_ref, xs_ref, woutf_ref, ling_ref, linb_ref,
                     l1g_ref, l1b_ref, wr_ref, wr2_ref, br_ref,
                     x1_ref, x1p_ref, route_ref, gate_ref, cnt_ref, run_ref, h_ref, lg_ref, woutb_ref,
                     *, alpha, n_pt):
    i = pl.program_id(0)

    @pl.when(i == 0)
    def _():
        run_ref[...] = jnp.zeros_like(run_ref)
        woutb_ref[...] = woutf_ref[...].astype(BF16)

    @pl.when(i < n_pt)
    def _():
        h_ref[...] = jnp.dot(mixp_ref[...], woutb_ref[...], preferred_element_type=F32)

    @pl.when(i >= n_pt)
    def _():
        h_ref[...] = jnp.dot(mixs_ref[...], woutf_ref[...], precision=HIGHEST,
                             preferred_element_type=F32)

    x = jnp.where(i < n_pt, xp_ref[...], xs_ref[...])
    xn = _layer_norm(x, ling_ref[...], linb_ref[...])
    x1 = _layer_norm(alpha * xn + h_ref[...], l1g_ref[...], l1b_ref[...])
    x1_ref[...] = x1
    x1p_ref[...] = _pack_bf16_pairs(x1)

    tm = x1.shape[0]

    @pl.when(i < n_pt)
    def _():
        xh = x1.astype(BF16)
        xl = (x1 - xh.astype(F32)).astype(BF16)
        pr = jnp.dot(jnp.concatenate([xh, xl], axis=0), wr2_ref[...], preferred_element_type=F32)
        lg_ref[...] = (pr[0:tm, 0:LANES] + pr[0:tm, LANES:2 * LANES]
                       + pr[tm:2 * tm, 0:LANES] + pr[tm:2 * tm, LANES:2 * LANES])

    @pl.when(i >= n_pt)
    def _():
        lg_ref[...] = jnp.dot(x1, wr_ref[...], precision=HIGHEST, preferred_element_type=F32)

    logits = lg_ref[...] + br_ref[...]
    lane = lax.broadcasted_iota(jnp.int32, (tm, LANES), 1)
    lane_f = lane.astype(F32)
    work = logits
    vals, idxs = [], []
    for _ in range(TOP_K):
        m = jnp.max(work, axis=1, keepdims=True)
        ix = jnp.min(jnp.where(work == m, lane_f, float(LANES)), axis=1, keepdims=True)
        vals.append(m)
        idxs.append(ix)
        work = jnp.where(lane_f == ix, NEG_BIG, work)
    ex = [jnp.exp(vv - vals[0]) for vv in vals]
    den = ex[0] + ex[1] + ex[2] + ex[3]
    onehots = [(lane_f == ix).astype(F32) for ix in idxs]
    oh = onehots[0] + onehots[1] + onehots[2] + onehots[3]
    tr = lax.broadcasted_iota(jnp.int32, (tm, tm), 0)
    tc = lax.broadcasted_iota(jnp.int32, (tm, tm), 1)
    before = jnp.dot((tc < tr).astype(BF16), oh.astype(BF16), preferred_element_type=F32)
    pos = before + run_ref[...]
    route = jnp.zeros((tm, LANES), F32)
    gates = jnp.zeros((tm, LANES), F32)
    for k in range(TOP_K):
        rank = jnp.sum(onehots[k] * pos, axis=1, keepdims=True)
        route = jnp.where(lane == k, idxs[k], route)
        route = jnp.where(lane == TOP_K + k, rank, route)
        gates = jnp.where(lane == k, ex[k] / den, gates)
    route_ref[...] = route.T[0:2 * TOP_K, :].astype(jnp.int32)
    gate_ref[...] = gates
    run_ref[...] = run_ref[...] + jnp.sum(oh, axis=0, keepdims=True)
    cnt_ref[...] = run_ref[...].astype(jnp.int32)


def _post_mix(mix_p, mix_s, x_p, x_s, w_out_f, lin_g, lin_b, l1_g, l1_b, w_r, w_r2, b_r, alpha, tm):
    d = x_p.shape[1]
    n = x_p.shape[0] + x_s.shape[0]
    d_mix = mix_p.shape[1]
    n_pt = mix_p.shape[0] // tm
    assert mix_p.shape[0] % tm == 0 and mix_s.shape[0] % tm == 0
    row = lambda i: (i, 0)
    fixed = lambda i: (0, 0)
    prompt_row = lambda i: (jnp.minimum(i, n_pt - 1), 0)
    sample_row = lambda i: (jnp.maximum(i - n_pt, 0), 0)
    once = pl.Buffered(1)
    kern = functools.partial(_post_mix_kernel, alpha=alpha, n_pt=n_pt)
    return pl.pallas_call(
        kern,
        grid=(n // tm,),
        in_specs=[pl.BlockSpec((tm, d_mix), prompt_row),
                  pl.BlockSpec((tm, d_mix), sample_row),
                  pl.BlockSpec((tm, d), prompt_row),
                  pl.BlockSpec((tm, d), sample_row),
                  pl.BlockSpec(w_out_f.shape, fixed, pipeline_mode=once),
                  pl.BlockSpec((1, d), fixed), pl.BlockSpec((1, d), fixed),
                  pl.BlockSpec((1, d), fixed), pl.BlockSpec((1, d), fixed),
                  pl.BlockSpec((d, LANES), fixed), pl.BlockSpec((d, 2 * LANES), fixed),
                  pl.BlockSpec((1, LANES), fixed)],
        out_specs=[pl.BlockSpec((tm, d), row),
                   pl.BlockSpec((tm, d // 2), row),
                   pl.BlockSpec((2 * TOP_K, tm), lambda i: (0, i)),
                   pl.BlockSpec((tm, LANES), row),
                   pl.BlockSpec((1, LANES), fixed)],
        out_shape=[jax.ShapeDtypeStruct((n, d), F32),
                   jax.ShapeDtypeStruct((n, d // 2), jnp.uint32),
                   jax.ShapeDtypeStruct((2 * TOP_K, n), jnp.int32),
                   jax.ShapeDtypeStruct((n, LANES), F32),
                   jax.ShapeDtypeStruct((1, LANES), jnp.int32)],
        scratch_shapes=[pltpu.VMEM((1, LANES), F32), pltpu.VMEM((tm, d), F32),
                        pltpu.VMEM((tm, LANES), F32), pltpu.VMEM(w_out_f.shape, BF16)],
        compiler_params=_cparams(("arbitrary",), 48),
        name="post_mix",
    )(mix_p, mix_s, x_p, x_s, w_out_f, lin_g, lin_b, l1_g, l1_b, w_r, w_r2, b_r)


def _dispatch_kernel(*refs, tm, n_experts):
    dest_refs = refs[:TOP_K]
    padlo_ref, padhi_ref, x_ref, xs_hbm, zero_ref, sem = refs[TOP_K:]
    i = pl.program_id(0)

    def row_copy(src, dst_row):
        return pltpu.make_async_copy(src, xs_hbm.at[pl.ds(dst_row, 1), :], sem)

    @pl.when(i == 0)
    def _():
        zero_ref[...] = jnp.zeros_like(zero_ref)

        def per_expert(e, carry):
            def start(r, c):
                row_copy(zero_ref.at[pl.ds(0, 1), :], r).start()
                return c

            def wait(r, c):
                row_copy(zero_ref.at[pl.ds(0, 1), :], r).wait()
                return c

            lax.fori_loop(padlo_ref[e], padhi_ref[e], start, 0)
            lax.fori_loop(padlo_ref[e], padhi_ref[e], wait, 0)
            return carry

        lax.fori_loop(0, n_experts, per_expert, 0)

        tail0 = padhi_ref[n_experts - 1]
        n_tail = (xs_hbm.shape[0] - tail0) // ROW_BLOCK

        def tail_copy(c):
            r0 = pl.multiple_of(tail0 + c * ROW_BLOCK, ROW_BLOCK)
            return pltpu.make_async_copy(zero_ref, xs_hbm.at[pl.ds(r0, ROW_BLOCK), :], sem)

        def tail_start(c, carry):
            tail_copy(c).start()
            return carry

        def tail_wait(c, carry):
            tail_copy(c).wait()
            return carry

        lax.fori_loop(0, n_tail, tail_start, 0)
        lax.fori_loop(0, n_tail, tail_wait, 0)

    def start(t, c):
        src = x_ref.at[pl.ds(t, 1), :]
        for k in range(TOP_K):
            row_copy(src, dest_refs[k][t]).start(priority=k % DMA_THREADS)
        return c

    def wait(t, c):
        src = x_ref.at[pl.ds(t, 1), :]
        for k in range(TOP_K):
            row_copy(src, dest_refs[k][t]).wait()
        return c

    lax.fori_loop(0, tm, start, 0, unroll=4)
    lax.fori_loop(0, tm, wait, 0, unroll=4)


def _dispatch(x1p, dest, pad_lo, pad_hi, n_slots, tm):
    n, dh = x1p.shape
    n_experts = pad_lo.shape[0]
    kern = functools.partial(_dispatch_kernel, tm=tm, n_experts=n_experts)
    return pl.pallas_call(
        kern,
        grid=(n // tm,),
        in_specs=[pl.BlockSpec((tm,), lambda i: (i,), memory_space=pltpu.SMEM)] * TOP_K + [
                  pl.BlockSpec(memory_space=pltpu.SMEM),
                  pl.BlockSpec(memory_space=pltpu.SMEM),
                  pl.BlockSpec((tm, dh), lambda i: (i, 0))],
        out_specs=pl.BlockSpec(memory_space=pl.ANY),
        out_shape=jax.ShapeDtypeStruct((n_slots, dh), jnp.uint32),
        scratch_shapes=[pltpu.VMEM((ROW_BLOCK, dh), jnp.uint32), pltpu.SemaphoreType.DMA(())],
        compiler_params=_cparams(("arbitrary",), 32),
        name="dispatch",
    )(*dest, pad_lo, pad_hi, x1p)


def _expert_kernel(ie_ref, is_ref, ir_ref, xs_hbm, wg_ref, wu_ref, wd_ref, bg_ref, bu_ref, bd_ref,
                   ys_hbm, xbuf, ybuf, wgu_bf, wd_bf, sem_in, sem_out):
    i = pl.program_id(0)
    j = pl.program_id(1)
    n_j = pl.num_programs(1)
    rows = ir_ref[i]
    start = is_ref[i]
    tf = wg_ref.shape[1]
    b_row = ie_ref[i] * n_j + j
    bg = bg_ref[pl.ds(b_row, 1), :]
    bu = bu_ref[pl.ds(b_row, 1), :]
    bd = bd_ref[pl.ds(ie_ref[i], 1), :]

    def in_copy(r0, size):
        g0 = pl.multiple_of(start + r0, ROW_BLOCK)
        return pltpu.make_async_copy(xs_hbm.at[pl.ds(g0, size), :], xbuf.at[pl.ds(r0, size), :], sem_in)

    def out_copy(r0, size):
        g0 = pl.multiple_of(start + r0, ROW_BLOCK)
        return pltpu.make_async_copy(xbuf.at[pl.ds(r0, size), :], ys_hbm.at[pl.ds(g0, size), :], sem_out)

    def for_blocks(fn):
        n_big = rows // MAX_BLOCK

        def body(c, carry):
            fn(pl.multiple_of(c * MAX_BLOCK, MAX_BLOCK), MAX_BLOCK)
            return carry
        lax.fori_loop(0, n_big, body, 0)
        base = n_big * MAX_BLOCK
        size = MAX_BLOCK // 2
        while size >= ROW_BLOCK:
            has = (rows & size) != 0

            @pl.when(has)
            def _(base=base, size=size):
                fn(pl.multiple_of(base, ROW_BLOCK), size)
            base = base + jnp.where(has, size, 0)
            size //= 2

    @pl.when(rows > 0)
    def _():
        @pl.when(j == 0)
        def _():
            for_blocks(lambda r0, size: in_copy(r0, size).start())

        wgu_bf[:, 0:tf] = wg_ref[...].astype(BF16)
        wgu_bf[:, tf:2 * tf] = wu_ref[...].astype(BF16)
        wd_bf[...] = wd_ref[...].astype(BF16)

        @pl.when(j == 0)
        def _():
            for_blocks(lambda r0, size: in_copy(r0, size).wait())

        def block(r0, size, first, last):
            lo, hi = _unpack_bf16_pairs(xbuf[pl.ds(r0, size), :])
            x = jnp.concatenate([lo.astype(BF16), hi.astype(BF16)], axis=1)
            gu = jnp.dot(x, wgu_bf[...], preferred_element_type=F32)
            gg = jnp.minimum(gu[:, 0:tf] + bg, SWIGLU_LIMIT)
            uu = jnp.clip(gu[:, tf:2 * tf] + bu, -SWIGLU_LIMIT, SWIGLU_LIMIT)
            hid = gg * _sigmoid(SWIGLU_ALPHA * gg) * (uu + 1.0)
            y = jnp.dot(hid.astype(BF16), wd_bf[...], preferred_element_type=F32)
            if not first:
                y = y + ybuf[pl.ds(r0, size), :]
            if last:
                xbuf[pl.ds(r0, size), :] = _pack_bf16_pairs(y + bd)
                out_copy(r0, size).start()
            else:
                ybuf[pl.ds(r0, size), :] = y

        @pl.when(j == 0)
        def _():
            for_blocks(lambda r0, size: block(r0, size, True, False))

        @pl.when(jnp.logical_and(j > 0, j < n_j - 1))
        def _():
            for_blocks(lambda r0, size: block(r0, size, False, False))

        @pl.when(j == n_j - 1)
        def _():
            for_blocks(lambda r0, size: block(r0, size, False, True))
            for_blocks(lambda r0, size: out_copy(r0, size).wait())

    @pl.when(jnp.logical_and(i == pl.num_programs(0) - 1, j == n_j - 1))
    def _():
        tail0 = is_ref[pl.num_programs(0)]
        n_tail = (ys_hbm.shape[0] - tail0) // ROW_BLOCK
        xbuf[0:ROW_BLOCK, :] = jnp.zeros((ROW_BLOCK, xbuf.shape[1]), jnp.uint32)

        def tail_copy(c):
            g0 = pl.multiple_of(tail0 + c * ROW_BLOCK, ROW_BLOCK)
            return pltpu.make_async_copy(xbuf.at[pl.ds(0, ROW_BLOCK), :],
                                         ys_hbm.at[pl.ds(g0, ROW_BLOCK), :], sem_out)

        def tail_start(c, carry):
            tail_copy(c).start()
            return carry

        def tail_wait(c, carry):
            tail_copy(c).wait()
            return carry

        lax.fori_loop(0, n_tail, tail_start, 0)
        lax.fori_loop(0, n_tail, tail_wait, 0)


def _experts(xs, item_e, item_start, item_rows, w_gate, w_up, w_down, b_gate, b_up, b_down):
    n_slots, dh = xs.shape
    n_exp, d, d_ff = w_gate.shape
    n_items = item_e.shape[0]
    n_j = d_ff // FF_TILE
    assert n_j >= 2 and d == 2 * dh

    def jj(i, j, ir):
        return jnp.where(ir[i] > 0, j, n_j - 1)

    grid_spec = pltpu.PrefetchScalarGridSpec(
        num_scalar_prefetch=3,
        grid=(n_items, n_j),
        in_specs=[pl.BlockSpec(memory_space=pl.ANY),
                  pl.BlockSpec((None, d, FF_TILE), lambda i, j, ie, is_, ir: (ie[i], 0, jj(i, j, ir))),
                  pl.BlockSpec((None, d, FF_TILE), lambda i, j, ie, is_, ir: (ie[i], 0, jj(i, j, ir))),
                  pl.BlockSpec((None, FF_TILE, d), lambda i, j, ie, is_, ir: (ie[i], jj(i, j, ir), 0)),
                  pl.BlockSpec((n_exp * n_j, FF_TILE), lambda i, j, ie, is_, ir: (0, 0)),
                  pl.BlockSpec((n_exp * n_j, FF_TILE), lambda i, j, ie, is_, ir: (0, 0)),
                  pl.BlockSpec((n_exp, d), lambda i, j, ie, is_, ir: (0, 0))],
        out_specs=pl.BlockSpec(memory_space=pl.ANY),
        scratch_shapes=[pltpu.VMEM((ITEM_ROWS, dh), jnp.uint32),
                        pltpu.VMEM((ITEM_ROWS, d), F32),
                        pltpu.VMEM((d, 2 * FF_TILE), BF16),
                        pltpu.VMEM((FF_TILE, d), BF16),
                        pltpu.SemaphoreType.DMA(()),
                        pltpu.SemaphoreType.DMA(())],
    )
    return pl.pallas_call(
        _expert_kernel,
        grid_spec=grid_spec,
        out_shape=jax.ShapeDtypeStruct((n_slots, dh), jnp.uint32),
        compiler_params=_cparams(("arbitrary", "arbitrary"), 58),
        name="experts",
    )(item_e, item_start, item_rows, xs, w_gate, w_up, w_down,
      b_gate.reshape(n_exp * n_j, FF_TILE), b_up.reshape(n_exp * n_j, FF_TILE), b_down)


def _combine_kernel(*refs, alpha, tm, n_pt):
    dcur_ref = refs[:TOP_K]
    dnext_ref = refs[TOP_K:2 * TOP_K]
    (ys_hbm, x1_ref, gate_ref, p_ref, wpg_ref, wpp_ref, l2g_ref, l2b_ref, l3g_ref, l3b_ref,
     op_ref, os_ref, gbuf_a, gbuf_b, wpgb_ref, sems) = refs[2 * TOP_K:]
    i = pl.program_id(0)
    n_i = pl.num_programs(0)

    @pl.when(i == 0)
    def _():
        wpgb_ref[...] = wpg_ref[...].astype(BF16)

    def row_copy(dref, t, k, buf, sem):
        return pltpu.make_async_copy(ys_hbm.at[pl.ds(dref[k][t], 1), :],
                                     buf.at[k, pl.ds(t, 1), :], sem)

    def gather_loop(dref, buf, sem, wait):
        def body(t, c):
            for k in range(TOP_K):
                cp = row_copy(dref, t, k, buf, sem)
                if wait:
                    cp.wait()
                else:
                    cp.start(priority=k % DMA_THREADS)
            return c
        lax.fori_loop(0, tm, body, 0, unroll=4)

    def step(cur, cur_sem, nxt, nxt_sem):
        @pl.when(i == 0)
        def _():
            gather_loop(dcur_ref, cur, cur_sem, False)

        gather_loop(dcur_ref, cur, cur_sem, True)

        for t in range(tm):
            for k in range(TOP_K):
                row_copy(dnext_ref, t, k, nxt, nxt_sem).start(priority=k % DMA_THREADS)

        x1 = x1_ref[...]
        lane = lax.broadcasted_iota(jnp.int32, gate_ref.shape, 1)
        gates = gate_ref[...]
        half = x1.shape[1] // 2
        ff_lo = jnp.zeros((tm, half), F32)
        ff_hi = jnp.zeros((tm, half), F32)
        for k in range(TOP_K):
            gk = jnp.sum(jnp.where(lane == k, gates, 0.0), axis=1, keepdims=True)
            y_lo, y_hi = _unpack_bf16_pairs(cur[k])
            ff_lo = ff_lo + gk * y_lo
            ff_hi = ff_hi + gk * y_hi
        ff = jnp.concatenate([ff_lo, ff_hi], axis=1)
        x2 = _layer_norm(alpha * x1 + ff, l2g_ref[...], l2b_ref[...])
        eg = _sigmoid(jnp.dot(x2.astype(BF16), wpgb_ref[...], preferred_element_type=F32))
        ep = jnp.dot(p_ref[...].astype(BF16), wpp_ref[...], preferred_element_type=F32)
        out = _layer_norm(alpha * x2 + eg * ep, l3g_ref[...], l3b_ref[...])

        @pl.when(i < n_pt)
        def _():
            op_ref[...] = out

        @pl.when(i >= n_pt)
        def _():
            os_ref[...] = out

        @pl.when(i == n_i - 1)
        def _():
            gather_loop(dnext_ref, nxt, nxt_sem, True)

    @pl.when(lax.rem(i, 2) == 0)
    def _():
        step(gbuf_a, sems.at[0], gbuf_b, sems.at[1])

    @pl.when(lax.rem(i, 2) == 1)
    def _():
        step(gbuf_b, sems.at[1], gbuf_a, sems.at[0])


def _combine(ys, dest, x1, gates, p, w_pg, w_pp, l2_g, l2_b, l3_g, l3_b, alpha, tm, n_prompt):
    n, d = x1.shape
    n_i = n // tm
    n_pt = n_prompt // tm
    assert n_prompt % tm == 0 and n % tm == 0
    row = lambda i: (i, 0)
    fixed = lambda i: (0, 0)
    kern = functools.partial(_combine_kernel, alpha=alpha, tm=tm, n_pt=n_pt)
    return pl.pallas_call(
        kern,
        grid=(n_i,),
        in_specs=[pl.BlockSpec((tm,), lambda i: (i,), memory_space=pltpu.SMEM)] * TOP_K + [
                  pl.BlockSpec((tm,), lambda i: (jnp.minimum(i + 1, n_i - 1),),
                               memory_space=pltpu.SMEM)] * TOP_K + [
                  pl.BlockSpec(memory_space=pl.ANY),
                  pl.BlockSpec((tm, d), row),
                  pl.BlockSpec((tm, LANES), row),
                  pl.BlockSpec((tm, p.shape[1]), row),
                  pl.BlockSpec(w_pg.shape, fixed, pipeline_mode=pl.Buffered(1)),
                  pl.BlockSpec(w_pp.shape, fixed),
                  pl.BlockSpec((1, d), fixed), pl.BlockSpec((1, d), fixed),
                  pl.BlockSpec((1, d), fixed), pl.BlockSpec((1, d), fixed)],
        out_specs=[pl.BlockSpec((tm, d), lambda i: (jnp.minimum(i, n_pt - 1), 0)),
                   pl.BlockSpec((tm, d), lambda i: (jnp.maximum(i - n_pt, 0), 0))],
        out_shape=[jax.ShapeDtypeStruct((n_prompt, d), F32),
                   jax.ShapeDtypeStruct((n - n_prompt, d), F32)],
        scratch_shapes=[pltpu.VMEM((TOP_K, tm, d // 2), jnp.uint32),
                        pltpu.VMEM((TOP_K, tm, d // 2), jnp.uint32),
                        pltpu.VMEM(w_pg.shape, BF16), pltpu.SemaphoreType.DMA((2,))],
        compiler_params=_cparams(("arbitrary",), 48),
        name="combine",
    )(*dest, *dest, ys, x1, gates, p, w_pg, w_pp, l2_g, l2_b, l3_g, l3_b)


def _routing_tables(expert_idx, rank, counts, n_items):
    n_exp = counts.shape[0]
    padded = (counts + ROW_BLOCK - 1) // ROW_BLOCK * ROW_BLOCK
    pad_end = jnp.cumsum(padded)
    pad_start = pad_end - padded
    experts = jnp.arange(n_exp, dtype=expert_idx.dtype)[:, None, None]
    group0 = jnp.sum(jnp.where(expert_idx[None] == experts, pad_start[:, None, None], 0), axis=0)
    dest = (group0 + rank).astype(jnp.int32)
    dest = tuple(dest[k] for k in range(TOP_K))
    items_per = (padded + ITEM_ROWS - 1) // ITEM_ROWS
    item_end = jnp.cumsum(items_per)
    ids = jnp.arange(n_items, dtype=jnp.int32)
    total = item_end[-1]
    last_valid = jnp.maximum(total - 1, 0)
    eff = jnp.minimum(ids, last_valid)
    e_of = jnp.minimum(jnp.searchsorted(item_end, eff, side='right'), n_exp - 1).astype(jnp.int32)
    sub = eff - (item_end[e_of] - items_per[e_of])
    start = pad_start[e_of] + sub * ITEM_ROWS
    rows = jnp.clip(padded[e_of] - sub * ITEM_ROWS, 0, ITEM_ROWS)
    rows = jnp.where(ids < total, rows, 0)
    start = jnp.concatenate([start, pad_end[-1:]])
    return (dest, (pad_start + counts).astype(jnp.int32), pad_end.astype(jnp.int32),
            e_of, start.astype(jnp.int32), rows.astype(jnp.int32))


def kernel(x_prompt, x_sample, state_conv, state_rec, p_prompt, p_sample, ln_in_g, ln_in_b, w_in, conv_w,
           lb_theta, rms_g, w_out, ln1_g, ln1_b, w_router, b_router, w_gate, b_gate, w_up, b_up, w_down,
           b_down, ln2_g, ln2_b, w_ple_gate, w_ple_proj, ln3_g, ln3_b):
    bp, sp, d = x_prompt.shape
    bs, ss, _ = x_sample.shape
    depth = w_in.shape[0]
    assert depth == 1 and ss == 1 and sp % CHUNK == 0
    d_conv = state_conv.shape[-1]
    n_exp = w_router.shape[-1]
    n_p = bp * sp
    n = n_p + bs
    alpha = (2 * depth) ** 0.25
    row2 = lambda a: a.reshape(1, -1)

    lb = jnp.cumsum(jax.nn.softmax(lb_theta.astype(F32), axis=0), axis=0)[0]
    x_p = x_prompt.reshape(n_p, d)
    x_s = x_sample.reshape(bs, d)
    p_all = jnp.concatenate([p_prompt[0].reshape(n_p, -1), p_sample[0].reshape(bs, -1)], axis=0)

    xn_p = _ln_in(x_p, row2(ln_in_g), row2(ln_in_b), tm=512)
    proj_p = _in_proj(xn_p, w_in[0], tm=1024, tn=1024)
    proj_s = _in_proj_sample(x_s, row2(ln_in_g), row2(ln_in_b), w_in[0], tn=1024)

    mix_p, conv_tail, rec_p = _mix_prompt(proj_p, conv_w[0], row2(lb), row2(rms_g[0]), bp, sp, d_conv)
    mix_s, conv_s, rec_s = _mix_sample(proj_s, state_conv[0].reshape(bs, -1), state_rec[0], conv_w[0],
                                       row2(lb), row2(rms_g[0]), d_conv)

    w_r = jnp.zeros((d, LANES), F32).at[:, :n_exp].set(w_router[0])
    b_r = jnp.full((1, LANES), NEG_BIG, F32).at[0, :n_exp].set(b_router[0])
    w_r_hi = w_r.astype(BF16)
    w_r2 = jnp.concatenate([w_r_hi, (w_r - w_r_hi.astype(F32)).astype(BF16)], axis=1)
    x1, x1p, route, gates, counts = _post_mix(
        mix_p, mix_s, x_p, x_s, w_out[0], row2(ln_in_g), row2(ln_in_b),
        row2(ln1_g[0]), row2(ln1_b[0]), w_r, w_r2, b_r, alpha, tm=128)

    n_slots = n * TOP_K + n_exp * ROW_BLOCK
    n_items = n_exp + n_slots // ITEM_ROWS
    dest, pad_lo, pad_hi, item_e, item_start, item_rows = _routing_tables(
        route[0:TOP_K], route[TOP_K:2 * TOP_K], counts[0, :n_exp], n_items)

    xs = _dispatch(x1p, dest, pad_lo, pad_hi, n_slots, tm=128)
    ys = _experts(xs, item_e, item_start, item_rows, w_gate[0], w_up[0], w_down[0],
                  b_gate[0], b_up[0], b_down[0])
    y_p, y_s = _combine(ys, dest, x1, gates, p_all, w_ple_gate[0], w_ple_proj[0].astype(BF16),
                        row2(ln2_g[0]), row2(ln2_b[0]), row2(ln3_g[0]), row2(ln3_b[0]), alpha, tm=128,
                        n_prompt=n_p)

    return (y_p.reshape(bp, sp, d),
            y_s.reshape(bs, ss, d),
            conv_tail[:, SUBLANES - (CONV_W - 1):, :][None],
            rec_p[None],
            conv_s.reshape(bs, CONV_W - 1, d_conv)[None],
            rec_s[None])
```

```python
import functools

import numpy as np
import jax
import jax.numpy as jnp
from jax import lax
from jax.experimental import pallas as pl
from jax.experimental.pallas import tpu as pltpu

F32 = jnp.float32
BF16 = jnp.bfloat16
HIGHEST = lax.Precision.HIGHEST

CONV_W = 3
N_HEADS = 8
HEAD_K = 128
HEAD_V = 128
TOP_K = 4
SWIGLU_LIMIT = 7.0
SWIGLU_ALPHA = 1.702
LN_EPS = 1e-5
RMS_EPS = 1e-6

LANES = 128
SUBLANES = 8
VMEM_PHYSICAL_BYTES = 64 * 1024 * 1024
DMA_THREADS = 2

CHUNK = 128
SEQS_PER_STEP = 2
SAMPLE_BLOCK = 16
ROW_BLOCK = 128
MAX_BLOCK = 512
ITEM_ROWS = 1280
FF_TILE = 512
NEG_BIG = -1e30


def _cparams(sem, vmem_mb):
    return pltpu.CompilerParams(dimension_semantics=sem, vmem_limit_bytes=vmem_mb * 1024 * 1024)


def _layer_norm(x, g, b):
    mu = jnp.mean(x, axis=-1, keepdims=True)
    xc = x - mu
    var = jnp.mean(xc * xc, axis=-1, keepdims=True)
    return xc * lax.rsqrt(var + LN_EPS) * g + b


def _sigmoid(x):
    return 1.0 / (1.0 + jnp.exp(-x))


def _pack_bf16_pairs(x):
    half = x.shape[1] // 2
    bits = pltpu.bitcast(x.astype(BF16).astype(F32), jnp.uint32)
    return (bits[:, half:] & jnp.uint32(0xFFFF0000)) | (bits[:, :half] >> 16)


def _unpack_bf16_pairs(u):
    return (pltpu.bitcast(u << 16, F32), pltpu.bitcast(u & jnp.uint32(0xFFFF0000), F32))


def _split3(x, axis):
    p1 = x.astype(BF16)
    r1 = x - p1.astype(F32)
    p2 = r1.astype(BF16)
    p3 = (r1 - p2.astype(F32)).astype(BF16)
    return jnp.concatenate([p1, p2, p3], axis=axis)


def _ln_in_kernel(x_ref, g_ref, b_ref, o_ref):
    o_ref[...] = _layer_norm(x_ref[...], g_ref[...], b_ref[...]).astype(BF16)


def _ln_in(x, g, b, tm):
    n, d = x.shape
    return pl.pallas_call(
        _ln_in_kernel,
        grid=(n // tm,),
        in_specs=[pl.BlockSpec((tm, d), lambda i: (i, 0)),
                  pl.BlockSpec((1, d), lambda i: (0, 0)),
                  pl.BlockSpec((1, d), lambda i: (0, 0))],
        out_specs=pl.BlockSpec((tm, d), lambda i: (i, 0)),
        out_shape=jax.ShapeDtypeStruct((n, d), BF16),
        compiler_params=_cparams(("parallel",), 40),
        name="ln_in",
    )(x, g, b)


def _matmul_kernel(x_ref, w_ref, o_ref, wb_ref):
    @pl.when(pl.program_id(1) == 0)
    def _():
        wb_ref[...] = w_ref[...].astype(BF16)

    o_ref[...] = jnp.dot(x_ref[...], wb_ref[...], preferred_element_type=F32)


def _in_proj(xn, w, tm, tn):
    n, d = xn.shape
    d_in = w.shape[1]
    return pl.pallas_call(
        _matmul_kernel,
        grid=(d_in // tn, n // tm),
        in_specs=[pl.BlockSpec((tm, d), lambda j, i: (i, 0)),
                  pl.BlockSpec((d, tn), lambda j, i: (0, j))],
        out_specs=pl.BlockSpec((tm, tn), lambda j, i: (i, j)),
        out_shape=jax.ShapeDtypeStruct((n, d_in), F32),
        scratch_shapes=[pltpu.VMEM((d, tn), BF16)],
        compiler_params=_cparams(("arbitrary", "arbitrary"), 48),
        name="in_proj",
    )(xn, w)


def _in_proj_sample_kernel(x_ref, g_ref, b_ref, w_ref, o_ref):
    xn = _layer_norm(x_ref[...], g_ref[...], b_ref[...])
    o_ref[...] = jnp.dot(xn, w_ref[...], precision=HIGHEST, preferred_element_type=F32)


def _in_proj_sample(x, g, b, w, tn):
    n, d = x.shape
    d_in = w.shape[1]
    return pl.pallas_call(
        _in_proj_sample_kernel,
        grid=(d_in // tn,),
        in_specs=[pl.BlockSpec((n, d), lambda j: (0, 0)),
                  pl.BlockSpec((1, d), lambda j: (0, 0)),
                  pl.BlockSpec((1, d), lambda j: (0, 0)),
                  pl.BlockSpec((d, tn), lambda j: (0, j))],
        out_specs=pl.BlockSpec((n, tn), lambda j: (0, j)),
        out_shape=jax.ShapeDtypeStruct((n, d_in), F32),
        compiler_params=_cparams(("parallel",), 40),
        name="in_proj_sample",
    )(x, g, b, w)


def _forget_gates(fz, lb):
    e = jnp.exp(-jnp.abs(fz))
    r = 1.0 / (1.0 + e)
    er = e * r
    pos = fz >= 0
    sig_p = jnp.where(pos, r, er)
    sig_n = jnp.where(pos, er, r)
    oml = 1.0 - lb
    return lb + oml * sig_p, oml * sig_n


def _chunk_matrices(c):
    t = np.arange(c)[:, None]
    j = np.arange(c)[None, :]
    mats = [(j <= t), (j > t)]
    blk = c
    while blk >= 2:
        half = blk // 2
        mid = (t // blk) * blk + half
        second = (t % blk) >= half
        m_q = (j >= mid) & (j <= t)
        m_k = (j > t) & (j < mid)
        mats.append(np.where(second, m_q, m_k))
        blk = half
    return np.concatenate(mats, axis=0).astype(np.float32)


def _mix_prompt_kernel(proj_ref, convw_ref, lb_ref, rmsg_ref, cmat_ref,
                       mix_ref, convst_ref, recst_ref, s_ref, carry_ref, *, d_conv):
    tb = pl.program_id(1)

    @pl.when(tb == 0)
    def _():
        s_ref[...] = jnp.zeros_like(s_ref)
        carry_ref[...] = jnp.zeros_like(carry_ref)

    for s in range(proj_ref.shape[0]):
        _mix_prompt_chunk(proj_ref.at[s], convw_ref, lb_ref, rmsg_ref, cmat_ref, mix_ref.at[s],
                          convst_ref.at[s], s_ref.at[s], carry_ref.at[s], d_conv=d_conv)

    @pl.when(tb == pl.num_programs(1) - 1)
    def _():
        recst_ref[...] = s_ref[...]


def _mix_prompt_chunk(proj_ref, convw_ref, lb_ref, rmsg_ref, cmat_ref, mix_ref, convst_ref, s_ref,
                      carry_ref, *, d_conv):
    c = CHUNK
    u = proj_ref[:, 0:d_conv] * proj_ref[:, 2 * d_conv:3 * d_conv]
    row = lax.broadcasted_iota(jnp.int32, u.shape, 0)
    prev1 = carry_ref[SUBLANES - 1:SUBLANES, :]
    prev2 = carry_ref[SUBLANES - 2:SUBLANES - 1, :]
    u1 = jnp.where(row == 0, prev1, pltpu.roll(u, 1, 0))
    u2 = jnp.where(row == 0, prev2, jnp.where(row == 1, prev1, pltpu.roll(u, 2, 0)))
    y = convw_ref[0:1, :] * u2 + convw_ref[1:2, :] * u1 + convw_ref[2:3, :] * u
    mix_ref[:, 0:d_conv] = (proj_ref[:, d_conv:2 * d_conv] * y).astype(BF16)
    carry_ref[...] = u[c - SUBLANES:c, :]
    convst_ref[...] = u[c - SUBLANES:c, :]

    o0 = 3 * d_conv
    d_rec = N_HEADS * HEAD_K
    q = proj_ref[:, o0:o0 + d_rec]
    fz = proj_ref[:, o0 + d_rec:o0 + 2 * d_rec]
    v = proj_ref[:, o0 + 2 * d_rec:o0 + 3 * d_rec]
    g = proj_ref[:, o0 + 3 * d_rec:o0 + 4 * d_rec]
    f, kk = _forget_gates(fz, lb_ref[...])
    contract0 = (((0,), (0,)), ((), ()))
    logf3 = _split3(jnp.log(f), axis=0)
    ex = jnp.dot(cmat_ref[...], logf3, preferred_element_type=F32)
    b_cum = ex[0:c]
    d_end = ex[c:2 * c]
    n_lev = cmat_ref.shape[0] // c - 2
    b_cols = lax.dot_general(logf3, jnp.ones((3 * c, HEAD_V), BF16), contract0,
                             preferred_element_type=F32)

    trow = lax.broadcasted_iota(jnp.int32, (c, c), 0)
    tcol = lax.broadcasted_iota(jnp.int32, (c, c), 1)
    prow = lax.broadcasted_iota(jnp.int32, (c, HEAD_K), 0)
    contract1 = (((1,), (1,)), ((), ()))

    for h in range(N_HEADS):
        sl = slice(h * HEAD_K, (h + 1) * HEAD_K)
        qh, kh, vh = q[:, sl], kk[:, sl], v[:, sl]
        vb = vh.astype(BF16)
        s_old = s_ref[h]
        o = jnp.dot((qh * jnp.exp(b_cum[:, sl])).astype(BF16), s_old.astype(BF16),
                    preferred_element_type=F32)
        sc = jnp.zeros((c, c), F32)
        for lev in range(n_lev):
            blk = c >> lev
            sh = blk.bit_length() - 1
            dl = jnp.exp(ex[(2 + lev) * c:(3 + lev) * c, sl])
            second = (prow & (blk - 1)) >= (blk // 2)
            qt = jnp.where(second, qh * dl, 0.0).astype(BF16)
            kt = jnp.where(second, 0.0, kh * dl).astype(BF16)
            s_l = lax.dot_general(qt, kt, contract1, preferred_element_type=F32)
            sc = sc + jnp.where((trow >> sh) == (tcol >> sh), s_l, 0.0)
        o = o + jnp.dot(sc.astype(BF16), vb, preferred_element_type=F32)
        o = o + jnp.sum(qh * kh, axis=1, keepdims=True) * vh
        khat = (kh * jnp.exp(d_end[:, sl])).astype(BF16)
        upd = lax.dot_general(khat, vb, contract0, preferred_element_type=F32)
        s_ref[h] = jnp.exp(b_cols[sl, :]) * s_old + upd
        on = o * lax.rsqrt(jnp.mean(o * o, axis=1, keepdims=True) + RMS_EPS) * rmsg_ref[:, sl]
        gh = g[:, sl]
        mix_ref[:, d_conv + h * HEAD_V:d_conv + (h + 1) * HEAD_V] = (
            on * (gh * _sigmoid(gh))).astype(BF16)


def _mix_prompt(proj, conv_w, lb, rms_g, bsz, seq, d_conv):
    d_in = proj.shape[1]
    d_mix = d_conv + N_HEADS * HEAD_V
    n_tb = seq // CHUNK
    ns = SEQS_PER_STEP
    assert bsz % ns == 0
    cmat = jnp.asarray(np.tile(_chunk_matrices(CHUNK), (1, 3)), dtype=BF16)
    kern = functools.partial(_mix_prompt_kernel, d_conv=d_conv)
    mix, conv_tail, rec = pl.pallas_call(
        kern,
        grid=(bsz // ns, n_tb),
        in_specs=[pl.BlockSpec((ns, CHUNK, d_in), lambda b, t: (b, t, 0)),
                  pl.BlockSpec((CONV_W, d_conv), lambda b, t: (0, 0)),
                  pl.BlockSpec((1, N_HEADS * HEAD_K), lambda b, t: (0, 0)),
                  pl.BlockSpec((1, N_HEADS * HEAD_V), lambda b, t: (0, 0)),
                  pl.BlockSpec(cmat.shape, lambda b, t: (0, 0))],
        out_specs=[pl.BlockSpec((ns, CHUNK, d_mix), lambda b, t: (b, t, 0)),
                   pl.BlockSpec((ns, SUBLANES, d_conv), lambda b, t: (b, 0, 0)),
                   pl.BlockSpec((ns, N_HEADS, HEAD_K, HEAD_V), lambda b, t: (b, 0, 0, 0))],
        out_shape=[jax.ShapeDtypeStruct((bsz, seq, d_mix), BF16),
                   jax.ShapeDtypeStruct((bsz, SUBLANES, d_conv), F32),
                   jax.ShapeDtypeStruct((bsz, N_HEADS, HEAD_K, HEAD_V), F32)],
        scratch_shapes=[pltpu.VMEM((ns, N_HEADS, HEAD_K, HEAD_V), F32),
                        pltpu.VMEM((ns, SUBLANES, d_conv), F32)],
        compiler_params=_cparams(("parallel", "arbitrary"), 48),
        name="mix_prompt",
    )(proj.reshape(bsz, seq, d_in), conv_w, lb, rms_g, cmat)
    return mix.reshape(bsz * seq, d_mix), conv_tail, rec


def _mix_sample_kernel(proj_ref, cst_ref, rst_ref, convw_ref, lb_ref, rmsg_ref, sel_ref,
                       mix_ref, cnew_ref, rnew_ref, *, d_conv):
    nb = SAMPLE_BLOCK
    u = proj_ref[:, 0:d_conv] * proj_ref[:, 2 * d_conv:3 * d_conv]
    buf0 = cst_ref[:, 0:d_conv]
    buf1 = cst_ref[:, d_conv:2 * d_conv]
    y = convw_ref[0:1, :] * buf0 + convw_ref[1:2, :] * buf1 + convw_ref[2:3, :] * u
    mix_ref[:, 0:d_conv] = proj_ref[:, d_conv:2 * d_conv] * y
    cnew_ref[:, 0:d_conv] = buf1
    cnew_ref[:, d_conv:2 * d_conv] = u

    o0 = 3 * d_conv
    d_rec = N_HEADS * HEAD_K
    q = proj_ref[:, o0:o0 + d_rec]
    fz = proj_ref[:, o0 + d_rec:o0 + 2 * d_rec]
    v = proj_ref[:, o0 + 2 * d_rec:o0 + 3 * d_rec]
    g = proj_ref[:, o0 + 3 * d_rec:o0 + 4 * d_rec]
    f, kk = _forget_gates(fz, lb_ref[...])
    contract0 = (((0,), (0,)), ((), ()))
    sel = sel_ref[...]
    row = lax.broadcasted_iota(jnp.int32, (nb, HEAD_V), 0)

    def columns(a):
        return lax.dot_general(_split3(a, axis=0), sel, contract0, preferred_element_type=F32)

    for h in range(N_HEADS):
        sl = slice(h * HEAD_K, (h + 1) * HEAD_K)
        f_c, k_c, q_c = columns(f[:, sl]), columns(kk[:, sl]), columns(q[:, sl])
        o = jnp.zeros((nb, HEAD_V), F32)
        for n in range(nb):
            nl = slice(n * HEAD_V, (n + 1) * HEAD_V)
            s_new = f_c[:, nl] * rst_ref[n, h] + k_c[:, nl] * v[n:n + 1, sl]
            rnew_ref[n, h] = s_new
            o_row = jnp.sum(q_c[:, nl] * s_new, axis=0, keepdims=True)
            o = jnp.where(row == n, o_row, o)
        on = o * lax.rsqrt(jnp.mean(o * o, axis=1, keepdims=True) + RMS_EPS) * rmsg_ref[:, sl]
        gh = g[:, sl]
        mix_ref[:, d_conv + h * HEAD_V:d_conv + (h + 1) * HEAD_V] = on * (gh * _sigmoid(gh))


def _mix_sample(proj, conv_state, rec_state, conv_w, lb, rms_g, d_conv):
    n_seq = conv_state.shape[0]
    d_in = proj.shape[1]
    d_mix = d_conv + N_HEADS * HEAD_V
    nb = SAMPLE_BLOCK
    sel = jnp.asarray(np.tile(np.kron(np.eye(nb), np.ones((1, HEAD_V))), (3, 1)), dtype=BF16)
    kern = functools.partial(_mix_sample_kernel, d_conv=d_conv)
    return pl.pallas_call(
        kern,
        grid=(n_seq // nb,),
        in_specs=[pl.BlockSpec((nb, d_in), lambda i: (i, 0)),
                  pl.BlockSpec((nb, 2 * d_conv), lambda i: (i, 0)),
                  pl.BlockSpec((nb, N_HEADS, HEAD_K, HEAD_V), lambda i: (i, 0, 0, 0)),
                  pl.BlockSpec((CONV_W, d_conv), lambda i: (0, 0)),
                  pl.BlockSpec((1, N_HEADS * HEAD_K), lambda i: (0, 0)),
                  pl.BlockSpec((1, N_HEADS * HEAD_V), lambda i: (0, 0)),
                  pl.BlockSpec(sel.shape, lambda i: (0, 0))],
        out_specs=[pl.BlockSpec((nb, d_mix), lambda i: (i, 0)),
                   pl.BlockSpec((nb, 2 * d_conv), lambda i: (i, 0)),
                   pl.BlockSpec((nb, N_HEADS, HEAD_K, HEAD_V), lambda i: (i, 0, 0, 0))],
        out_shape=[jax.ShapeDtypeStruct((n_seq, d_mix), F32),
                   jax.ShapeDtypeStruct((n_seq, 2 * d_conv), F32),
                   jax.ShapeDtypeStruct(rec_state.shape, F32)],
        compiler_params=_cparams(("parallel",), 52),
        name="mix_sample",
    )(proj, conv_state, rec_state, conv_w, lb, rms_g, sel)


def _post_mix_kernel(mixp_ref, mixs_ref, xp_ref, xs_ref, woutf_ref, ling_ref, linb_ref,
                     l1g_ref, l1b_ref, wr_ref, wr2_ref, br_ref,
                     x1_ref, x1p_ref, route_ref, gate_ref, cnt_ref, run_ref, h_ref, lg_ref, woutb_ref,
                     *, alpha, n_pt):
    i = pl.program_id(0)

    @pl.when(i == 0)
    def _():
        run_ref[...] = jnp.zeros_like(run_ref)
        woutb_ref[...] = woutf_ref[...].astype(BF16)

    @pl.when(i < n_pt)
    def _():
        h_ref[...] = jnp.dot(mixp_ref[...], woutb_ref[...], preferred_element_type=F32)

    @pl.when(i >= n_pt)
    def _():
        h_ref[...] = jnp.dot(mixs_ref[...], woutf_ref[...], precision=HIGHEST,
                             preferred_element_type=F32)

    x = jnp.where(i < n_pt, xp_ref[...], xs_ref[...])
    xn = _layer_norm(x, ling_ref[...], linb_ref[...])
    x1 = _layer_norm(alpha * xn + h_ref[...], l1g_ref[...], l1b_ref[...])
    x1_ref[...] = x1
    x1p_ref[...] = _pack_bf16_pairs(x1)

    tm = x1.shape[0]

    @pl.when(i < n_pt)
    def _():
        xh = x1.astype(BF16)
        xl = (x1 - xh.astype(F32)).astype(BF16)
        pr = jnp.dot(jnp.concatenate([xh, xl], axis=0), wr2_ref[...], preferred_element_type=F32)
        lg_ref[...] = (pr[0:tm, 0:LANES] + pr[0:tm, LANES:2 * LANES]
                       + pr[tm:2 * tm, 0:LANES] + pr[tm:2 * tm, LANES:2 * LANES])

    @pl.when(i >= n_pt)
    def _():
        lg_ref[...] = jnp.dot(x1, wr_ref[...], precision=HIGHEST, preferred_element_type=F32)

    logits = lg_ref[...] + br_ref[...]
    lane = lax.broadcasted_iota(jnp.int32, (tm, LANES), 1)
    lane_f = lane.astype(F32)
    work = logits
    vals, idxs = [], []
    for _ in range(TOP_K):
        m = jnp.max(work, axis=1, keepdims=True)
        ix = jnp.min(jnp.where(work == m, lane_f, float(LANES)), axis=1, keepdims=True)
        vals.append(m)
        idxs.append(ix)
        work = jnp.where(lane_f == ix, NEG_BIG, work)
    ex = [jnp.exp(vv - vals[0]) for vv in vals]
    den = ex[0] + ex[1] + ex[2] + ex[3]
    onehots = [(lane_f == ix).astype(F32) for ix in idxs]
    oh = onehots[0] + onehots[1] + onehots[2] + onehots[3]
    tr = lax.broadcasted_iota(jnp.int32, (tm, tm), 0)
    tc = lax.broadcasted_iota(jnp.int32, (tm, tm), 1)
    before = jnp.dot((tc < tr).astype(BF16), oh.astype(BF16), preferred_element_type=F32)
    pos = before + run_ref[...]
    route = jnp.zeros((tm, LANES), F32)
    gates = jnp.zeros((tm, LANES), F32)
    for k in range(TOP_K):
        rank = jnp.sum(onehots[k] * pos, axis=1, keepdims=True)
        route = jnp.where(lane == k, idxs[k], route)
        route = jnp.where(lane == TOP_K + k, rank, route)
        gates = jnp.where(lane == k, ex[k] / den, gates)
    route_ref[...] = route.T[0:2 * TOP_K, :].astype(jnp.int32)
    gate_ref[...] = gates
    run_ref[...] = run_ref[...] + jnp.sum(oh, axis=0, keepdims=True)
    cnt_ref[...] = run_ref[...].astype(jnp.int32)


def _post_mix(mix_p, mix_s, x_p, x_s, w_out_f, lin_g, lin_b, l1_g, l1_b, w_r, w_r2, b_r, alpha, tm):
    d = x_p.shape[1]
    n = x_p.shape[0] + x_s.shape[0]
    d_mix = mix_p.shape[1]
    n_pt = mix_p.shape[0] // tm
    assert mix_p.shape[0] % tm == 0 and mix_s.shape[0] % tm == 0
    row = lambda i: (i, 0)
    fixed = lambda i: (0, 0)
    prompt_row = lambda i: (jnp.minimum(i, n_pt - 1), 0)
    sample_row = lambda i: (jnp.maximum(i - n_pt, 0), 0)
    once = pl.Buffered(1)
    kern = functools.partial(_post_mix_kernel, alpha=alpha, n_pt=n_pt)
    return pl.pallas_call(
        kern,
        grid=(n // tm,),
        in_specs=[pl.BlockSpec((tm, d_mix), prompt_row),
                  pl.BlockSpec((tm, d_mix), sample_row),
                  pl.BlockSpec((tm, d), prompt_row),
                  pl.BlockSpec((tm, d), sample_row),
                  pl.BlockSpec(w_out_f.shape, fixed, pipeline_mode=once),
                  pl.BlockSpec((1, d), fixed), pl.BlockSpec((1, d), fixed),
                  pl.BlockSpec((1, d), fixed), pl.BlockSpec((1, d), fixed),
                  pl.BlockSpec((d, LANES), fixed), pl.BlockSpec((d, 2 * LANES), fixed),
                  pl.BlockSpec((1, LANES), fixed)],
        out_specs=[pl.BlockSpec((tm, d), row),
                   pl.BlockSpec((tm, d // 2), row),
                   pl.BlockSpec((2 * TOP_K, tm), lambda i: (0, i)),
                   pl.BlockSpec((tm, LANES), row),
                   pl.BlockSpec((1, LANES), fixed)],
        out_shape=[jax.ShapeDtypeStruct((n, d), F32),
                   jax.ShapeDtypeStruct((n, d // 2), jnp.uint32),
                   jax.ShapeDtypeStruct((2 * TOP_K, n), jnp.int32),
                   jax.ShapeDtypeStruct((n, LANES), F32),
                   jax.ShapeDtypeStruct((1, LANES), jnp.int32)],
        scratch_shapes=[pltpu.VMEM((1, LANES), F32), pltpu.VMEM((tm, d), F32),
                        pltpu.VMEM((tm, LANES), F32), pltpu.VMEM(w_out_f.shape, BF16)],
        compiler_params=_cparams(("arbitrary",), 48),
        name="post_mix",
    )(mix_p, mix_s, x_p, x_s, w_out_f, lin_g, lin_b, l1_g, l1_b, w_r, w_r2, b_r)


def _dispatch_kernel(*refs, tm, n_experts):
    dest_refs = refs[:TOP_K]
    padlo_ref, padhi_ref, x_ref, xs_hbm, zero_ref, sem = refs[TOP_K:]
    i = pl.program_id(0)

    def row_copy(src, dst_row):
        return pltpu.make_async_copy(src, xs_hbm.at[pl.ds(dst_row, 1), :], sem)

    @pl.when(i == 0)
    def _():
        zero_ref[...] = jnp.zeros_like(zero_ref)

        def per_expert(e, carry):
            def start(r, c):
                row_copy(zero_ref.at[pl.ds(0, 1), :], r).start()
                return c

            def wait(r, c):
                row_copy(zero_ref.at[pl.ds(0, 1), :], r).wait()
                return c

            lax.fori_loop(padlo_ref[e], padhi_ref[e], start, 0)
            lax.fori_loop(padlo_ref[e], padhi_ref[e], wait, 0)
            return carry

        lax.fori_loop(0, n_experts, per_expert, 0)

        tail0 = padhi_ref[n_experts - 1]
        n_tail = (xs_hbm.shape[0] - tail0) // ROW_BLOCK

        def tail_copy(c):
            r0 = pl.multiple_of(tail0 + c * ROW_BLOCK, ROW_BLOCK)
            return pltpu.make_async_copy(zero_ref, xs_hbm.at[pl.ds(r0, ROW_BLOCK), :], sem)

        def tail_start(c, carry):
            tail_copy(c).start()
            return carry

        def tail_wait(c, carry):
            tail_copy(c).wait()
            return carry

        lax.fori_loop(0, n_tail, tail_start, 0)
        lax.fori_loop(0, n_tail, tail_wait, 0)

    copies = [row_copy(x_ref.at[pl.ds(t, 1), :], dest_refs[k][t]) for t in range(tm) for k in range(TOP_K)]
    for n, cp in enumerate(copies):
        cp.start(priority=n % DMA_THREADS)
    for cp in copies:
        cp.wait()


def _dispatch(x1p, dest, pad_lo, pad_hi, n_slots, tm):
    n, dh = x1p.shape
    n_experts = pad_lo.shape[0]
    kern = functools.partial(_dispatch_kernel, tm=tm, n_experts=n_experts)
    return pl.pallas_call(
        kern,
        grid=(n // tm,),
        in_specs=[pl.BlockSpec((tm,), lambda i: (i,), memory_space=pltpu.SMEM)] * TOP_K + [
                  pl.BlockSpec(memory_space=pltpu.SMEM),
                  pl.BlockSpec(memory_space=pltpu.SMEM),
                  pl.BlockSpec((tm, dh), lambda i: (i, 0))],
        out_specs=pl.BlockSpec(memory_space=pl.ANY),
        out_shape=jax.ShapeDtypeStruct((n_slots, dh), jnp.uint32),
        scratch_shapes=[pltpu.VMEM((ROW_BLOCK, dh), jnp.uint32), pltpu.SemaphoreType.DMA(())],
        compiler_params=_cparams(("arbitrary",), 32),
        name="dispatch",
    )(*dest, pad_lo, pad_hi, x1p)


def _expert_kernel(ie_ref, is_ref, ir_ref, xs_hbm, wg_ref, wu_ref, wd_ref, bg_ref, bu_ref, bd_ref,
                   ys_hbm, xbuf, ybuf, wgu_bf, wd_bf, sem_in, sem_out):
    i = pl.program_id(0)
    j = pl.program_id(1)
    n_j = pl.num_programs(1)
    rows = ir_ref[i]
    start = is_ref[i]
    tf = wg_ref.shape[1]
    b_row = ie_ref[i] * n_j + j
    bg = bg_ref[pl.ds(b_row, 1), :]
    bu = bu_ref[pl.ds(b_row, 1), :]
    bd = bd_ref[pl.ds(ie_ref[i], 1), :]

    def in_copy(r0, size):
        g0 = pl.multiple_of(start + r0, ROW_BLOCK)
        return pltpu.make_async_copy(xs_hbm.at[pl.ds(g0, size), :], xbuf.at[pl.ds(r0, size), :], sem_in)

    def out_copy(r0, size):
        g0 = pl.multiple_of(start + r0, ROW_BLOCK)
        return pltpu.make_async_copy(xbuf.at[pl.ds(r0, size), :], ys_hbm.at[pl.ds(g0, size), :], sem_out)

    def for_blocks(fn):
        n_big = rows // MAX_BLOCK

        def body(c, carry):
            fn(pl.multiple_of(c * MAX_BLOCK, MAX_BLOCK), MAX_BLOCK)
            return carry
        lax.fori_loop(0, n_big, body, 0)
        base = n_big * MAX_BLOCK
        size = MAX_BLOCK // 2
        while size >= ROW_BLOCK:
            has = (rows & size) != 0

            @pl.when(has)
            def _(base=base, size=size):
                fn(pl.multiple_of(base, ROW_BLOCK), size)
            base = base + jnp.where(has, size, 0)
            size //= 2

    @pl.when(rows > 0)
    def _():
        @pl.when(j == 0)
        def _():
            for_blocks(lambda r0, size: in_copy(r0, size).start())

        wgu_bf[:, 0:tf] = wg_ref[...].astype(BF16)
        wgu_bf[:, tf:2 * tf] = wu_ref[...].astype(BF16)
        wd_bf[...] = wd_ref[...].astype(BF16)

        @pl.when(j == 0)
        def _():
            for_blocks(lambda r0, size: in_copy(r0, size).wait())

        def block(r0, size, first, last):
            lo, hi = _unpack_bf16_pairs(xbuf[pl.ds(r0, size), :])
            x = jnp.concatenate([lo.astype(BF16), hi.astype(BF16)], axis=1)
            gu = jnp.dot(x, wgu_bf[...], preferred_element_type=F32)
            gg = jnp.minimum(gu[:, 0:tf] + bg, SWIGLU_LIMIT)
            uu = jnp.clip(gu[:, tf:2 * tf] + bu, -SWIGLU_LIMIT, SWIGLU_LIMIT)
            hid = gg * _sigmoid(SWIGLU_ALPHA * gg) * (uu + 1.0)
            y = jnp.dot(hid.astype(BF16), wd_bf[...], preferred_element_type=F32)
            if not first:
                y = y + ybuf[pl.ds(r0, size), :]
            if last:
                xbuf[pl.ds(r0, size), :] = _pack_bf16_pairs(y + bd)
                out_copy(r0, size).start()
            else:
                ybuf[pl.ds(r0, size), :] = y

        @pl.when(j == 0)
        def _():
            for_blocks(lambda r0, size: block(r0, size, True, False))

        @pl.when(jnp.logical_and(j > 0, j < n_j - 1))
        def _():
            for_blocks(lambda r0, size: block(r0, size, False, False))

        @pl.when(j == n_j - 1)
        def _():
            for_blocks(lambda r0, size: block(r0, size, False, True))
            for_blocks(lambda r0, size: out_copy(r0, size).wait())

    @pl.when(jnp.logical_and(i == pl.num_programs(0) - 1, j == n_j - 1))
    def _():
        tail0 = is_ref[pl.num_programs(0)]
        n_tail = (ys_hbm.shape[0] - tail0) // ROW_BLOCK
        xbuf[0:ROW_BLOCK, :] = jnp.zeros((ROW_BLOCK, xbuf.shape[1]), jnp.uint32)

        def tail_copy(c):
            g0 = pl.multiple_of(tail0 + c * ROW_BLOCK, ROW_BLOCK)
            return pltpu.make_async_copy(xbuf.at[pl.ds(0, ROW_BLOCK), :],
                                         ys_hbm.at[pl.ds(g0, ROW_BLOCK), :], sem_out)

        def tail_start(c, carry):
            tail_copy(c).start()
            return carry

        def tail_wait(c, carry):
            tail_copy(c).wait()
            return carry

        lax.fori_loop(0, n_tail, tail_start, 0)
        lax.fori_loop(0, n_tail, tail_wait, 0)


def _experts(xs, item_e, item_start, item_rows, w_gate, w_up, w_down, b_gate, b_up, b_down):
    n_slots, dh = xs.shape
    n_exp, d, d_ff = w_gate.shape
    n_items = item_e.shape[0]
    n_j = d_ff // FF_TILE
    assert n_j >= 2 and d == 2 * dh

    def jj(i, j, ir):
        return jnp.where(ir[i] > 0, j, n_j - 1)

    grid_spec = pltpu.PrefetchScalarGridSpec(
        num_scalar_prefetch=3,
        grid=(n_items, n_j),
        in_specs=[pl.BlockSpec(memory_space=pl.ANY),
                  pl.BlockSpec((None, d, FF_TILE), lambda i, j, ie, is_, ir: (ie[i], 0, jj(i, j, ir))),
                  pl.BlockSpec((None, d, FF_TILE), lambda i, j, ie, is_, ir: (ie[i], 0, jj(i, j, ir))),
                  pl.BlockSpec((None, FF_TILE, d), lambda i, j, ie, is_, ir: (ie[i], jj(i, j, ir), 0)),
                  pl.BlockSpec((n_exp * n_j, FF_TILE), lambda i, j, ie, is_, ir: (0, 0)),
                  pl.BlockSpec((n_exp * n_j, FF_TILE), lambda i, j, ie, is_, ir: (0, 0)),
                  pl.BlockSpec((n_exp, d), lambda i, j, ie, is_, ir: (0, 0))],
        out_specs=pl.BlockSpec(memory_space=pl.ANY),
        scratch_shapes=[pltpu.VMEM((ITEM_ROWS, dh), jnp.uint32),
                        pltpu.VMEM((ITEM_ROWS, d), F32),
                        pltpu.VMEM((d, 2 * FF_TILE), BF16),
                        pltpu.VMEM((FF_TILE, d), BF16),
                        pltpu.SemaphoreType.DMA(()),
                        pltpu.SemaphoreType.DMA(())],
    )
    return pl.pallas_call(
        _expert_kernel,
        grid_spec=grid_spec,
        out_shape=jax.ShapeDtypeStruct((n_slots, dh), jnp.uint32),
        compiler_params=_cparams(("arbitrary", "arbitrary"), 58),
        name="experts",
    )(item_e, item_start, item_rows, xs, w_gate, w_up, w_down,
      b_gate.reshape(n_exp * n_j, FF_TILE), b_up.reshape(n_exp * n_j, FF_TILE), b_down)


def _combine_kernel(*refs, alpha, tm, n_pt):
    dcur_ref = refs[:TOP_K]
    dnext_ref = refs[TOP_K:2 * TOP_K]
    (ys_hbm, x1_ref, gate_ref, p_ref, wpg_ref, wpp_ref, l2g_ref, l2b_ref, l3g_ref, l3b_ref,
     op_ref, os_ref, gbuf_a, gbuf_b, wpgb_ref, sems) = refs[2 * TOP_K:]
    i = pl.program_id(0)
    n_i = pl.num_programs(0)

    @pl.when(i == 0)
    def _():
        wpgb_ref[...] = wpg_ref[...].astype(BF16)

    def row_copy(dref, t, k, buf, sem):
        return pltpu.make_async_copy(ys_hbm.at[pl.ds(dref[k][t], 1), :],
                                     buf.at[k, pl.ds(t, 1), :], sem)

    def gather_loop(dref, buf, sem, wait):
        def body(t, c):
            for k in range(TOP_K):
                cp = row_copy(dref, t, k, buf, sem)
                if wait:
                    cp.wait()
                else:
                    cp.start(priority=k % DMA_THREADS)
            return c
        lax.fori_loop(0, tm, body, 0, unroll=4)

    def step(cur, cur_sem, nxt, nxt_sem):
        @pl.when(i == 0)
        def _():
            gather_loop(dcur_ref, cur, cur_sem, False)

        for t in range(tm):
            for k in range(TOP_K):
                row_copy(dcur_ref, t, k, cur, cur_sem).wait()

        for t in range(tm):
            for k in range(TOP_K):
                row_copy(dnext_ref, t, k, nxt, nxt_sem).start(priority=k % DMA_THREADS)

        x1 = x1_ref[...]
        lane = lax.broadcasted_iota(jnp.int32, gate_ref.shape, 1)
        gates = gate_ref[...]
        half = x1.shape[1] // 2
        ff_lo = jnp.zeros((tm, half), F32)
        ff_hi = jnp.zeros((tm, half), F32)
        for k in range(TOP_K):
            gk = jnp.sum(jnp.where(lane == k, gates, 0.0), axis=1, keepdims=True)
            y_lo, y_hi = _unpack_bf16_pairs(cur[k])
            ff_lo = ff_lo + gk * y_lo
            ff_hi = ff_hi + gk * y_hi
        ff = jnp.concatenate([ff_lo, ff_hi], axis=1)
        x2 = _layer_norm(alpha * x1 + ff, l2g_ref[...], l2b_ref[...])
        eg = _sigmoid(jnp.dot(x2.astype(BF16), wpgb_ref[...], preferred_element_type=F32))
        ep = jnp.dot(p_ref[...].astype(BF16), wpp_ref[...], preferred_element_type=F32)
        out = _layer_norm(alpha * x2 + eg * ep, l3g_ref[...], l3b_ref[...])

        @pl.when(i < n_pt)
        def _():
            op_ref[...] = out

        @pl.when(i >= n_pt)
        def _():
            os_ref[...] = out

        @pl.when(i == n_i - 1)
        def _():
            gather_loop(dnext_ref, nxt, nxt_sem, True)

    @pl.when(lax.rem(i, 2) == 0)
    def _():
        step(gbuf_a, sems.at[0], gbuf_b, sems.at[1])

    @pl.when(lax.rem(i, 2) == 1)
    def _():
        step(gbuf_b, sems.at[1], gbuf_a, sems.at[0])


def _combine(ys, dest, x1, gates, p, w_pg, w_pp, l2_g, l2_b, l3_g, l3_b, alpha, tm, n_prompt):
    n, d = x1.shape
    n_i = n // tm
    n_pt = n_prompt // tm
    assert n_prompt % tm == 0 and n % tm == 0
    row = lambda i: (i, 0)
    fixed = lambda i: (0, 0)
    kern = functools.partial(_combine_kernel, alpha=alpha, tm=tm, n_pt=n_pt)
    return pl.pallas_call(
        kern,
        grid=(n_i,),
        in_specs=[pl.BlockSpec((tm,), lambda i: (i,), memory_space=pltpu.SMEM)] * TOP_K + [
                  pl.BlockSpec((tm,), lambda i: (jnp.minimum(i + 1, n_i - 1),),
                               memory_space=pltpu.SMEM)] * TOP_K + [
                  pl.BlockSpec(memory_space=pl.ANY),
                  pl.BlockSpec((tm, d), row),
                  pl.BlockSpec((tm, LANES), row),
                  pl.BlockSpec((tm, p.shape[1]), row),
                  pl.BlockSpec(w_pg.shape, fixed, pipeline_mode=pl.Buffered(1)),
                  pl.BlockSpec(w_pp.shape, fixed),
                  pl.BlockSpec((1, d), fixed), pl.BlockSpec((1, d), fixed),
                  pl.BlockSpec((1, d), fixed), pl.BlockSpec((1, d), fixed)],
        out_specs=[pl.BlockSpec((tm, d), lambda i: (jnp.minimum(i, n_pt - 1), 0)),
                   pl.BlockSpec((tm, d), lambda i: (jnp.maximum(i - n_pt, 0), 0))],
        out_shape=[jax.ShapeDtypeStruct((n_prompt, d), F32),
                   jax.ShapeDtypeStruct((n - n_prompt, d), F32)],
        scratch_shapes=[pltpu.VMEM((TOP_K, tm, d // 2), jnp.uint32),
                        pltpu.VMEM((TOP_K, tm, d // 2), jnp.uint32),
                        pltpu.VMEM(w_pg.shape, BF16), pltpu.SemaphoreType.DMA((2,))],
        compiler_params=_cparams(("arbitrary",), 48),
        name="combine",
    )(*dest, *dest, ys, x1, gates, p, w_pg, w_pp, l2_g, l2_b, l3_g, l3_b)


def _routing_tables(expert_idx, rank, counts, n_items):
    n_exp = counts.shape[0]
    padded = (counts + ROW_BLOCK - 1) // ROW_BLOCK * ROW_BLOCK
    pad_end = jnp.cumsum(padded)
    pad_start = pad_end - padded
    experts = jnp.arange(n_exp, dtype=expert_idx.dtype)[:, None, None]
    group0 = jnp.sum(jnp.where(expert_idx[None] == experts, pad_start[:, None, None], 0), axis=0)
    dest = (group0 + rank).astype(jnp.int32)
    dest = tuple(dest[k] for k in range(TOP_K))
    items_per = (padded + ITEM_ROWS - 1) // ITEM_ROWS
    item_end = jnp.cumsum(items_per)
    ids = jnp.arange(n_items, dtype=jnp.int32)
    total = item_end[-1]
    last_valid = jnp.maximum(total - 1, 0)
    eff = jnp.minimum(ids, last_valid)
    e_of = jnp.minimum(jnp.searchsorted(item_end, eff, side='right'), n_exp - 1).astype(jnp.int32)
    sub = eff - (item_end[e_of] - items_per[e_of])
    start = pad_start[e_of] + sub * ITEM_ROWS
    rows = jnp.clip(padded[e_of] - sub * ITEM_ROWS, 0, ITEM_ROWS)
    rows = jnp.where(ids < total, rows, 0)
    start = jnp.concatenate([start, pad_end[-1:]])
    return (dest, (pad_start + counts).astype(jnp.int32), pad_end.astype(jnp.int32),
            e_of, start.astype(jnp.int32), rows.astype(jnp.int32))


def kernel(x_prompt, x_sample, state_conv, state_rec, p_prompt, p_sample, ln_in_g, ln_in_b, w_in, conv_w,
           lb_theta, rms_g, w_out, ln1_g, ln1_b, w_router, b_router, w_gate, b_gate, w_up, b_up, w_down,
           b_down, ln2_g, ln2_b, w_ple_gate, w_ple_proj, ln3_g, ln3_b):
    bp, sp, d = x_prompt.shape
    bs, ss, _ = x_sample.shape
    depth = w_in.shape[0]
    assert depth == 1 and ss == 1 and sp % CHUNK == 0
    d_conv = state_conv.shape[-1]
    n_exp = w_router.shape[-1]
    n_p = bp * sp
    n = n_p + bs
    alpha = (2 * depth) ** 0.25
    row2 = lambda a: a.reshape(1, -1)

    lb = jnp.cumsum(jax.nn.softmax(lb_theta.astype(F32), axis=0), axis=0)[0]
    x_p = x_prompt.reshape(n_p, d)
    x_s = x_sample.reshape(bs, d)
    p_all = jnp.concatenate([p_prompt[0].reshape(n_p, -1), p_sample[0].reshape(bs, -1)], axis=0)

    xn_p = _ln_in(x_p, row2(ln_in_g), row2(ln_in_b), tm=512)
    proj_p = _in_proj(xn_p, w_in[0], tm=1024, tn=1024)
    proj_s = _in_proj_sample(x_s, row2(ln_in_g), row2(ln_in_b), w_in[0], tn=1024)

    mix_p, conv_tail, rec_p = _mix_prompt(proj_p, conv_w[0], row2(lb), row2(rms_g[0]), bp, sp, d_conv)
    mix_s, conv_s, rec_s = _mix_sample(proj_s, state_conv[0].reshape(bs, -1), state_rec[0], conv_w[0],
                                       row2(lb), row2(rms_g[0]), d_conv)

    w_r = jnp.zeros((d, LANES), F32).at[:, :n_exp].set(w_router[0])
    b_r = jnp.full((1, LANES), NEG_BIG, F32).at[0, :n_exp].set(b_router[0])
    w_r_hi = w_r.astype(BF16)
    w_r2 = jnp.concatenate([w_r_hi, (w_r - w_r_hi.astype(F32)).astype(BF16)], axis=1)
    x1, x1p, route, gates, counts = _post_mix(
        mix_p, mix_s, x_p, x_s, w_out[0], row2(ln_in_g), row2(ln_in_b),
        row2(ln1_g[0]), row2(ln1_b[0]), w_r, w_r2, b_r, alpha, tm=128)

    n_slots = n * TOP_K + n_exp * ROW_BLOCK
    n_items = n_exp + n_slots // ITEM_ROWS
    dest, pad_lo, pad_hi, item_e, item_start, item_rows = _routing_tables(
        route[0:TOP_K], route[TOP_K:2 * TOP_K], counts[0, :n_exp], n_items)

    xs = _dispatch(x1p, dest, pad_lo, pad_hi, n_slots, tm=128)
    ys = _experts(xs, item_e, item_start, item_rows, w_gate[0], w_up[0], w_down[0],
                  b_gate[0], b_up[0], b_down[0])
    y_p, y_s = _combine(ys, dest, x1, gates, p_all, w_ple_gate[0], w_ple_proj[0].astype(BF16),
                        row2(ln2_g[0]), row2(ln2_b[0]), row2(ln3_g[0]), row2(ln3_b[0]), alpha, tm=128,
                        n_prompt=n_p)

    return (y_p.reshape(bp, sp, d),
            y_s.reshape(bs, ss, d),
            conv_tail[:, SUBLANES - (CONV_W - 1):, :][None],
            rec_p[None],
            conv_s.reshape(bs, CONV_W - 1, d_conv)[None],
            rec_s[None])
```

```python
import functools

import numpy as np
import jax
import jax.numpy as jnp
from jax import lax
from jax.experimental import pallas as pl
from jax.experimental.pallas import tpu as pltpu

F32 = jnp.float32
BF16 = jnp.bfloat16
HIGHEST = lax.Precision.HIGHEST

CONV_W = 3
N_HEADS = 8
HEAD_K = 128
HEAD_V = 128
TOP_K = 4
SWIGLU_LIMIT = 7.0
SWIGLU_ALPHA = 1.702
LN_EPS = 1e-5
RMS_EPS = 1e-6

LANES = 128
SUBLANES = 8
DMA_THREADS = 2

CHUNK = 128
SEQS_PER_STEP = 2
SAMPLE_BLOCK = 16
ROW_BLOCK = 128
MAX_BLOCK = 512
ITEM_ROWS = 1280
FF_TILE = 512
NEG_BIG = -1e30


def _cparams(sem, vmem_mb):
    return pltpu.CompilerParams(dimension_semantics=sem, vmem_limit_bytes=vmem_mb * 1024 * 1024)


def _layer_norm(x, g, b):
    mu = jnp.mean(x, axis=-1, keepdims=True)
    xc = x - mu
    var = jnp.mean(xc * xc, axis=-1, keepdims=True)
    return xc * lax.rsqrt(var + LN_EPS) * g + b


def _sigmoid(x):
    return 1.0 / (1.0 + jnp.exp(-x))


def _pack_bf16_pairs(x):
    half = x.shape[1] // 2
    bits = pltpu.bitcast(x.astype(BF16).astype(F32), jnp.uint32)
    return (bits[:, half:] & jnp.uint32(0xFFFF0000)) | (bits[:, :half] >> 16)


def _unpack_bf16_pairs(u):
    return (pltpu.bitcast(u << 16, F32), pltpu.bitcast(u & jnp.uint32(0xFFFF0000), F32))


def _split3(x, axis):
    p1 = x.astype(BF16)
    r1 = x - p1.astype(F32)
    p2 = r1.astype(BF16)
    p3 = (r1 - p2.astype(F32)).astype(BF16)
    return jnp.concatenate([p1, p2, p3], axis=axis)


def _ln_in_kernel(x_ref, g_ref, b_ref, o_ref):
    o_ref[...] = _layer_norm(x_ref[...], g_ref[...], b_ref[...]).astype(BF16)


def _ln_in(x, g, b, tm):
    n, d = x.shape
    return pl.pallas_call(
        _ln_in_kernel,
        grid=(n // tm,),
        in_specs=[pl.BlockSpec((tm, d), lambda i: (i, 0)),
                  pl.BlockSpec((1, d), lambda i: (0, 0)),
                  pl.BlockSpec((1, d), lambda i: (0, 0))],
        out_specs=pl.BlockSpec((tm, d), lambda i: (i, 0)),
        out_shape=jax.ShapeDtypeStruct((n, d), BF16),
        compiler_params=_cparams(("parallel",), 40),
        name="ln_in",
    )(x, g, b)


def _matmul_kernel(x_ref, w_ref, o_ref, wb_ref):
    @pl.when(pl.program_id(1) == 0)
    def _():
        wb_ref[...] = w_ref[...].astype(BF16)

    o_ref[...] = jnp.dot(x_ref[...], wb_ref[...], preferred_element_type=F32)


def _in_proj(xn, w, tm, tn):
    n, d = xn.shape
    d_in = w.shape[1]
    return pl.pallas_call(
        _matmul_kernel,
        grid=(d_in // tn, n // tm),
        in_specs=[pl.BlockSpec((tm, d), lambda j, i: (i, 0)),
                  pl.BlockSpec((d, tn), lambda j, i: (0, j))],
        out_specs=pl.BlockSpec((tm, tn), lambda j, i: (i, j)),
        out_shape=jax.ShapeDtypeStruct((n, d_in), F32),
        scratch_shapes=[pltpu.VMEM((d, tn), BF16)],
        compiler_params=_cparams(("arbitrary", "arbitrary"), 48),
        name="in_proj",
    )(xn, w)


def _in_proj_sample_kernel(x_ref, g_ref, b_ref, w_ref, o_ref):
    xn = _layer_norm(x_ref[...], g_ref[...], b_ref[...])
    o_ref[...] = jnp.dot(xn, w_ref[...], precision=HIGHEST, preferred_element_type=F32)


def _in_proj_sample(x, g, b, w, tn):
    n, d = x.shape
    d_in = w.shape[1]
    return pl.pallas_call(
        _in_proj_sample_kernel,
        grid=(d_in // tn,),
        in_specs=[pl.BlockSpec((n, d), lambda j: (0, 0)),
                  pl.BlockSpec((1, d), lambda j: (0, 0)),
                  pl.BlockSpec((1, d), lambda j: (0, 0)),
                  pl.BlockSpec((d, tn), lambda j: (0, j))],
        out_specs=pl.BlockSpec((n, tn), lambda j: (0, j)),
        out_shape=jax.ShapeDtypeStruct((n, d_in), F32),
        compiler_params=_cparams(("parallel",), 40),
        name="in_proj_sample",
    )(x, g, b, w)


def _forget_gates(fz, lb):
    e = jnp.exp(-jnp.abs(fz))
    r = 1.0 / (1.0 + e)
    er = e * r
    pos = fz >= 0
    sig_p = jnp.where(pos, r, er)
    sig_n = jnp.where(pos, er, r)
    oml = 1.0 - lb
    return lb + oml * sig_p, oml * sig_n


def _chunk_matrices(c):
    t = np.arange(c)[:, None]
    j = np.arange(c)[None, :]
    mats = [(j <= t), (j > t)]
    blk = c
    while blk >= 2:
        half = blk // 2
        mid = (t // blk) * blk + half
        second = (t % blk) >= half
        m_q = (j >= mid) & (j <= t)
        m_k = (j > t) & (j < mid)
        mats.append(np.where(second, m_q, m_k))
        blk = half
    return np.concatenate(mats, axis=0).astype(np.float32)


def _mix_prompt_kernel(proj_ref, convw_ref, lb_ref, rmsg_ref, cmat_ref,
                       mix_ref, convst_ref, recst_ref, s_ref, carry_ref, *, d_conv):
    tb = pl.program_id(1)

    @pl.when(tb == 0)
    def _():
        s_ref[...] = jnp.zeros_like(s_ref)
        carry_ref[...] = jnp.zeros_like(carry_ref)

    for s in range(proj_ref.shape[0]):
        _mix_prompt_chunk(proj_ref.at[s], convw_ref, lb_ref, rmsg_ref, cmat_ref, mix_ref.at[s],
                          convst_ref.at[s], s_ref.at[s], carry_ref.at[s], d_conv=d_conv)

    @pl.when(tb == pl.num_programs(1) - 1)
    def _():
        recst_ref[...] = s_ref[...]


def _mix_prompt_chunk(proj_ref, convw_ref, lb_ref, rmsg_ref, cmat_ref, mix_ref, convst_ref, s_ref,
                      carry_ref, *, d_conv):
    c = CHUNK
    u = proj_ref[:, 0:d_conv] * proj_ref[:, 2 * d_conv:3 * d_conv]
    row = lax.broadcasted_iota(jnp.int32, u.shape, 0)
    prev1 = carry_ref[SUBLANES - 1:SUBLANES, :]
    prev2 = carry_ref[SUBLANES - 2:SUBLANES - 1, :]
    u1 = jnp.where(row == 0, prev1, pltpu.roll(u, 1, 0))
    u2 = jnp.where(row == 0, prev2, jnp.where(row == 1, prev1, pltpu.roll(u, 2, 0)))
    y = convw_ref[0:1, :] * u2 + convw_ref[1:2, :] * u1 + convw_ref[2:3, :] * u
    mix_ref[:, 0:d_conv] = (proj_ref[:, d_conv:2 * d_conv] * y).astype(BF16)
    carry_ref[...] = u[c - SUBLANES:c, :]
    convst_ref[...] = u[c - SUBLANES:c, :]

    o0 = 3 * d_conv
    d_rec = N_HEADS * HEAD_K
    q = proj_ref[:, o0:o0 + d_rec]
    fz = proj_ref[:, o0 + d_rec:o0 + 2 * d_rec]
    v = proj_ref[:, o0 + 2 * d_rec:o0 + 3 * d_rec]
    g = proj_ref[:, o0 + 3 * d_rec:o0 + 4 * d_rec]
    f, kk = _forget_gates(fz, lb_ref[...])
    contract0 = (((0,), (0,)), ((), ()))
    logf3 = _split3(jnp.log(f), axis=0)
    ex = jnp.dot(cmat_ref[...], logf3, preferred_element_type=F32)
    b_cum = ex[0:c]
    d_end = ex[c:2 * c]
    n_lev = cmat_ref.shape[0] // c - 2
    b_cols = lax.dot_general(logf3, jnp.ones((3 * c, HEAD_V), BF16), contract0,
                             preferred_element_type=F32)

    trow = lax.broadcasted_iota(jnp.int32, (c, c), 0)
    tcol = lax.broadcasted_iota(jnp.int32, (c, c), 1)
    prow = lax.broadcasted_iota(jnp.int32, (c, HEAD_K), 0)
    contract1 = (((1,), (1,)), ((), ()))

    for h in range(N_HEADS):
        sl = slice(h * HEAD_K, (h + 1) * HEAD_K)
        qh, kh, vh = q[:, sl], kk[:, sl], v[:, sl]
        vb = vh.astype(BF16)
        s_old = s_ref[h]
        o = jnp.dot((qh * jnp.exp(b_cum[:, sl])).astype(BF16), s_old.astype(BF16),
                    preferred_element_type=F32)
        sc = jnp.zeros((c, c), F32)
        for lev in range(n_lev):
            blk = c >> lev
            sh = blk.bit_length() - 1
            dl = jnp.exp(ex[(2 + lev) * c:(3 + lev) * c, sl])
            second = (prow & (blk - 1)) >= (blk // 2)
            qt = jnp.where(second, qh * dl, 0.0).astype(BF16)
            kt = jnp.where(second, 0.0, kh * dl).astype(BF16)
            s_l = lax.dot_general(qt, kt, contract1, preferred_element_type=F32)
            sc = sc + jnp.where((trow >> sh) == (tcol >> sh), s_l, 0.0)
        o = o + jnp.dot(sc.astype(BF16), vb, preferred_element_type=F32)
        o = o + jnp.sum(qh * kh, axis=1, keepdims=True) * vh
        khat = (kh * jnp.exp(d_end[:, sl])).astype(BF16)
        upd = lax.dot_general(khat, vb, contract0, preferred_element_type=F32)
        s_ref[h] = jnp.exp(b_cols[sl, :]) * s_old + upd
        on = o * lax.rsqrt(jnp.mean(o * o, axis=1, keepdims=True) + RMS_EPS) * rmsg_ref[:, sl]
        gh = g[:, sl]
        mix_ref[:, d_conv + h * HEAD_V:d_conv + (h + 1) * HEAD_V] = (
            on * (gh * _sigmoid(gh))).astype(BF16)


def _mix_prompt(proj, conv_w, lb, rms_g, bsz, seq, d_conv):
    d_in = proj.shape[1]
    d_mix = d_conv + N_HEADS * HEAD_V
    n_tb = seq // CHUNK
    ns = SEQS_PER_STEP
    assert bsz % ns == 0
    cmat = jnp.asarray(np.tile(_chunk_matrices(CHUNK), (1, 3)), dtype=BF16)
    kern = functools.partial(_mix_prompt_kernel, d_conv=d_conv)
    mix, conv_tail, rec = pl.pallas_call(
        kern,
        grid=(bsz // ns, n_tb),
        in_specs=[pl.BlockSpec((ns, CHUNK, d_in), lambda b, t: (b, t, 0)),
                  pl.BlockSpec((CONV_W, d_conv), lambda b, t: (0, 0)),
                  pl.BlockSpec((1, N_HEADS * HEAD_K), lambda b, t: (0, 0)),
                  pl.BlockSpec((1, N_HEADS * HEAD_V), lambda b, t: (0, 0)),
                  pl.BlockSpec(cmat.shape, lambda b, t: (0, 0))],
        out_specs=[pl.BlockSpec((ns, CHUNK, d_mix), lambda b, t: (b, t, 0)),
                   pl.BlockSpec((ns, SUBLANES, d_conv), lambda b, t: (b, 0, 0)),
                   pl.BlockSpec((ns, N_HEADS, HEAD_K, HEAD_V), lambda b, t: (b, 0, 0, 0))],
        out_shape=[jax.ShapeDtypeStruct((bsz, seq, d_mix), BF16),
                   jax.ShapeDtypeStruct((bsz, SUBLANES, d_conv), F32),
                   jax.ShapeDtypeStruct((bsz, N_HEADS, HEAD_K, HEAD_V), F32)],
        scratch_shapes=[pltpu.VMEM((ns, N_HEADS, HEAD_K, HEAD_V), F32),
                        pltpu.VMEM((ns, SUBLANES, d_conv), F32)],
        compiler_params=_cparams(("parallel", "arbitrary"), 48),
        name="mix_prompt",
    )(proj.reshape(bsz, seq, d_in), conv_w, lb, rms_g, cmat)
    return mix.reshape(bsz * seq, d_mix), conv_tail, rec


def _mix_sample_kernel(proj_ref, cst_ref, rst_ref, convw_ref, lb_ref, rmsg_ref, sel_ref,
                       mix_ref, cnew_ref, rnew_ref, *, d_conv):
    nb = SAMPLE_BLOCK
    u = proj_ref[:, 0:d_conv] * proj_ref[:, 2 * d_conv:3 * d_conv]
    buf0 = cst_ref[:, 0:d_conv]
    buf1 = cst_ref[:, d_conv:2 * d_conv]
    y = convw_ref[0:1, :] * buf0 + convw_ref[1:2, :] * buf1 + convw_ref[2:3, :] * u
    mix_ref[:, 0:d_conv] = proj_ref[:, d_conv:2 * d_conv] * y
    cnew_ref[:, 0:d_conv] = buf1
    cnew_ref[:, d_conv:2 * d_conv] = u

    o0 = 3 * d_conv
    d_rec = N_HEADS * HEAD_K
    q = proj_ref[:, o0:o0 + d_rec]
    fz = proj_ref[:, o0 + d_rec:o0 + 2 * d_rec]
    v = proj_ref[:, o0 + 2 * d_rec:o0 + 3 * d_rec]
    g = proj_ref[:, o0 + 3 * d_rec:o0 + 4 * d_rec]
    f, kk = _forget_gates(fz, lb_ref[...])
    contract0 = (((0,), (0,)), ((), ()))
    sel = sel_ref[...]
    row = lax.broadcasted_iota(jnp.int32, (nb, HEAD_V), 0)

    def columns(a):
        return lax.dot_general(_split3(a, axis=0), sel, contract0, preferred_element_type=F32)

    for h in range(N_HEADS):
        sl = slice(h * HEAD_K, (h + 1) * HEAD_K)
        f_c, k_c, q_c = columns(f[:, sl]), columns(kk[:, sl]), columns(q[:, sl])
        o = jnp.zeros((nb, HEAD_V), F32)
        for n in range(nb):
            nl = slice(n * HEAD_V, (n + 1) * HEAD_V)
            s_new = f_c[:, nl] * rst_ref[n, h] + k_c[:, nl] * v[n:n + 1, sl]
            rnew_ref[n, h] = s_new
            o_row = jnp.sum(q_c[:, nl] * s_new, axis=0, keepdims=True)
            o = jnp.where(row == n, o_row, o)
        on = o * lax.rsqrt(jnp.mean(o * o, axis=1, keepdims=True) + RMS_EPS) * rmsg_ref[:, sl]
        gh = g[:, sl]
        mix_ref[:, d_conv + h * HEAD_V:d_conv + (h + 1) * HEAD_V] = on * (gh * _sigmoid(gh))


def _mix_sample(proj, conv_state, rec_state, conv_w, lb, rms_g, d_conv):
    n_seq = conv_state.shape[0]
    d_in = proj.shape[1]
    d_mix = d_conv + N_HEADS * HEAD_V
    nb = SAMPLE_BLOCK
    sel = jnp.asarray(np.tile(np.kron(np.eye(nb), np.ones((1, HEAD_V))), (3, 1)), dtype=BF16)
    kern = functools.partial(_mix_sample_kernel, d_conv=d_conv)
    return pl.pallas_call(
        kern,
        grid=(n_seq // nb,),
        in_specs=[pl.BlockSpec((nb, d_in), lambda i: (i, 0)),
                  pl.BlockSpec((nb, 2 * d_conv), lambda i: (i, 0)),
                  pl.BlockSpec((nb, N_HEADS, HEAD_K, HEAD_V), lambda i: (i, 0, 0, 0)),
                  pl.BlockSpec((CONV_W, d_conv), lambda i: (0, 0)),
                  pl.BlockSpec((1, N_HEADS * HEAD_K), lambda i: (0, 0)),
                  pl.BlockSpec((1, N_HEADS * HEAD_V), lambda i: (0, 0)),
                  pl.BlockSpec(sel.shape, lambda i: (0, 0))],
        out_specs=[pl.BlockSpec((nb, d_mix), lambda i: (i, 0)),
                   pl.BlockSpec((nb, 2 * d_conv), lambda i: (i, 0)),
                   pl.BlockSpec((nb, N_HEADS, HEAD_K, HEAD_V), lambda i: (i, 0, 0, 0))],
        out_shape=[jax.ShapeDtypeStruct((n_seq, d_mix), F32),
                   jax.ShapeDtypeStruct((n_seq, 2 * d_conv), F32),
                   jax.ShapeDtypeStruct(rec_state.shape, F32)],
        compiler_params=_cparams(("parallel",), 52),
        name="mix_sample",
    )(proj, conv_state, rec_state, conv_w, lb, rms_g, sel)


def _post_mix_kernel(mixp_ref, mixs_ref, xp_ref, xs_ref, woutf_ref, ling_ref, linb_ref,
                     l1g_ref, l1b_ref, wr_ref, wr2_ref, br_ref,
                     x1_ref, x1p_ref, route_ref, gate_ref, cnt_ref, run_ref, h_ref, lg_ref, woutb_ref,
                     *, alpha, n_pt):
    i = pl.program_id(0)

    @pl.when(i == 0)
    def _():
        run_ref[...] = jnp.zeros_like(run_ref)
        woutb_ref[...] = woutf_ref[...].astype(BF16)

    @pl.when(i < n_pt)
    def _():
        h_ref[...] = jnp.dot(mixp_ref[...], woutb_ref[...], preferred_element_type=F32)

    @pl.when(i >= n_pt)
    def _():
        h_ref[...] = jnp.dot(mixs_ref[...], woutf_ref[...], precision=HIGHEST,
                             preferred_element_type=F32)

    x = jnp.where(i < n_pt, xp_ref[...], xs_ref[...])
    xn = _layer_norm(x, ling_ref[...], linb_ref[...])
    x1 = _layer_norm(alpha * xn + h_ref[...], l1g_ref[...], l1b_ref[...])
    x1_ref[...] = x1
    x1p_ref[...] = _pack_bf16_pairs(x1)

    tm = x1.shape[0]

    @pl.when(i < n_pt)
    def _():
        xh = x1.astype(BF16)
        xl = (x1 - xh.astype(F32)).astype(BF16)
        pr = jnp.dot(jnp.concatenate([xh, xl], axis=0), wr2_ref[...], preferred_element_type=F32)
        lg_ref[...] = (pr[0:tm, 0:LANES] + pr[0:tm, LANES:2 * LANES]
                       + pr[tm:2 * tm, 0:LANES] + pr[tm:2 * tm, LANES:2 * LANES])

    @pl.when(i >= n_pt)
    def _():
        lg_ref[...] = jnp.dot(x1, wr_ref[...], precision=HIGHEST, preferred_element_type=F32)

    logits = lg_ref[...] + br_ref[...]
    lane = lax.broadcasted_iota(jnp.int32, (tm, LANES), 1)
    lane_f = lane.astype(F32)
    work = logits
    vals, idxs = [], []
    for _ in range(TOP_K):
        m = jnp.max(work, axis=1, keepdims=True)
        ix = jnp.min(jnp.where(work == m, lane_f, float(LANES)), axis=1, keepdims=True)
        vals.append(m)
        idxs.append(ix)
        work = jnp.where(lane_f == ix, NEG_BIG, work)
    ex = [jnp.exp(vv - vals[0]) for vv in vals]
    den = ex[0] + ex[1] + ex[2] + ex[3]
    onehots = [(lane_f == ix).astype(F32) for ix in idxs]
    oh = onehots[0] + onehots[1] + onehots[2] + onehots[3]
    tr = lax.broadcasted_iota(jnp.int32, (tm, tm), 0)
    tc = lax.broadcasted_iota(jnp.int32, (tm, tm), 1)
    before = jnp.dot((tc < tr).astype(BF16), oh.astype(BF16), preferred_element_type=F32)
    pos = before + run_ref[...]
    route = jnp.zeros((tm, LANES), F32)
    gates = jnp.zeros((tm, LANES), F32)
    for k in range(TOP_K):
        rank = jnp.sum(onehots[k] * pos, axis=1, keepdims=True)
        route = jnp.where(lane == k, idxs[k], route)
        route = jnp.where(lane == TOP_K + k, rank, route)
        gates = jnp.where(lane == k, ex[k] / den, gates)
    route_ref[...] = route.T[0:2 * TOP_K, :].astype(jnp.int32)
    gate_ref[...] = gates
    run_ref[...] = run_ref[...] + jnp.sum(oh, axis=0, keepdims=True)
    cnt_ref[...] = run_ref[...].astype(jnp.int32)


def _post_mix(mix_p, mix_s, x_p, x_s, w_out_f, lin_g, lin_b, l1_g, l1_b, w_r, w_r2, b_r, alpha, tm):
    d = x_p.shape[1]
    n = x_p.shape[0] + x_s.shape[0]
    d_mix = mix_p.shape[1]
    n_pt = mix_p.shape[0] // tm
    assert mix_p.shape[0] % tm == 0 and mix_s.shape[0] % tm == 0
    row = lambda i: (i, 0)
    fixed = lambda i: (0, 0)
    prompt_row = lambda i: (jnp.minimum(i, n_pt - 1), 0)
    sample_row = lambda i: (jnp.maximum(i - n_pt, 0), 0)
    once = pl.Buffered(1)
    kern = functools.partial(_post_mix_kernel, alpha=alpha, n_pt=n_pt)
    return pl.pallas_call(
        kern,
        grid=(n // tm,),
        in_specs=[pl.BlockSpec((tm, d_mix), prompt_row),
                  pl.BlockSpec((tm, d_mix), sample_row),
                  pl.BlockSpec((tm, d), prompt_row),
                  pl.BlockSpec((tm, d), sample_row),
                  pl.BlockSpec(w_out_f.shape, fixed, pipeline_mode=once),
                  pl.BlockSpec((1, d), fixed), pl.BlockSpec((1, d), fixed),
                  pl.BlockSpec((1, d), fixed), pl.BlockSpec((1, d), fixed),
                  pl.BlockSpec((d, LANES), fixed), pl.BlockSpec((d, 2 * LANES), fixed),
                  pl.BlockSpec((1, LANES), fixed)],
        out_specs=[pl.BlockSpec((tm, d), row),
                   pl.BlockSpec((tm, d // 2), row),
                   pl.BlockSpec((2 * TOP_K, tm), lambda i: (0, i)),
                   pl.BlockSpec((tm, LANES), row),
                   pl.BlockSpec((1, LANES), fixed)],
        out_shape=[jax.ShapeDtypeStruct((n, d), F32),
                   jax.ShapeDtypeStruct((n, d // 2), jnp.uint32),
                   jax.ShapeDtypeStruct((2 * TOP_K, n), jnp.int32),
                   jax.ShapeDtypeStruct((n, LANES), F32),
                   jax.ShapeDtypeStruct((1, LANES), jnp.int32)],
        scratch_shapes=[pltpu.VMEM((1, LANES), F32), pltpu.VMEM((tm, d), F32),
                        pltpu.VMEM((tm, LANES), F32), pltpu.VMEM(w_out_f.shape, BF16)],
        compiler_params=_cparams(("arbitrary",), 48),
        name="post_mix",
    )(mix_p, mix_s, x_p, x_s, w_out_f, lin_g, lin_b, l1_g, l1_b, w_r, w_r2, b_r)


def _dispatch_kernel(*refs, tm, n_experts):
    dest_refs = refs[:TOP_K]
    padlo_ref, padhi_ref, x_ref, xs_hbm, zero_ref, sem = refs[TOP_K:]
    i = pl.program_id(0)

    def row_copy(src, dst_row):
        return pltpu.make_async_copy(src, xs_hbm.at[pl.ds(dst_row, 1), :], sem)

    @pl.when(i == 0)
    def _():
        zero_ref[...] = jnp.zeros_like(zero_ref)

        def per_expert(e, carry):
            def start(r, c):
                row_copy(zero_ref.at[pl.ds(0, 1), :], r).start()
                return c

            def wait(r, c):
                row_copy(zero_ref.at[pl.ds(0, 1), :], r).wait()
                return c

            lax.fori_loop(padlo_ref[e], padhi_ref[e], start, 0)
            lax.fori_loop(padlo_ref[e], padhi_ref[e], wait, 0)
            return carry

        lax.fori_loop(0, n_experts, per_expert, 0)

        tail0 = padhi_ref[n_experts - 1]
        n_tail = (xs_hbm.shape[0] - tail0) // ROW_BLOCK

        def tail_copy(c):
            r0 = pl.multiple_of(tail0 + c * ROW_BLOCK, ROW_BLOCK)
            return pltpu.make_async_copy(zero_ref, xs_hbm.at[pl.ds(r0, ROW_BLOCK), :], sem)

        def tail_start(c, carry):
            tail_copy(c).start()
            return carry

        def tail_wait(c, carry):
            tail_copy(c).wait()
            return carry

        lax.fori_loop(0, n_tail, tail_start, 0)
        lax.fori_loop(0, n_tail, tail_wait, 0)

    copies = [row_copy(x_ref.at[pl.ds(t, 1), :], dest_refs[k][t]) for t in range(tm) for k in range(TOP_K)]
    for n, cp in enumerate(copies):
        cp.start(priority=n % DMA_THREADS)
    for cp in copies:
        cp.wait()


def _dispatch(x1p, dest, pad_lo, pad_hi, n_slots, tm):
    n, dh = x1p.shape
    n_experts = pad_lo.shape[0]
    kern = functools.partial(_dispatch_kernel, tm=tm, n_experts=n_experts)
    return pl.pallas_call(
        kern,
        grid=(n // tm,),
        in_specs=[pl.BlockSpec((tm,), lambda i: (i,), memory_space=pltpu.SMEM)] * TOP_K + [
                  pl.BlockSpec(memory_space=pltpu.SMEM),
                  pl.BlockSpec(memory_space=pltpu.SMEM),
                  pl.BlockSpec((tm, dh), lambda i: (i, 0))],
        out_specs=pl.BlockSpec(memory_space=pl.ANY),
        out_shape=jax.ShapeDtypeStruct((n_slots, dh), jnp.uint32),
        scratch_shapes=[pltpu.VMEM((ROW_BLOCK, dh), jnp.uint32), pltpu.SemaphoreType.DMA(())],
        compiler_params=_cparams(("arbitrary",), 32),
        name="dispatch",
    )(*dest, pad_lo, pad_hi, x1p)


def _expert_kernel(ie_ref, is_ref, ir_ref, xs_hbm, wg_ref, wu_ref, wd_ref, bg_ref, bu_ref, bd_ref,
                   ys_hbm, xbuf, ybuf, wgu_bf, wd_bf, sem_in, sem_out):
    i = pl.program_id(0)
    j = pl.program_id(1)
    n_j = pl.num_programs(1)
    rows = ir_ref[i]
    start = is_ref[i]
    tf = wg_ref.shape[1]
    b_row = ie_ref[i] * n_j + j
    bg = bg_ref[pl.ds(b_row, 1), :]
    bu = bu_ref[pl.ds(b_row, 1), :]
    bd = bd_ref[pl.ds(ie_ref[i], 1), :]

    def in_copy(r0, size):
        g0 = pl.multiple_of(start + r0, ROW_BLOCK)
        return pltpu.make_async_copy(xs_hbm.at[pl.ds(g0, size), :], xbuf.at[pl.ds(r0, size), :], sem_in)

    def out_copy(r0, size):
        g0 = pl.multiple_of(start + r0, ROW_BLOCK)
        return pltpu.make_async_copy(xbuf.at[pl.ds(r0, size), :], ys_hbm.at[pl.ds(g0, size), :], sem_out)

    def for_blocks(fn):
        n_big = rows // MAX_BLOCK

        def body(c, carry):
            fn(pl.multiple_of(c * MAX_BLOCK, MAX_BLOCK), MAX_BLOCK)
            return carry
        lax.fori_loop(0, n_big, body, 0)
        base = n_big * MAX_BLOCK
        size = MAX_BLOCK // 2
        while size >= ROW_BLOCK:
            has = (rows & size) != 0

            @pl.when(has)
            def _(base=base, size=size):
                fn(pl.multiple_of(base, ROW_BLOCK), size)
            base = base + jnp.where(has, size, 0)
            size //= 2

    @pl.when(rows > 0)
    def _():
        @pl.when(j == 0)
        def _():
            for_blocks(lambda r0, size: in_copy(r0, size).start())

        wgu_bf[:, 0:tf] = wg_ref[...].astype(BF16)
        wgu_bf[:, tf:2 * tf] = wu_ref[...].astype(BF16)
        wd_bf[...] = wd_ref[...].astype(BF16)

        @pl.when(j == 0)
        def _():
            for_blocks(lambda r0, size: in_copy(r0, size).wait())

        def block(r0, size, first, last):
            lo, hi = _unpack_bf16_pairs(xbuf[pl.ds(r0, size), :])
            x = jnp.concatenate([lo.astype(BF16), hi.astype(BF16)], axis=1)
            gu = jnp.dot(x, wgu_bf[...], preferred_element_type=F32)
            gg = jnp.minimum(gu[:, 0:tf] + bg, SWIGLU_LIMIT)
            uu = jnp.clip(gu[:, tf:2 * tf] + bu, -SWIGLU_LIMIT, SWIGLU_LIMIT)
            hid = gg * _sigmoid(SWIGLU_ALPHA * gg) * (uu + 1.0)
            y = jnp.dot(hid.astype(BF16), wd_bf[...], preferred_element_type=F32)
            if not first:
                y = y + ybuf[pl.ds(r0, size), :]
            if last:
                xbuf[pl.ds(r0, size), :] = _pack_bf16_pairs(y + bd)
                out_copy(r0, size).start()
            else:
                ybuf[pl.ds(r0, size), :] = y

        @pl.when(j == 0)
        def _():
            for_blocks(lambda r0, size: block(r0, size, True, False))

        @pl.when(jnp.logical_and(j > 0, j < n_j - 1))
        def _():
            for_blocks(lambda r0, size: block(r0, size, False, False))

        @pl.when(j == n_j - 1)
        def _():
            for_blocks(lambda r0, size: block(r0, size, False, True))
            for_blocks(lambda r0, size: out_copy(r0, size).wait())

    @pl.when(jnp.logical_and(i == pl.num_programs(0) - 1, j == n_j - 1))
    def _():
        tail0 = is_ref[pl.num_programs(0)]
        n_tail = (ys_hbm.shape[0] - tail0) // ROW_BLOCK
        xbuf[0:ROW_BLOCK, :] = jnp.zeros((ROW_BLOCK, xbuf.shape[1]), jnp.uint32)

        def tail_copy(c):
            g0 = pl.multiple_of(tail0 + c * ROW_BLOCK, ROW_BLOCK)
            return pltpu.make_async_copy(xbuf.at[pl.ds(0, ROW_BLOCK), :],
                                         ys_hbm.at[pl.ds(g0, ROW_BLOCK), :], sem_out)

        def tail_start(c, carry):
            tail_copy(c).start()
            return carry

        def tail_wait(c, carry):
            tail_copy(c).wait()
            return carry

        lax.fori_loop(0, n_tail, tail_start, 0)
        lax.fori_loop(0, n_tail, tail_wait, 0)


def _experts(xs, item_e, item_start, item_rows, w_gate, w_up, w_down, b_gate, b_up, b_down):
    n_slots, dh = xs.shape
    n_exp, d, d_ff = w_gate.shape
    n_items = item_e.shape[0]
    n_j = d_ff // FF_TILE
    assert n_j >= 2 and d == 2 * dh

    def jj(i, j, ir):
        return jnp.where(ir[i] > 0, j, n_j - 1)

    grid_spec = pltpu.PrefetchScalarGridSpec(
        num_scalar_prefetch=3,
        grid=(n_items, n_j),
        in_specs=[pl.BlockSpec(memory_space=pl.ANY),
                  pl.BlockSpec((None, d, FF_TILE), lambda i, j, ie, is_, ir: (ie[i], 0, jj(i, j, ir))),
                  pl.BlockSpec((None, d, FF_TILE), lambda i, j, ie, is_, ir: (ie[i], 0, jj(i, j, ir))),
                  pl.BlockSpec((None, FF_TILE, d), lambda i, j, ie, is_, ir: (ie[i], jj(i, j, ir), 0)),
                  pl.BlockSpec((n_exp * n_j, FF_TILE), lambda i, j, ie, is_, ir: (0, 0)),
                  pl.BlockSpec((n_exp * n_j, FF_TILE), lambda i, j, ie, is_, ir: (0, 0)),
                  pl.BlockSpec((n_exp, d), lambda i, j, ie, is_, ir: (0, 0))],
        out_specs=pl.BlockSpec(memory_space=pl.ANY),
        scratch_shapes=[pltpu.VMEM((ITEM_ROWS, dh), jnp.uint32),
                        pltpu.VMEM((ITEM_ROWS, d), F32),
                        pltpu.VMEM((d, 2 * FF_TILE), BF16),
                        pltpu.VMEM((FF_TILE, d), BF16),
                        pltpu.SemaphoreType.DMA(()),
                        pltpu.SemaphoreType.DMA(())],
    )
    return pl.pallas_call(
        _expert_kernel,
        grid_spec=grid_spec,
        out_shape=jax.ShapeDtypeStruct((n_slots, dh), jnp.uint32),
        compiler_params=_cparams(("arbitrary", "arbitrary"), 58),
        name="experts",
    )(item_e, item_start, item_rows, xs, w_gate, w_up, w_down,
      b_gate.reshape(n_exp * n_j, FF_TILE), b_up.reshape(n_exp * n_j, FF_TILE), b_down)


def _combine_kernel(*refs, alpha, tm, n_pt):
    dcur_ref = refs[:TOP_K]
    dnext_ref = refs[TOP_K:2 * TOP_K]
    (ys_hbm, x1_ref, gate_ref, p_ref, wpg_ref, wpp_ref, l2g_ref, l2b_ref, l3g_ref, l3b_ref,
     op_ref, os_ref, gbuf_a, gbuf_b, wpgb_ref, sems) = refs[2 * TOP_K:]
    i = pl.program_id(0)
    n_i = pl.num_programs(0)

    @pl.when(i == 0)
    def _():
        wpgb_ref[...] = wpg_ref[...].astype(BF16)

    def row_copy(dref, t, k, buf, sem):
        return pltpu.make_async_copy(ys_hbm.at[pl.ds(dref[k][t], 1), :],
                                     buf.at[k, pl.ds(t, 1), :], sem)

    def gather_loop(dref, buf, sem, wait):
        def body(t, c):
            for k in range(TOP_K):
                cp = row_copy(dref, t, k, buf, sem)
                if wait:
                    cp.wait()
                else:
                    cp.start(priority=k % DMA_THREADS)
            return c
        lax.fori_loop(0, tm, body, 0, unroll=4)

    def step(cur, cur_sem, nxt, nxt_sem):
        @pl.when(i == 0)
        def _():
            gather_loop(dcur_ref, cur, cur_sem, False)

        for t in range(tm):
            for k in range(TOP_K):
                row_copy(dcur_ref, t, k, cur, cur_sem).wait()

        for t in range(tm):
            for k in range(TOP_K):
                row_copy(dnext_ref, t, k, nxt, nxt_sem).start(priority=k % DMA_THREADS)

        x1 = x1_ref[...]
        lane = lax.broadcasted_iota(jnp.int32, gate_ref.shape, 1)
        gates = gate_ref[...]
        half = x1.shape[1] // 2
        ff_lo = jnp.zeros((tm, half), F32)
        ff_hi = jnp.zeros((tm, half), F32)
        for k in range(TOP_K):
            gk = jnp.sum(jnp.where(lane == k, gates, 0.0), axis=1, keepdims=True)
            y_lo, y_hi = _unpack_bf16_pairs(cur[k])
            ff_lo = ff_lo + gk * y_lo
            ff_hi = ff_hi + gk * y_hi
        ff = jnp.concatenate([ff_lo, ff_hi], axis=1)
        x2 = _layer_norm(alpha * x1 + ff, l2g_ref[...], l2b_ref[...])
        eg = _sigmoid(jnp.dot(x2.astype(BF16), wpgb_ref[...], preferred_element_type=F32))
        ep = jnp.dot(p_ref[...].astype(BF16), wpp_ref[...], preferred_element_type=F32)
        out = _layer_norm(alpha * x2 + eg * ep, l3g_ref[...], l3b_ref[...])

        @pl.when(i < n_pt)
        def _():
            op_ref[...] = out

        @pl.when(i >= n_pt)
        def _():
            os_ref[...] = out

        @pl.when(i == n_i - 1)
        def _():
            gather_loop(dnext_ref, nxt, nxt_sem, True)

    @pl.when(lax.rem(i, 2) == 0)
    def _():
        step(gbuf_a, sems.at[0], gbuf_b, sems.at[1])

    @pl.when(lax.rem(i, 2) == 1)
    def _():
        step(gbuf_b, sems.at[1], gbuf_a, sems.at[0])


def _combine(ys, dest, x1, gates, p, w_pg, w_pp, l2_g, l2_b, l3_g, l3_b, alpha, tm, n_prompt):
    n, d = x1.shape
    n_i = n // tm
    n_pt = n_prompt // tm
    assert n_prompt % tm == 0 and n % tm == 0
    row = lambda i: (i, 0)
    fixed = lambda i: (0, 0)
    kern = functools.partial(_combine_kernel, alpha=alpha, tm=tm, n_pt=n_pt)
    return pl.pallas_call(
        kern,
        grid=(n_i,),
        in_specs=[pl.BlockSpec((tm,), lambda i: (i,), memory_space=pltpu.SMEM)] * TOP_K + [
                  pl.BlockSpec((tm,), lambda i: (jnp.minimum(i + 1, n_i - 1),),
                               memory_space=pltpu.SMEM)] * TOP_K + [
                  pl.BlockSpec(memory_space=pl.ANY),
                  pl.BlockSpec((tm, d), row),
                  pl.BlockSpec((tm, LANES), row),
                  pl.BlockSpec((tm, p.shape[1]), row),
                  pl.BlockSpec(w_pg.shape, fixed, pipeline_mode=pl.Buffered(1)),
                  pl.BlockSpec(w_pp.shape, fixed),
                  pl.BlockSpec((1, d), fixed), pl.BlockSpec((1, d), fixed),
                  pl.BlockSpec((1, d), fixed), pl.BlockSpec((1, d), fixed)],
        out_specs=[pl.BlockSpec((tm, d), lambda i: (jnp.minimum(i, n_pt - 1), 0)),
                   pl.BlockSpec((tm, d), lambda i: (jnp.maximum(i - n_pt, 0), 0))],
        out_shape=[jax.ShapeDtypeStruct((n_prompt, d), F32),
                   jax.ShapeDtypeStruct((n - n_prompt, d), F32)],
        scratch_shapes=[pltpu.VMEM((TOP_K, tm, d // 2), jnp.uint32),
                        pltpu.VMEM((TOP_K, tm, d // 2), jnp.uint32),
                        pltpu.VMEM(w_pg.shape, BF16), pltpu.SemaphoreType.DMA((2,))],
        compiler_params=_cparams(("arbitrary",), 48),
        name="combine",
    )(*dest, *dest, ys, x1, gates, p, w_pg, w_pp, l2_g, l2_b, l3_g, l3_b)


def _routing_tables(expert_idx, rank, counts, n_items):
    n_exp = counts.shape[0]
    padded = (counts + ROW_BLOCK - 1) // ROW_BLOCK * ROW_BLOCK
    pad_end = jnp.cumsum(padded)
    pad_start = pad_end - padded
    experts = jnp.arange(n_exp, dtype=expert_idx.dtype)[:, None, None]
    group0 = jnp.sum(jnp.where(expert_idx[None] == experts, pad_start[:, None, None], 0), axis=0)
    dest = (group0 + rank).astype(jnp.int32)
    dest = tuple(dest[k] for k in range(TOP_K))
    items_per = (padded + ITEM_ROWS - 1) // ITEM_ROWS
    item_end = jnp.cumsum(items_per)
    ids = jnp.arange(n_items, dtype=jnp.int32)
    total = item_end[-1]
    last_valid = jnp.maximum(total - 1, 0)
    eff = jnp.minimum(ids, last_valid)
    e_of = jnp.minimum(jnp.searchsorted(item_end, eff, side='right'), n_exp - 1).astype(jnp.int32)
    sub = eff - (item_end[e_of] - items_per[e_of])
    start = pad_start[e_of] + sub * ITEM_ROWS
    rows = jnp.clip(padded[e_of] - sub * ITEM_ROWS, 0, ITEM_ROWS)
    rows = jnp.where(ids < total, rows, 0)
    start = jnp.concatenate([start, pad_end[-1:]])
    return (dest, (pad_start + counts).astype(jnp.int32), pad_end.astype(jnp.int32),
            e_of, start.astype(jnp.int32), rows.astype(jnp.int32))


def kernel(x_prompt, x_sample, state_conv, state_rec, p_prompt, p_sample, ln_in_g, ln_in_b, w_in, conv_w,
           lb_theta, rms_g, w_out, ln1_g, ln1_b, w_router, b_router, w_gate, b_gate, w_up, b_up, w_down,
           b_down, ln2_g, ln2_b, w_ple_gate, w_ple_proj, ln3_g, ln3_b):
    bp, sp, d = x_prompt.shape
    bs, ss, _ = x_sample.shape
    depth = w_in.shape[0]
    assert depth == 1 and ss == 1 and sp % CHUNK == 0
    d_conv = state_conv.shape[-1]
    n_exp = w_router.shape[-1]
    n_p = bp * sp
    n = n_p + bs
    alpha = (2 * depth) ** 0.25
    row2 = lambda a: a.reshape(1, -1)

    lb = jnp.cumsum(jax.nn.softmax(lb_theta.astype(F32), axis=0), axis=0)[0]
    x_p = x_prompt.reshape(n_p, d)
    x_s = x_sample.reshape(bs, d)
    p_all = jnp.concatenate([p_prompt[0].reshape(n_p, -1), p_sample[0].reshape(bs, -1)], axis=0)

    xn_p = _ln_in(x_p, row2(ln_in_g), row2(ln_in_b), tm=512)
    proj_p = _in_proj(xn_p, w_in[0], tm=1024, tn=1024)
    proj_s = _in_proj_sample(x_s, row2(ln_in_g), row2(ln_in_b), w_in[0], tn=1024)

    mix_p, conv_tail, rec_p = _mix_prompt(proj_p, conv_w[0], row2(lb), row2(rms_g[0]), bp, sp, d_conv)
    mix_s, conv_s, rec_s = _mix_sample(proj_s, state_conv[0].reshape(bs, -1), state_rec[0], conv_w[0],
                                       row2(lb), row2(rms_g[0]), d_conv)

    w_r = jnp.zeros((d, LANES), F32).at[:, :n_exp].set(w_router[0])
    b_r = jnp.full((1, LANES), NEG_BIG, F32).at[0, :n_exp].set(b_router[0])
    w_r_hi = w_r.astype(BF16)
    w_r2 = jnp.concatenate([w_r_hi, (w_r - w_r_hi.astype(F32)).astype(BF16)], axis=1)
    x1, x1p, route, gates, counts = _post_mix(
        mix_p, mix_s, x_p, x_s, w_out[0], row2(ln_in_g), row2(ln_in_b),
        row2(ln1_g[0]), row2(ln1_b[0]), w_r, w_r2, b_r, alpha, tm=128)

    n_slots = n * TOP_K + n_exp * ROW_BLOCK
    n_items = n_exp + n_slots // ITEM_ROWS
    dest, pad_lo, pad_hi, item_e, item_start, item_rows = _routing_tables(
        route[0:TOP_K], route[TOP_K:2 * TOP_K], counts[0, :n_exp], n_items)

    xs = _dispatch(x1p, dest, pad_lo, pad_hi, n_slots, tm=128)
    ys = _experts(xs, item_e, item_start, item_rows, w_gate[0], w_up[0], w_down[0],
                  b_gate[0], b_up[0], b_down[0])
    y_p, y_s = _combine(ys, dest, x1, gates, p_all, w_ple_gate[0], w_ple_proj[0].astype(BF16),
                        row2(ln2_g[0]), row2(ln2_b[0]), row2(ln3_g[0]), row2(ln3_b[0]), alpha, tm=128,
                        n_prompt=n_p)

    return (y_p.reshape(bp, sp, d),
            y_s.reshape(bs, ss, d),
            conv_tail[:, SUBLANES - (CONV_W - 1):, :][None],
            rec_p[None],
            conv_s.reshape(bs, CONV_W - 1, d_conv)[None],
            rec_s[None])
```
